```python
import jax, jax.numpy as jnp
from jax import lax
import numpy as np

D_MODEL = 1024
BATCH = 8
SEQ = 4096
DEPTH = 2

RMS_EPS = 1e-5
N_BRANCHES = 3
GMLP_WIDTH = D_MODEL
GMLP_HEADS = 8
GMLP_HEAD_DIM = GMLP_WIDTH // GMLP_HEADS
CHUNK = 128
POOL_WIDTH = D_MODEL
POOL_WINDOWS = (2, 4, 8, 16)
POOL_GROUPS = len(POOL_WINDOWS)
POOL_GROUP_DIM = POOL_WIDTH // POOL_GROUPS
CONV_WIDTH = D_MODEL
CONV_K = 3
SPLIT_POINTS = (
    GMLP_WIDTH,
    2 * GMLP_WIDTH,
    2 * GMLP_WIDTH + POOL_WIDTH,
    2 * GMLP_WIDTH + POOL_WIDTH + CONV_WIDTH,
    2 * GMLP_WIDTH + POOL_WIDTH + 2 * CONV_WIDTH,
    2 * GMLP_WIDTH + POOL_WIDTH + 3 * CONV_WIDTH,
)
IN_COLS = 2 * GMLP_WIDTH + POOL_WIDTH + 3 * CONV_WIDTH + N_BRANCHES * D_MODEL
N_EXPERTS = 32
TOP_K = 4
D_FF = D_MODEL
SWIGLU_LIMIT = 7.0
SWIGLU_ALPHA = 1.702
MOE_BLOCK = 128

kernel_name = "hybrid_gmlp_pool_conv_moe_adaln"


def rms_norm(x, g):
    xf = x.astype(jnp.float32)
    y = xf * lax.rsqrt(jnp.mean(xf * xf, axis=-1, keepdims=True) + RMS_EPS)
    return (y * g.astype(jnp.float32)).astype(x.dtype)


def gmlp_branch(u, v, norm_g, ws, bs, w_proj):
    bsz, seq, _ = v.shape
    u = jax.nn.gelu(u)
    v = rms_norm(jax.nn.gelu(v), norm_g)
    vb = v.reshape(bsz, seq // CHUNK, CHUNK, GMLP_HEADS, GMLP_HEAD_DIM)
    causal = jnp.tril(jnp.ones((CHUNK, CHUNK), dtype=bool))
    ws_m = jnp.where(causal[None], ws, jnp.zeros_like(ws))
    s = jnp.einsum('hts,bcshd->bcthd', ws_m, vb) + jnp.transpose(bs)[:, :, None]
    return (u * s.reshape(bsz, seq, GMLP_WIDTH)) @ w_proj


def pool_branch(p, pool_w, pool_scale):
    bsz, seq, _ = p.shape
    pf = p.astype(jnp.float32)
    cs = jnp.cumsum(pf, axis=1)
    pos = jnp.arange(seq)
    diffs = []
    for gi, w in enumerate(POOL_WINDOWS):
        sl = slice(gi * POOL_GROUP_DIM, (gi + 1) * POOL_GROUP_DIM)
        cs_g = cs[:, :, sl]
        lag = jnp.pad(cs_g, ((0, 0), (w, 0), (0, 0)))[:, :seq]
        cnt = jnp.minimum(pos + 1, w).astype(jnp.float32)[None, :, None]
        diffs.append((cs_g - lag) / cnt - pf[:, :, sl])
    d = jnp.stack(diffs, axis=2).astype(p.dtype)
    y = jnp.einsum('bsgc,gce->bsge', d, pool_w).reshape(bsz, seq, POOL_WIDTH)
    return y * pool_scale


def conv_branch(xc, b_gate, c_gate, conv_w, w_proj):
    z = c_gate * xc
    zp = jnp.pad(z, ((0, 0), (CONV_K - 1, 0), (0, 0)))
    seq = z.shape[1]
    conv = sum(conv_w[k] * zp[:, k:k + seq] for k in range(CONV_K))
    return (b_gate * conv) @ w_proj


def hybrid_mixer(h, w_in, gmlp_norm_g, gmlp_ws, gmlp_bs, w_proj_a, pool_w, pool_scale,
                 conv_w, w_proj_c, w_out):
    z = jnp.einsum('bsd,dn->bsn', h, w_in)
    u, v, p, xc, b_gate, c_gate, gates = jnp.split(z, SPLIT_POINTS, axis=-1)
    ya = gmlp_branch(u, v, gmlp_norm_g, gmlp_ws, gmlp_bs, w_proj_a)
    yb = pool_branch(p, pool_w, pool_scale)
    yc = conv_branch(xc, b_gate, c_gate, conv_w, w_proj_c)
    ga, gb, gc = jnp.split(jax.nn.sigmoid(gates), N_BRANCHES, axis=-1)
    return (ga * ya + gb * yb + gc * yc) @ w_out


def moe_ffn(h, router_w, router_b, w_gu, b_gu, w_down, b_down):
    bsz, seq, d = h.shape
    n_tok = bsz * seq
    n_assign = n_tok * TOP_K
    n_rows = n_assign + N_EXPERTS * MOE_BLOCK
    n_blocks = n_rows // MOE_BLOCK
    t = h.reshape(n_tok, d)
    logits = (t @ router_w + router_b).astype(jnp.float32)
    top_vals, top_idx = lax.top_k(logits, TOP_K)
    gate_w = jax.nn.softmax(top_vals, axis=-1)
    flat_e = top_idx.reshape(-1).astype(jnp.int32)
    flat_tok = jnp.repeat(jnp.arange(n_tok, dtype=jnp.int32), TOP_K)
    order = jnp.argsort(flat_e)
    sorted_e = flat_e[order]
    sorted_tok = flat_tok[order]
    w_sorted = gate_w.reshape(-1)[order].astype(h.dtype)
    counts = jnp.bincount(flat_e, length=N_EXPERTS).astype(jnp.int32)
    padded = ((counts + MOE_BLOCK - 1) // MOE_BLOCK) * MOE_BLOCK
    pad_end = jnp.cumsum(padded)
    pad_start = pad_end - padded
    grp_start = jnp.cumsum(counts) - counts
    rank = jnp.arange(n_assign, dtype=jnp.int32) - grp_start[sorted_e]
    dest = pad_start[sorted_e] + rank
    row_tok = jnp.zeros((n_rows,), jnp.int32).at[dest].set(sorted_tok)
    xs = t[row_tok].reshape(n_blocks, MOE_BLOCK, d)
    block_start = jnp.arange(n_blocks, dtype=jnp.int32) * MOE_BLOCK
    block_e = jnp.minimum(jnp.searchsorted(pad_end, block_start, side='right'), N_EXPERTS - 1)

    def expert_block(args):
        xb, e = args
        gu = xb @ w_gu[e] + b_gu[e]
        gate, up = jnp.split(gu, 2, axis=-1)
        gate = jnp.minimum(gate, SWIGLU_LIMIT)
        up = jnp.clip(up, -SWIGLU_LIMIT, SWIGLU_LIMIT)
        glu = gate * jax.nn.sigmoid(SWIGLU_ALPHA * gate)
        return ((up + 1.0) * glu) @ w_down[e] + b_down[e]

    ys = lax.map(expert_block, (xs, block_e)).reshape(n_rows, d)
    y_assign = ys[dest] * w_sorted[:, None]
    out = jax.ops.segment_sum(y_assign, sorted_tok, num_segments=n_tok)
    return out.reshape(bsz, seq, d)


def setup_inputs(seed: int = 0) -> dict:
    key = jax.random.key(seed)
    ks = jax.random.split(key, 24)
    f32 = jnp.float32
    L, D = DEPTH, D_MODEL

    def nrm(k, shape, s):
        return jax.random.normal(k, shape, f32) * s

    return {
        "x": nrm(ks[0], (BATCH, SEQ, D), 1.0),
        "c": nrm(ks[1], (BATCH, D), 1.0),
        "norm1_g": 1.0 + nrm(ks[2], (L, D), 0.05),
        "ada_w": nrm(ks[3], (L, D, 6 * D), 0.2 * D ** -0.5),
        "ada_b": nrm(ks[4], (L, 6 * D), 0.02),
        "w_in": nrm(ks[5], (L, D, IN_COLS), D ** -0.5),
        "gmlp_norm_g": 1.0 + nrm(ks[6], (L, GMLP_WIDTH), 0.05),
        "gmlp_ws": nrm(ks[7], (L, GMLP_HEADS, CHUNK, CHUNK), CHUNK ** -0.5),
        "gmlp_bs": 1.0 + nrm(ks[8], (L, GMLP_HEADS, CHUNK), 0.05),
        "w_proj_a": nrm(ks[9], (L, GMLP_WIDTH, D), GMLP_WIDTH ** -0.5),
        "pool_w": nrm(ks[10], (L, POOL_GROUPS, POOL_GROUP_DIM, POOL_GROUP_DIM), POOL_GROUP_DIM ** -0.5),
        "pool_scale": 1.0 + nrm(ks[11], (L, POOL_WIDTH), 0.05),
        "conv_w": nrm(ks[12], (L, CONV_K, CONV_WIDTH), CONV_K ** -0.5),
        "w_proj_c": nrm(ks[13], (L, CONV_WIDTH, D), CONV_WIDTH ** -0.5),
        "w_out": nrm(ks[14], (L, D, D), D ** -0.5),
        "norm2_g": 1.0 + nrm(ks[15], (L, D), 0.05),
        "router_w": nrm(ks[16], (L, D, N_EXPERTS), D ** -0.5),
        "router_b": nrm(ks[17], (L, N_EXPERTS), 0.01),
        "exp_w_gu": nrm(ks[18], (L, N_EXPERTS, D, 2 * D_FF), D ** -0.5),
        "exp_b_gu": nrm(ks[19], (L, N_EXPERTS, 2 * D_FF), 0.02),
        "exp_w_down": nrm(ks[20], (L, N_EXPERTS, D_FF, D), D_FF ** -0.5),
        "exp_b_down": nrm(ks[21], (L, N_EXPERTS, D), 0.02),
        "final_g": 1.0 + nrm(ks[22], (D,), 0.05),
    }


def reference(x, c, norm1_g, ada_w, ada_b, w_in, gmlp_norm_g, gmlp_ws, gmlp_bs, w_proj_a,
              pool_w, pool_scale, conv_w, w_proj_c, w_out, norm2_g, router_w, router_b,
              exp_w_gu, exp_b_gu, exp_w_down, exp_b_down, final_g):
    cond = jax.nn.silu(c)
    for l in range(DEPTH):
        mods = cond @ ada_w[l] + ada_b[l]
        sh1, sc1, g1, sh2, sc2, g2 = [m[:, None, :] for m in jnp.split(mods, 6, axis=-1)]
        h = rms_norm(x, norm1_g[l]) * (1.0 + sc1) + sh1
        x = x + g1 * hybrid_mixer(h, w_in[l], gmlp_norm_g[l], gmlp_ws[l], gmlp_bs[l], w_proj_a[l],
                                  pool_w[l], pool_scale[l], conv_w[l], w_proj_c[l], w_out[l])
        h = rms_norm(x, norm2_g[l]) * (1.0 + sc2) + sh2
        x = x + g2 * moe_ffn(h, router_w[l], router_b[l], exp_w_gu[l], exp_b_gu[l],
                             exp_w_down[l], exp_b_down[l])
    return rms_norm(x, final_g)
```

```python
import functools

import jax
import jax.numpy as jnp
from jax import lax
from jax.experimental import pallas as pl
from jax.experimental.pallas import tpu as pltpu

F32 = jnp.float32
BF16 = jnp.bfloat16
I32 = jnp.int32
U32 = jnp.uint32

RMS_EPS = 1e-5
GMLP_HEADS = 8
CHUNK = 128
POOL_WINDOWS = (2, 4, 8, 16)
POOL_CARRY = 16
CONV_K = 3
CONV_CARRY = 8
N_EXPERTS = 32
TOP_K = 4
SWIGLU_LIMIT = 7.0
SWIGLU_ALPHA = 1.702

MIX_ROWS = 256
ROUTE_ROWS = 512
DEST_COLS = 4096
DISPATCH_ROWS = 256
EXPERT_ROWS = 512
COMBINE_ROWS = 256

VMEM_LIMIT = 56 * 1024 * 1024


def _rms(x, g):
    return x * lax.rsqrt(jnp.mean(x * x, axis=-1, keepdims=True) + RMS_EPS) * g


def _dot(a, b):
    return jnp.dot(a, b, preferred_element_type=F32)


def _params(n_axes, vmem=VMEM_LIMIT):
    return pltpu.CompilerParams(dimension_semantics=("arbitrary",) * n_axes, vmem_limit_bytes=vmem)


def _resident(shape):
    zeros = (0,) * len(shape)
    return pl.BlockSpec(shape, lambda *_: zeros, pipeline_mode=pl.Buffered(1))


def _ada_kernel(c_ref, w_ref, b_ref, o_ref):
    c = c_ref[...]
    cond = c * jax.nn.sigmoid(c)
    o_ref[0] = _dot(cond.astype(BF16), w_ref[0].astype(BF16)) + b_ref[0]


def _ada(c, ada_w, ada_b):
    depth, d, six_d = ada_w.shape
    bsz = c.shape[0]
    return pl.pallas_call(
        _ada_kernel,
        grid=(depth, six_d // d),
        in_specs=[
            pl.BlockSpec((bsz, d), lambda l, j: (0, 0)),
            pl.BlockSpec((1, d, d), lambda l, j: (l, 0, j)),
            pl.BlockSpec((1, 1, d), lambda l, j: (l, 0, j)),
        ],
        out_specs=pl.BlockSpec((1, bsz, d), lambda l, j: (l, 0, j)),
        out_shape=jax.ShapeDtypeStruct((depth, bsz, six_d), F32),
        compiler_params=_params(2),
        name="ada",
    )(c, ada_w, ada_b.reshape(depth, 1, six_d))


def _mixer_kernel(x_ref, sh_ref, sc_ref, gt_ref, n1_ref, win_ref, gng_ref, ws_ref, bst_ref, wpa_ref,
                  pw_ref, ps_ref, cw_ref, wpc_ref, wout_ref, o_ref, s_ref, pext_ref, zext_ref):
    tm, d = x_ref.shape[1], x_ref.shape[2]
    j = pl.program_id(1)

    @pl.when(j == 0)
    def _():
        pext_ref[0:POOL_CARRY, :] = jnp.zeros((POOL_CARRY, d), F32)
        zext_ref[0:CONV_CARRY, :] = jnp.zeros((CONV_CARRY, d), F32)

    x = x_ref[0]
    hb = (_rms(x, n1_ref[...]) * (1.0 + sc_ref[0]) + sh_ref[0]).astype(BF16)

    def proj(col):
        return _dot(hb, win_ref[:, col * d:(col + 1) * d])

    u = jax.nn.gelu(proj(0))
    vb = _rms(jax.nn.gelu(proj(1)), gng_ref[...]).astype(BF16)
    hd = d // GMLP_HEADS
    row = lax.broadcasted_iota(I32, (CHUNK, CHUNK), 0)
    col = lax.broadcasted_iota(I32, (CHUNK, CHUNK), 1)
    for h in range(GMLP_HEADS):
        wm = jnp.where(row >= col, ws_ref[h], 0.0).astype(BF16)
        bias = bst_ref[:, h:h + 1]
        for ci in range(tm // CHUNK):
            blk = vb[ci * CHUNK:(ci + 1) * CHUNK, h * hd:(h + 1) * hd]
            s_ref[ci * CHUNK:(ci + 1) * CHUNK, h * hd:(h + 1) * hd] = _dot(wm, blk) + bias
    ya = _dot((u * s_ref[...]).astype(BF16), wpa_ref[...])
    mix = jax.nn.sigmoid(proj(6)) * ya

    p = proj(2)
    pext_ref[POOL_CARRY:POOL_CARRY + tm, :] = p
    pos1 = (j * tm + 1 + lax.broadcasted_iota(I32, (tm, 1), 0))
    gd = d // len(POOL_WINDOWS)
    yb_parts = []
    for gi, w in enumerate(POOL_WINDOWS):
        cs = slice(gi * gd, (gi + 1) * gd)
        acc = p[:, cs]
        for k in range(1, w):
            acc = acc + pext_ref[POOL_CARRY - k:POOL_CARRY - k + tm, cs]
        cnt = jnp.minimum(pos1, w).astype(F32)
        dg = (acc / cnt - p[:, cs]).astype(BF16)
        yb_parts.append(_dot(dg, pw_ref[gi]))
    yb = jnp.concatenate(yb_parts, axis=1) * ps_ref[...]
    pext_ref[0:POOL_CARRY, :] = pext_ref[tm:tm + POOL_CARRY, :]
    mix = mix + jax.nn.sigmoid(proj(7)) * yb

    z = proj(5) * proj(3)
    zext_ref[CONV_CARRY:CONV_CARRY + tm, :] = z
    conv = cw_ref[CONV_K - 1:CONV_K, :] * z
    for k in range(CONV_K - 1):
        lag = CONV_K - 1 - k
        conv = conv + cw_ref[k:k + 1, :] * zext_ref[CONV_CARRY - lag:CONV_CARRY - lag + tm, :]
    zext_ref[0:CONV_CARRY, :] = zext_ref[tm:tm + CONV_CARRY, :]
    yc = _dot((proj(4) * conv).astype(BF16), wpc_ref[...])
    mix = mix + jax.nn.sigmoid(proj(8)) * yc

    o_ref[0] = x + gt_ref[0] * _dot(mix.astype(BF16), wout_ref[...])


def _mixer(x, sh, sc, gt, n1, w_in, gng, ws, bs, wpa, pool_w, pool_scale, conv_w, wpc, w_out):
    bsz, seq, d = x.shape
    tm = min(MIX_ROWS, seq)
    vec = pl.BlockSpec((1, 1, d), lambda b, j: (b, 0, 0))
    tile = pl.BlockSpec((1, tm, d), lambda b, j: (b, j, 0))
    return pl.pallas_call(
        _mixer_kernel,
        grid=(bsz, seq // tm),
        in_specs=[
            tile, vec, vec, vec,
            _resident((1, d)),
            _resident(w_in.shape),
            _resident((1, d)),
            _resident(ws.shape),
            _resident((CHUNK, GMLP_HEADS)),
            _resident(wpa.shape),
            _resident(pool_w.shape),
            _resident((1, d)),
            _resident(conv_w.shape),
            _resident(wpc.shape),
            _resident(w_out.shape),
        ],
        out_specs=tile,
        out_shape=jax.ShapeDtypeStruct(x.shape, F32),
        scratch_shapes=[
            pltpu.VMEM((tm, d), F32),
            pltpu.VMEM((POOL_CARRY + tm, d), F32),
            pltpu.VMEM((CONV_CARRY + tm, d), F32),
        ],
        compiler_params=_params(2),
        name="mixer",
    )(x, sh, sc, gt, n1.reshape(1, d), w_in.astype(BF16), gng.reshape(1, d), ws, bs.T,
      wpa.astype(BF16), pool_w.astype(BF16), pool_scale.reshape(1, d), conv_w, wpc.astype(BF16),
      w_out.astype(BF16))


def _router_kernel(x_ref, sh_ref, sc_ref, n2_ref, rwt_ref, rb_ref,
                   hpk_ref, idx_ref, gw_ref, rank_ref, cnt_ref, carry_ref):
    tm, d = x_ref.shape[1], x_ref.shape[2]
    first = jnp.logical_and(pl.program_id(0) == 0, pl.program_id(1) == 0)

    @pl.when(first)
    def _():
        carry_ref[...] = jnp.zeros(carry_ref.shape, F32)

    hb = (_rms(x_ref[0], n2_ref[...]) * (1.0 + sc_ref[0]) + sh_ref[0]).astype(BF16)
    bits = lax.bitcast_convert_type(hb.astype(F32), U32)
    hpk_ref[...] = (bits[:, :d // 2] >> 16) | bits[:, d // 2:]

    logits = lax.dot_general(rwt_ref[...], hb, (((1,), (1,)), ((), ())),
                             preferred_element_type=F32) + rb_ref[...]
    iota_e = lax.broadcasted_iota(I32, logits.shape, 0)
    vals, idxs, sels = [], [], []
    rest = logits
    for _ in range(TOP_K):
        m = jnp.max(rest, axis=0, keepdims=True)
        ik = jnp.min(jnp.where(rest == m, iota_e, N_EXPERTS), axis=0, keepdims=True)
        sel = iota_e == ik
        rest = jnp.where(sel, -jnp.inf, rest)
        vals.append(m)
        idxs.append(ik)
        sels.append(sel)
    exps = [jnp.exp(v - vals[0]) for v in vals]
    denom = exps[0] + exps[1] + exps[2] + exps[3]

    chosen = jnp.logical_or(jnp.logical_or(sels[0], sels[1]), jnp.logical_or(sels[2], sels[3]))
    a = jnp.where(chosen, 1.0, 0.0)
    before = lax.broadcasted_iota(I32, (tm, tm), 0) < lax.broadcasted_iota(I32, (tm, tm), 1)
    prior = _dot(a.astype(BF16), jnp.where(before, 1.0, 0.0).astype(BF16)) + carry_ref[:, 0:1]
    for k in range(TOP_K):
        idx_ref[k:k + 1, :] = idxs[k]
        gw_ref[k:k + 1, :] = exps[k] / denom
        rank_ref[k:k + 1, :] = jnp.sum(jnp.where(sels[k], prior, 0.0), axis=0, keepdims=True).astype(I32)
    total = carry_ref[...] + jnp.sum(a, axis=1, keepdims=True)
    carry_ref[...] = total
    cnt_ref[...] = total.astype(I32)


def _router(x, sh, sc, n2, router_w, router_b):
    bsz, seq, d = x.shape
    n_tok = bsz * seq
    tm = min(ROUTE_ROWS, seq)
    nj = seq // tm
    vec = pl.BlockSpec((1, 1, d), lambda b, j: (b, 0, 0))
    per_tok = pl.BlockSpec((TOP_K, tm), lambda b, j: (0, b * nj + j))
    return pl.pallas_call(
        _router_kernel,
        grid=(bsz, nj),
        in_specs=[
            pl.BlockSpec((1, tm, d), lambda b, j: (b, j, 0)), vec, vec,
            _resident((1, d)), _resident((N_EXPERTS, d)), _resident((N_EXPERTS, 1)),
        ],
        out_specs=[
            pl.BlockSpec((tm, d // 2), lambda b, j: (b * nj + j, 0)),
            per_tok, per_tok, per_tok,
            pl.BlockSpec((N_EXPERTS, 128), lambda b, j: (0, 0)),
        ],
        out_shape=[
            jax.ShapeDtypeStruct((n_tok, d // 2), U32),
            jax.ShapeDtypeStruct((TOP_K, n_tok), I32),
            jax.ShapeDtypeStruct((TOP_K, n_tok), F32),
            jax.ShapeDtypeStruct((TOP_K, n_tok), I32),
            jax.ShapeDtypeStruct((N_EXPERTS, 128), I32),
        ],
        scratch_shapes=[pltpu.VMEM((N_EXPERTS, 128), F32)],
        compiler_params=_params(2),
        name="router",
    )(x, sh, sc, n2.reshape(1, d), router_w.T.astype(BF16), router_b.reshape(N_EXPERTS, 1))


def _dest_kernel(start_ref, idx_ref, rank_ref, o_ref):
    idx = idx_ref[...]
    base = jnp.zeros(idx.shape, I32)
    for e in range(N_EXPERTS):
        base = jnp.where(idx == e, start_ref[e], base)
    o_ref[...] = base + rank_ref[...]


def _dest(pad_start, idx, rank):
    n_tok = idx.shape[1]
    tc = min(DEST_COLS, n_tok)
    blk = pl.BlockSpec((TOP_K, tc), lambda i, s: (0, i))
    return pl.pallas_call(
        _dest_kernel,
        grid_spec=pltpu.PrefetchScalarGridSpec(
            num_scalar_prefetch=1, grid=(n_tok // tc,), in_specs=[blk, blk], out_specs=blk),
        out_shape=jax.ShapeDtypeStruct(idx.shape, I32),
        compiler_params=_params(1),
        name="dest",
    )(pad_start, idx, rank)


def _dispatch_kernel(fill_lo_ref, fill_hi_ref, nv_ref, hpk_ref, dest_ref, xs_ref, zero_ref, sem, zsem):
    tm = hpk_ref.shape[0]
    tb = zero_ref.shape[0]
    n_blocks = xs_ref.shape[0] // tb

    def row_copy(t, dst_row):
        return pltpu.make_async_copy(hpk_ref.at[pl.ds(t, 1), :], xs_ref.at[pl.ds(dst_row, 1), :], sem)

    def zero_copy(dst_row):
        return pltpu.make_async_copy(zero_ref.at[pl.ds(0, 1), :], xs_ref.at[pl.ds(dst_row, 1), :], zsem)

    def zero_block_copy(b):
        return pltpu.make_async_copy(zero_ref, xs_ref.at[pl.ds(pl.multiple_of(b * tb, tb), tb), :], zsem)

    @pl.when(pl.program_id(0) == 0)
    def _():
        zero_ref[...] = jnp.zeros(zero_ref.shape, U32)
        for e in range(N_EXPERTS):
            lo, hi = fill_lo_ref[e], fill_hi_ref[e]
            lax.fori_loop(lo, hi, lambda r, c: (zero_copy(r).start(), c)[1], 0)
        lax.fori_loop(nv_ref[0], n_blocks, lambda b, c: (zero_block_copy(b).start(), c)[1], 0)
        for e in range(N_EXPERTS):
            lo, hi = fill_lo_ref[e], fill_hi_ref[e]
            lax.fori_loop(lo, hi, lambda r, c: (zero_copy(r).wait(), c)[1], 0)
        lax.fori_loop(nv_ref[0], n_blocks, lambda b, c: (zero_block_copy(b).wait(), c)[1], 0)

    def issue(t, c):
        for k in range(TOP_K):
            row_copy(t, dest_ref[0, k, t]).start()
        return c

    def drain(t, c):
        for k in range(TOP_K):
            row_copy(t, dest_ref[0, k, t]).wait()
        return c

    lax.fori_loop(0, tm, issue, 0)
    lax.fori_loop(0, tm, drain, 0)


def _dispatch(fill_lo, fill_hi, n_valid, hpk, dest, n_rows):
    n_tok, half = hpk.shape
    tm = min(DISPATCH_ROWS, n_tok)
    dest3 = dest.reshape(TOP_K, n_tok // tm, tm).transpose(1, 0, 2)
    return pl.pallas_call(
        _dispatch_kernel,
        grid_spec=pltpu.PrefetchScalarGridSpec(
            num_scalar_prefetch=3,
            grid=(n_tok // tm,),
            in_specs=[
                pl.BlockSpec((tm, half), lambda i, *_: (i, 0)),
                pl.BlockSpec((1, TOP_K, tm), lambda i, *_: (i, 0, 0), memory_space=pltpu.SMEM),
            ],
            out_specs=pl.BlockSpec(memory_space=pl.ANY),
            scratch_shapes=[pltpu.VMEM((EXPERT_ROWS, half), U32), pltpu.SemaphoreType.DMA,
                            pltpu.SemaphoreType.DMA],
        ),
        out_shape=jax.ShapeDtypeStruct((n_rows, half), U32),
        compiler_params=_params(1),
        name="dispatch",
    )(fill_lo, fill_hi, n_valid, hpk, dest3)


def _expert_kernel(be_ref, nv_ref, xs_ref, wgu_ref, bgu_ref, wd_ref, bd_ref, ys_ref):
    @pl.when(pl.program_id(0) >= nv_ref[0])
    def _():
        ys_ref[...] = jnp.zeros(ys_ref.shape, F32)

    @pl.when(pl.program_id(0) < nv_ref[0])
    def _():
        w = xs_ref[...]
        lo = lax.bitcast_convert_type(w << 16, F32)
        hi = lax.bitcast_convert_type(w & jnp.uint32(0xFFFF0000), F32)
        xb = jnp.concatenate([lo, hi], axis=1).astype(BF16)
        gu = _dot(xb, wgu_ref[0]) + bgu_ref[0]
        ff = gu.shape[1] // 2
        gate = jnp.minimum(gu[:, :ff], SWIGLU_LIMIT)
        up = jnp.clip(gu[:, ff:], -SWIGLU_LIMIT, SWIGLU_LIMIT)
        glu = gate * jax.nn.sigmoid(SWIGLU_ALPHA * gate)
        ys_ref[...] = _dot(((up + 1.0) * glu).astype(BF16), wd_ref[0]) + bd_ref[0]


def _experts(block_e, n_valid, xs, w_gu, b_gu, w_down, b_down):
    n_rows, half = xs.shape
    _, d, two_f = w_gu.shape
    tb = EXPERT_ROWS

    def rows(i, be, nv):
        return (jnp.minimum(i, nv[0] - 1), 0)

    def by_expert(i, be, nv):
        return (be[i], 0, 0)

    return pl.pallas_call(
        _expert_kernel,
        grid_spec=pltpu.PrefetchScalarGridSpec(
            num_scalar_prefetch=2,
            grid=(n_rows // tb,),
            in_specs=[
                pl.BlockSpec((tb, half), rows),
                pl.BlockSpec((1, d, two_f), by_expert),
                pl.BlockSpec((1, 1, two_f), by_expert),
                pl.BlockSpec((1, two_f // 2, d), by_expert),
                pl.BlockSpec((1, 1, d), by_expert),
            ],
            out_specs=pl.BlockSpec((tb, d), lambda i, be, nv: (i, 0)),
        ),
        out_shape=jax.ShapeDtypeStruct((n_rows, d), F32),
        compiler_params=_params(1),
        name="experts",
    )(block_e, n_valid, xs, w_gu.astype(BF16), b_gu.reshape(N_EXPERTS, 1, two_f),
      w_down.astype(BF16), b_down.reshape(N_EXPERTS, 1, d))


def _combine_kernel(dest_ref, ys_ref, x_ref, gw_ref, gt_ref, fg_ref, o_ref, buf_ref, sem, *, final_norm):
    tm = x_ref.shape[0]

    def row_copy(t, k):
        return pltpu.make_async_copy(ys_ref.at[pl.ds(dest_ref[0, k, t], 1), :],
                                     buf_ref.at[k, pl.ds(t, 1), :], sem)

    def issue(t, c):
        for k in range(TOP_K):
            row_copy(t, k).start()
        return c

    def drain(t, c):
        for k in range(TOP_K):
            row_copy(t, k).wait()
        return c

    lax.fori_loop(0, tm, issue, 0)
    lax.fori_loop(0, tm, drain, 0)

    gw = gw_ref[...]
    acc = gw[:, 0:1] * buf_ref[0]
    for k in range(1, TOP_K):
        acc = acc + gw[:, k:k + 1] * buf_ref[k]
    y = x_ref[...] + gt_ref[0] * acc
    o_ref[...] = _rms(y, fg_ref[...]) if final_norm else y


def _combine(dest, ys, x, gw, gt, final_g, final_norm):
    bsz, seq, d = x.shape
    n_tok = bsz * seq
    tm = min(COMBINE_ROWS, seq)
    per_seq = seq // tm
    dest3 = dest.reshape(TOP_K, n_tok // tm, tm).transpose(1, 0, 2)
    out = pl.pallas_call(
        functools.partial(_combine_kernel, final_norm=final_norm),
        grid=(n_tok // tm,),
        in_specs=[
            pl.BlockSpec((1, TOP_K, tm), lambda i: (i, 0, 0), memory_space=pltpu.SMEM),
            pl.BlockSpec(memory_space=pl.ANY),
            pl.BlockSpec((tm, d), lambda i: (i, 0)),
            pl.BlockSpec((tm, TOP_K), lambda i: (i, 0)),
            pl.BlockSpec((1, 1, d), lambda i: (i // per_seq, 0, 0)),
            pl.BlockSpec((1, d), lambda i: (0, 0)),
        ],
        out_specs=pl.BlockSpec((tm, d), lambda i: (i, 0)),
        out_shape=jax.ShapeDtypeStruct((n_tok, d), F32),
        scratch_shapes=[pltpu.VMEM((TOP_K, tm, d), F32), pltpu.SemaphoreType.DMA],
        compiler_params=_params(1),
        name="combine",
    )(dest3, ys, x.reshape(n_tok, d), gw.T, gt, final_g.reshape(1, d))
    return out.reshape(bsz, seq, d)


def _moe(x, sh, sc, gt, n2, router_w, router_b, w_gu, b_gu, w_down, b_down, final_g, final_norm):
    bsz, seq, _ = x.shape
    n_tok = bsz * seq
    tb = EXPERT_ROWS
    n_rows = -(-(n_tok * TOP_K + N_EXPERTS * (tb - 1)) // tb) * tb
    hpk, idx, gw, rank, cnt = _router(x, sh, sc, n2, router_w, router_b)
    counts = cnt[:, 0]
    padded = (counts + tb - 1) // tb * tb
    pad_end = jnp.cumsum(padded)
    pad_start = pad_end - padded
    n_valid = pad_end[-1] // tb
    blocks = jnp.arange(n_rows // tb, dtype=I32)
    block_e = jnp.minimum(jnp.searchsorted(pad_end, blocks * tb, side="right"), N_EXPERTS - 1).astype(I32)
    block_e = jnp.where(blocks < n_valid, block_e, block_e[n_valid - 1])
    dest = _dest(pad_start, idx, rank)
    n_valid = n_valid.reshape(1)
    xs = _dispatch(pad_start + counts, pad_end, n_valid, hpk, dest, n_rows)
    ys = _experts(block_e, n_valid, xs, w_gu, b_gu, w_down, b_down)
    return _combine(dest, ys, x, gw, gt, final_g, final_norm)


def kernel(x, c, norm1_g, ada_w, ada_b, w_in, gmlp_norm_g, gmlp_ws, gmlp_bs, w_proj_a, pool_w, pool_scale,
           conv_w, w_proj_c, w_out, norm2_g, router_w, router_b, exp_w_gu, exp_b_gu, exp_w_down,
           exp_b_down, final_g):
    depth = ada_w.shape[0]
    bsz, _, d = x.shape
    mods = _ada(c, ada_w, ada_b)
    for l in range(depth):
        sh1, sc1, g1, sh2, sc2, g2 = [mods[l, :, i * d:(i + 1) * d].reshape(bsz, 1, d) for i in range(6)]
        x = _mixer(x, sh1, sc1, g1, norm1_g[l], w_in[l], gmlp_norm_g[l], gmlp_ws[l], gmlp_bs[l],
                   w_proj_a[l], pool_w[l], pool_scale[l], conv_w[l], w_proj_c[l], w_out[l])
        x = _moe(x, sh2, sc2, g2, norm2_g[l], router_w[l], router_b[l], exp_w_gu[l], exp_b_gu[l],
                 exp_w_down[l], exp_b_down[l], final_g, l == depth - 1)
    return x
```

```python
import functools

import jax
import jax.numpy as jnp
from jax import lax
from jax.experimental import pallas as pl
from jax.experimental.pallas import tpu as pltpu

F32 = jnp.float32
BF16 = jnp.bfloat16
I32 = jnp.int32
U32 = jnp.uint32

RMS_EPS = 1e-5
GMLP_HEADS = 8
CHUNK = 128
POOL_WINDOWS = (2, 4, 8, 16)
POOL_CARRY = 16
CONV_K = 3
CONV_CARRY = 8
N_EXPERTS = 32
TOP_K = 4
SWIGLU_LIMIT = 7.0
SWIGLU_ALPHA = 1.702

MIX_ROWS = 512
ROUTE_ROWS = 512
DEST_COLS = 4096
DISPATCH_ROWS = 256
EXPERT_ROWS = 512
COMBINE_ROWS = 256
ISSUE_UNROLL = 4
DRAIN_UNROLL = 16

VMEM_LIMIT = 56 * 1024 * 1024


def _rms(x, g):
    return x * lax.rsqrt(jnp.mean(x * x, axis=-1, keepdims=True) + RMS_EPS) * g


def _dot(a, b):
    return jnp.dot(a, b, preferred_element_type=F32)


def _params(n_axes, vmem=VMEM_LIMIT):
    return pltpu.CompilerParams(dimension_semantics=("arbitrary",) * n_axes, vmem_limit_bytes=vmem)


def _resident(shape):
    zeros = (0,) * len(shape)
    return pl.BlockSpec(shape, lambda *_: zeros, pipeline_mode=pl.Buffered(1))


def _ada_kernel(c_ref, w_ref, b_ref, o_ref):
    c = c_ref[...]
    cond = c * jax.nn.sigmoid(c)
    o_ref[0] = _dot(cond.astype(BF16), w_ref[0].astype(BF16)) + b_ref[0]


def _ada(c, ada_w, ada_b):
    depth, d, six_d = ada_w.shape
    bsz = c.shape[0]
    return pl.pallas_call(
        _ada_kernel,
        grid=(depth, six_d // d),
        in_specs=[
            pl.BlockSpec((bsz, d), lambda l, j: (0, 0)),
            pl.BlockSpec((1, d, d), lambda l, j: (l, 0, j)),
            pl.BlockSpec((1, 1, d), lambda l, j: (l, 0, j)),
        ],
        out_specs=pl.BlockSpec((1, bsz, d), lambda l, j: (l, 0, j)),
        out_shape=jax.ShapeDtypeStruct((depth, bsz, six_d), F32),
        compiler_params=_params(2),
        name="ada",
    )(c, ada_w, ada_b.reshape(depth, 1, six_d))


def _mixer_kernel(x_ref, sh_ref, sc_ref, gt_ref, n1_ref, win_ref, gng_ref, ws_ref, bst_ref, wpa_ref,
                  pw_ref, ps_ref, cw_ref, wpc_ref, wout_ref, o_ref, s_ref, pext_ref, zext_ref):
    tm, d = x_ref.shape[1], x_ref.shape[2]
    j = pl.program_id(1)

    @pl.when(j == 0)
    def _():
        pext_ref[0:POOL_CARRY, :] = jnp.zeros((POOL_CARRY, d), F32)
        zext_ref[0:CONV_CARRY, :] = jnp.zeros((CONV_CARRY, d), F32)

    x = x_ref[0]
    hb = (_rms(x, n1_ref[...]) * (1.0 + sc_ref[0]) + sh_ref[0]).astype(BF16)

    def proj(col):
        return _dot(hb, win_ref[:, col * d:(col + 1) * d])

    u = jax.nn.gelu(proj(0))
    vb = _rms(jax.nn.gelu(proj(1)), gng_ref[...]).astype(BF16)
    hd = d // GMLP_HEADS
    row = lax.broadcasted_iota(I32, (CHUNK, CHUNK), 0)
    col = lax.broadcasted_iota(I32, (CHUNK, CHUNK), 1)
    for h in range(GMLP_HEADS):
        wm = jnp.where(row >= col, ws_ref[h], 0.0).astype(BF16)
        bias = bst_ref[:, h:h + 1]
        for ci in range(tm // CHUNK):
            blk = vb[ci * CHUNK:(ci + 1) * CHUNK, h * hd:(h + 1) * hd]
            s_ref[ci * CHUNK:(ci + 1) * CHUNK, h * hd:(h + 1) * hd] = _dot(wm, blk) + bias
    ya = _dot((u * s_ref[...]).astype(BF16), wpa_ref[...])
    mix = jax.nn.sigmoid(proj(6)) * ya

    p = proj(2)
    pext_ref[POOL_CARRY:POOL_CARRY + tm, :] = p
    pos1 = (j * tm + 1 + lax.broadcasted_iota(I32, (tm, 1), 0))
    gd = d // len(POOL_WINDOWS)
    yb_parts = []
    for gi, w in enumerate(POOL_WINDOWS):
        cs = slice(gi * gd, (gi + 1) * gd)
        acc = p[:, cs]
        for k in range(1, w):
            acc = acc + pext_ref[POOL_CARRY - k:POOL_CARRY - k + tm, cs]
        cnt = jnp.minimum(pos1, w).astype(F32)
        dg = (acc / cnt - p[:, cs]).astype(BF16)
        yb_parts.append(_dot(dg, pw_ref[gi]))
    yb = jnp.concatenate(yb_parts, axis=1) * ps_ref[...]
    pext_ref[0:POOL_CARRY, :] = pext_ref[tm:tm + POOL_CARRY, :]
    mix = mix + jax.nn.sigmoid(proj(7)) * yb

    z = proj(5) * proj(3)
    zext_ref[CONV_CARRY:CONV_CARRY + tm, :] = z
    conv = cw_ref[CONV_K - 1:CONV_K, :] * z
    for k in range(CONV_K - 1):
        lag = CONV_K - 1 - k
        conv = conv + cw_ref[k:k + 1, :] * zext_ref[CONV_CARRY - lag:CONV_CARRY - lag + tm, :]
    zext_ref[0:CONV_CARRY, :] = zext_ref[tm:tm + CONV_CARRY, :]
    yc = _dot((proj(4) * conv).astype(BF16), wpc_ref[...])
    mix = mix + jax.nn.sigmoid(proj(8)) * yc

    o_ref[0] = x + gt_ref[0] * _dot(mix.astype(BF16), wout_ref[...])


def _mixer(x, sh, sc, gt, n1, w_in, gng, ws, bs, wpa, pool_w, pool_scale, conv_w, wpc, w_out):
    bsz, seq, d = x.shape
    tm = min(MIX_ROWS, seq)
    vec = pl.BlockSpec((1, 1, d), lambda b, j: (b, 0, 0))
    tile = pl.BlockSpec((1, tm, d), lambda b, j: (b, j, 0))
    return pl.pallas_call(
        _mixer_kernel,
        grid=(bsz, seq // tm),
        in_specs=[
            tile, vec, vec, vec,
            _resident((1, d)),
            _resident(w_in.shape),
            _resident((1, d)),
            _resident(ws.shape),
            _resident((CHUNK, GMLP_HEADS)),
            _resident(wpa.shape),
            _resident(pool_w.shape),
            _resident((1, d)),
            _resident(conv_w.shape),
            _resident(wpc.shape),
            _resident(w_out.shape),
        ],
        out_specs=tile,
        out_shape=jax.ShapeDtypeStruct(x.shape, F32),
        scratch_shapes=[
            pltpu.VMEM((tm, d), F32),
            pltpu.VMEM((POOL_CARRY + tm, d), F32),
            pltpu.VMEM((CONV_CARRY + tm, d), F32),
        ],
        compiler_params=_params(2),
        name="mixer",
    )(x, sh, sc, gt, n1.reshape(1, d), w_in.astype(BF16), gng.reshape(1, d), ws, bs.T,
      wpa.astype(BF16), pool_w.astype(BF16), pool_scale.reshape(1, d), conv_w, wpc.astype(BF16),
      w_out.astype(BF16))


def _router_kernel(x_ref, sh_ref, sc_ref, n2_ref, rwt_ref, rb_ref,
                   hpk_ref, idx_ref, gw_ref, rank_ref, cnt_ref, carry_ref):
    tm, d = x_ref.shape[1], x_ref.shape[2]
    first = jnp.logical_and(pl.program_id(0) == 0, pl.program_id(1) == 0)

    @pl.when(first)
    def _():
        carry_ref[...] = jnp.zeros(carry_ref.shape, F32)

    hb = (_rms(x_ref[0], n2_ref[...]) * (1.0 + sc_ref[0]) + sh_ref[0]).astype(BF16)
    bits = lax.bitcast_convert_type(hb.astype(F32), U32)
    hpk_ref[...] = (bits[:, :d // 2] >> 16) | bits[:, d // 2:]

    logits = lax.dot_general(rwt_ref[...], hb, (((1,), (1,)), ((), ())),
                             preferred_element_type=F32) + rb_ref[...]
    iota_e = lax.broadcasted_iota(I32, logits.shape, 0)
    vals, idxs, sels = [], [], []
    rest = logits
    for _ in range(TOP_K):
        m = jnp.max(rest, axis=0, keepdims=True)
        ik = jnp.min(jnp.where(rest == m, iota_e, N_EXPERTS), axis=0, keepdims=True)
        sel = iota_e == ik
        rest = jnp.where(sel, -jnp.inf, rest)
        vals.append(m)
        idxs.append(ik)
        sels.append(sel)
    exps = [jnp.exp(v - vals[0]) for v in vals]
    denom = exps[0] + exps[1] + exps[2] + exps[3]

    chosen = jnp.logical_or(jnp.logical_or(sels[0], sels[1]), jnp.logical_or(sels[2], sels[3]))
    a = jnp.where(chosen, 1.0, 0.0)
    before = lax.broadcasted_iota(I32, (tm, tm), 0) < lax.broadcasted_iota(I32, (tm, tm), 1)
    prior = _dot(a.astype(BF16), jnp.where(before, 1.0, 0.0).astype(BF16)) + carry_ref[:, 0:1]
    for k in range(TOP_K):
        idx_ref[k:k + 1, :] = idxs[k]
        gw_ref[k:k + 1, :] = exps[k] / denom
        rank_ref[k:k + 1, :] = jnp.sum(jnp.where(sels[k], prior, 0.0), axis=0, keepdims=True).astype(I32)
    total = carry_ref[...] + jnp.sum(a, axis=1, keepdims=True)
    carry_ref[...] = total
    cnt_ref[...] = total.astype(I32)


def _router(x, sh, sc, n2, router_w, router_b):
    bsz, seq, d = x.shape
    n_tok = bsz * seq
    tm = min(ROUTE_ROWS, seq)
    nj = seq // tm
    vec = pl.BlockSpec((1, 1, d), lambda b, j: (b, 0, 0))
    per_tok = pl.BlockSpec((TOP_K, tm), lambda b, j: (0, b * nj + j))
    return pl.pallas_call(
        _router_kernel,
        grid=(bsz, nj),
        in_specs=[
            pl.BlockSpec((1, tm, d), lambda b, j: (b, j, 0)), vec, vec,
            _resident((1, d)), _resident((N_EXPERTS, d)), _resident((N_EXPERTS, 1)),
        ],
        out_specs=[
            pl.BlockSpec((tm, d // 2), lambda b, j: (b * nj + j, 0)),
            per_tok, per_tok, per_tok,
            pl.BlockSpec((N_EXPERTS, 128), lambda b, j: (0, 0)),
        ],
        out_shape=[
            jax.ShapeDtypeStruct((n_tok, d // 2), U32),
            jax.ShapeDtypeStruct((TOP_K, n_tok), I32),
            jax.ShapeDtypeStruct((TOP_K, n_tok), F32),
            jax.ShapeDtypeStruct((TOP_K, n_tok), I32),
            jax.ShapeDtypeStruct((N_EXPERTS, 128), I32),
        ],
        scratch_shapes=[pltpu.VMEM((N_EXPERTS, 128), F32)],
        compiler_params=_params(2),
        name="router",
    )(x, sh, sc, n2.reshape(1, d), router_w.T.astype(BF16), router_b.reshape(N_EXPERTS, 1))


def _dest_kernel(start_ref, idx_ref, rank_ref, o_ref):
    idx = idx_ref[...]
    base = jnp.zeros(idx.shape, I32)
    for e in range(N_EXPERTS):
        base = jnp.where(idx == e, start_ref[e], base)
    o_ref[...] = base + rank_ref[...]


def _dest(pad_start, idx, rank):
    n_tok = idx.shape[1]
    tc = min(DEST_COLS, n_tok)
    blk = pl.BlockSpec((TOP_K, tc), lambda i, s: (0, i))
    return pl.pallas_call(
        _dest_kernel,
        grid_spec=pltpu.PrefetchScalarGridSpec(
            num_scalar_prefetch=1, grid=(n_tok // tc,), in_specs=[blk, blk], out_specs=blk),
        out_shape=jax.ShapeDtypeStruct(idx.shape, I32),
        compiler_params=_params(1),
        name="dest",
    )(pad_start, idx, rank)


def _dispatch_kernel(fill_lo_ref, fill_hi_ref, nv_ref, hpk_ref, dest_ref, xs_ref, zero_ref, sem, zsem):
    tm = hpk_ref.shape[0]
    tb = zero_ref.shape[0]
    n_blocks = xs_ref.shape[0] // tb

    def row_copy(t, dst_row):
        return pltpu.make_async_copy(hpk_ref.at[pl.ds(t, 1), :], xs_ref.at[pl.ds(dst_row, 1), :], sem)

    def zero_copy(dst_row):
        return pltpu.make_async_copy(zero_ref.at[pl.ds(0, 1), :], xs_ref.at[pl.ds(dst_row, 1), :], zsem)

    def zero_block_copy(b):
        return pltpu.make_async_copy(zero_ref, xs_ref.at[pl.ds(pl.multiple_of(b * tb, tb), tb), :], zsem)

    @pl.when(pl.program_id(0) == 0)
    def _():
        zero_ref[...] = jnp.zeros(zero_ref.shape, U32)
        for e in range(N_EXPERTS):
            lo, hi = fill_lo_ref[e], fill_hi_ref[e]
            lax.fori_loop(lo, hi, lambda r, c: (zero_copy(r).start(), c)[1], 0)
        lax.fori_loop(nv_ref[0], n_blocks, lambda b, c: (zero_block_copy(b).start(), c)[1], 0)
        for e in range(N_EXPERTS):
            lo, hi = fill_lo_ref[e], fill_hi_ref[e]
            lax.fori_loop(lo, hi, lambda r, c: (zero_copy(r).wait(), c)[1], 0)
        lax.fori_loop(nv_ref[0], n_blocks, lambda b, c: (zero_block_copy(b).wait(), c)[1], 0)

    def issue(t, c):
        for k in range(TOP_K):
            row_copy(t, dest_ref[TOP_K * t + k]).start(priority=k % 2)
        return c

    def drain(t, c):
        for k in range(TOP_K):
            row_copy(0, 0).wait()
        return c

    lax.fori_loop(0, tm, issue, 0, unroll=ISSUE_UNROLL)
    lax.fori_loop(0, tm, drain, 0, unroll=DRAIN_UNROLL)


def _dispatch(fill_lo, fill_hi, n_valid, hpk, dest, n_rows):
    n_tok, half = hpk.shape
    tm = min(DISPATCH_ROWS, n_tok)
    return pl.pallas_call(
        _dispatch_kernel,
        grid_spec=pltpu.PrefetchScalarGridSpec(
            num_scalar_prefetch=3,
            grid=(n_tok // tm,),
            in_specs=[
                pl.BlockSpec((tm, half), lambda i, *_: (i, 0)),
                pl.BlockSpec((TOP_K * tm,), lambda i, *_: (i,), memory_space=pltpu.SMEM),
            ],
            out_specs=pl.BlockSpec(memory_space=pl.ANY),
            scratch_shapes=[pltpu.VMEM((EXPERT_ROWS, half), U32), pltpu.SemaphoreType.DMA,
                            pltpu.SemaphoreType.DMA],
        ),
        out_shape=jax.ShapeDtypeStruct((n_rows, half), U32),
        compiler_params=_params(1),
        name="dispatch",
    )(fill_lo, fill_hi, n_valid, hpk, dest)


def _expert_kernel(be_ref, nv_ref, xs_ref, wgu_ref, bgu_ref, wd_ref, bd_ref, ys_ref, wgu_bf, wd_bf):
    i = pl.program_id(0)

    @pl.when(i >= nv_ref[0])
    def _():
        ys_ref[...] = jnp.zeros(ys_ref.shape, F32)

    @pl.when(jnp.logical_or(i == 0, be_ref[i] != be_ref[jnp.maximum(i - 1, 0)]))
    def _():
        wgu_bf[...] = wgu_ref[0].astype(BF16)
        wd_bf[...] = wd_ref[0].astype(BF16)

    @pl.when(i < nv_ref[0])
    def _():
        w = xs_ref[...]
        lo = lax.bitcast_convert_type(w << 16, F32)
        hi = lax.bitcast_convert_type(w & jnp.uint32(0xFFFF0000), F32)
        xb = jnp.concatenate([lo, hi], axis=1).astype(BF16)
        gu = _dot(xb, wgu_bf[...]) + bgu_ref[0]
        ff = gu.shape[1] // 2
        gate = jnp.minimum(gu[:, :ff], SWIGLU_LIMIT)
        up = jnp.clip(gu[:, ff:], -SWIGLU_LIMIT, SWIGLU_LIMIT)
        glu = gate * jax.nn.sigmoid(SWIGLU_ALPHA * gate)
        ys_ref[...] = _dot(((up + 1.0) * glu).astype(BF16), wd_bf[...]) + bd_ref[0]


def _experts(block_e, n_valid, xs, w_gu, b_gu, w_down, b_down):
    n_rows, half = xs.shape
    _, d, two_f = w_gu.shape
    tb = EXPERT_ROWS

    def rows(i, be, nv):
        return (jnp.minimum(i, nv[0] - 1), 0)

    def by_expert(i, be, nv):
        return (be[i], 0, 0)

    return pl.pallas_call(
        _expert_kernel,
        grid_spec=pltpu.PrefetchScalarGridSpec(
            num_scalar_prefetch=2,
            grid=(n_rows // tb,),
            in_specs=[
                pl.BlockSpec((tb, half), rows),
                pl.BlockSpec((1, d, two_f), by_expert),
                pl.BlockSpec((1, 1, two_f), by_expert),
                pl.BlockSpec((1, two_f // 2, d), by_expert),
                pl.BlockSpec((1, 1, d), by_expert),
            ],
            out_specs=pl.BlockSpec((tb, d), lambda i, be, nv: (i, 0)),
            scratch_shapes=[pltpu.VMEM((d, two_f), BF16), pltpu.VMEM((two_f // 2, d), BF16)],
        ),
        out_shape=jax.ShapeDtypeStruct((n_rows, d), F32),
        compiler_params=_params(1),
        name="experts",
    )(block_e, n_valid, xs, w_gu, b_gu.reshape(N_EXPERTS, 1, two_f), w_down,
      b_down.reshape(N_EXPERTS, 1, d))


def _combine_kernel(dest_ref, ys_ref, x_ref, gw_ref, gt_ref, fg_ref, o_ref, buf_ref, sem, *, final_norm):
    tm = x_ref.shape[0]

    def row_copy(t, k):
        return pltpu.make_async_copy(ys_ref.at[pl.ds(dest_ref[TOP_K * t + k], 1), :],
                                     buf_ref.at[k, pl.ds(t, 1), :], sem)

    def issue(t, c):
        for k in range(TOP_K):
            row_copy(t, k).start(priority=k % 2)
        return c

    def drain(t, c):
        for k in range(TOP_K):
            row_copy(t, k).wait()
        return c

    lax.fori_loop(0, tm, issue, 0, unroll=ISSUE_UNROLL)
    lax.fori_loop(0, tm, drain, 0, unroll=DRAIN_UNROLL)

    gw = gw_ref[...]
    acc = gw[:, 0:1] * buf_ref[0]
    for k in range(1, TOP_K):
        acc = acc + gw[:, k:k + 1] * buf_ref[k]
    y = x_ref[...] + gt_ref[0] * acc
    o_ref[...] = _rms(y, fg_ref[...]) if final_norm else y


def _combine(dest, ys, x, gw, gt, final_g, final_norm):
    bsz, seq, d = x.shape
    n_tok = bsz * seq
    tm = min(COMBINE_ROWS, seq)
    per_seq = seq // tm
    out = pl.pallas_call(
        functools.partial(_combine_kernel, final_norm=final_norm),
        grid=(n_tok // tm,),
        in_specs=[
            pl.BlockSpec((TOP_K * tm,), lambda i: (i,), memory_space=pltpu.SMEM),
            pl.BlockSpec(memory_space=pl.ANY),
            pl.BlockSpec((tm, d), lambda i: (i, 0)),
            pl.BlockSpec((tm, TOP_K), lambda i: (i, 0)),
            pl.BlockSpec((1, 1, d), lambda i: (i // per_seq, 0, 0)),
            pl.BlockSpec((1, d), lambda i: (0, 0)),
        ],
        out_specs=pl.BlockSpec((tm, d), lambda i: (i, 0)),
        out_shape=jax.ShapeDtypeStruct((n_tok, d), F32),
        scratch_shapes=[pltpu.VMEM((TOP_K, tm, d), F32), pltpu.SemaphoreType.DMA],
        compiler_params=_params(1),
        name="combine",
    )(dest, ys, x.reshape(n_tok, d), gw.T, gt, final_g.reshape(1, d))
    return out.reshape(bsz, seq, d)


def _moe(x, sh, sc, gt, n2, router_w, router_b, w_gu, b_gu, w_down, b_down, final_g, final_norm):
    bsz, seq, _ = x.shape
    n_tok = bsz * seq
    tb = EXPERT_ROWS
    n_rows = -(-(n_tok * TOP_K + N_EXPERTS * (tb - 1)) // tb) * tb
    hpk, idx, gw, rank, cnt = _router(x, sh, sc, n2, router_w, router_b)
    counts = cnt[:, 0]
    padded = (counts + tb - 1) // tb * tb
    pad_end = jnp.cumsum(padded)
    pad_start = pad_end - padded
    n_valid = pad_end[-1] // tb
    blocks = jnp.arange(n_rows // tb, dtype=I32)
    block_e = jnp.sum((pad_end[None, :] <= (blocks * tb)[:, None]).astype(I32), axis=1)
    block_e = jnp.minimum(block_e, N_EXPERTS - 1)
    block_e = jnp.where(blocks < n_valid, block_e, block_e[n_valid - 1])
    dest = _dest(pad_start, idx, rank).T.reshape(-1)
    n_valid = n_valid.reshape(1)
    xs = _dispatch(pad_start + counts, pad_end, n_valid, hpk, dest, n_rows)
    ys = _experts(block_e, n_valid, xs, w_gu, b_gu, w_down, b_down)
    return _combine(dest, ys, x, gw, gt, final_g, final_norm)


def kernel(x, c, norm1_g, ada_w, ada_b, w_in, gmlp_norm_g, gmlp_ws, gmlp_bs, w_proj_a, pool_w, pool_scale,
           conv_w, w_proj_c, w_out, norm2_g, router_w, router_b, exp_w_gu, exp_b_gu, exp_w_down,
           exp_b_down, final_g):
    depth = ada_w.shape[0]
    bsz, _, d = x.shape
    mods = _ada(c, ada_w, ada_b)
    for l in range(depth):
        sh1, sc1, g1, sh2, sc2, g2 = [mods[l, :, i * d:(i + 1) * d].reshape(bsz, 1, d) for i in range(6)]
        x = _mixer(x, sh1, sc1, g1, norm1_g[l], w_in[l], gmlp_norm_g[l], gmlp_ws[l], gmlp_bs[l],
                   w_proj_a[l], pool_w[l], pool_scale[l], conv_w[l], w_proj_c[l], w_out[l])
        x = _moe(x, sh2, sc2, g2, norm2_g[l], router_w[l], router_b[l], exp_w_gu[l], exp_b_gu[l],
                 exp_w_down[l], exp_b_down[l], final_g, l == depth - 1)
    return x
```

```python
import functools

import jax
import jax.numpy as jnp
from jax import lax
from jax.experimental import pallas as pl
from jax.experimental.pallas import tpu as pltpu
from jax.experimental.pallas import tpu_sc as plsc

F32 = jnp.float32
BF16 = jnp.bfloat16
I32 = jnp.int32
U32 = jnp.uint32

RMS_EPS = 1e-5
GMLP_HEADS = 8
CHUNK = 128
POOL_WINDOWS = (2, 4, 8, 16)
POOL_CARRY = 16
CONV_K = 3
CONV_CARRY = 8
N_EXPERTS = 32
TOP_K = 4
SWIGLU_LIMIT = 7.0
SWIGLU_ALPHA = 1.702

MIX_ROWS = 512
ROUTE_ROWS = 512
DEST_COLS = 4096
EXPERT_ROWS = 512
COMBINE_ROWS = 256
ISSUE_UNROLL = 4
DRAIN_UNROLL = 16
SC_SUBCORES = 16
SC_WORKERS = 2 * SC_SUBCORES
SC_WINDOW = 128

VMEM_LIMIT = 56 * 1024 * 1024


def _rms(x, g):
    return x * lax.rsqrt(jnp.mean(x * x, axis=-1, keepdims=True) + RMS_EPS) * g


def _dot(a, b):
    return jnp.dot(a, b, preferred_element_type=F32)


def _params(n_axes, vmem=VMEM_LIMIT):
    return pltpu.CompilerParams(dimension_semantics=("arbitrary",) * n_axes, vmem_limit_bytes=vmem)


def _resident(shape):
    zeros = (0,) * len(shape)
    return pl.BlockSpec(shape, lambda *_: zeros, pipeline_mode=pl.Buffered(1))


def _ada_kernel(c_ref, w_ref, b_ref, o_ref):
    c = c_ref[...]
    cond = c * jax.nn.sigmoid(c)
    o_ref[0] = _dot(cond.astype(BF16), w_ref[0].astype(BF16)) + b_ref[0]


def _ada(c, ada_w, ada_b):
    depth, d, six_d = ada_w.shape
    bsz = c.shape[0]
    return pl.pallas_call(
        _ada_kernel,
        grid=(depth, six_d // d),
        in_specs=[
            pl.BlockSpec((bsz, d), lambda l, j: (0, 0)),
            pl.BlockSpec((1, d, d), lambda l, j: (l, 0, j)),
            pl.BlockSpec((1, 1, d), lambda l, j: (l, 0, j)),
        ],
        out_specs=pl.BlockSpec((1, bsz, d), lambda l, j: (l, 0, j)),
        out_shape=jax.ShapeDtypeStruct((depth, bsz, six_d), F32),
        compiler_params=_params(2),
        name="ada",
    )(c, ada_w, ada_b.reshape(depth, 1, six_d))


def _mixer_kernel(x_ref, sh_ref, sc_ref, gt_ref, n1_ref, win_ref, gng_ref, ws_ref, bst_ref, wpa_ref,
                  pw_ref, ps_ref, cw_ref, wpc_ref, wout_ref, o_ref, s_ref, pext_ref, zext_ref):
    tm, d = x_ref.shape[1], x_ref.shape[2]
    j = pl.program_id(1)

    @pl.when(j == 0)
    def _():
        pext_ref[0:POOL_CARRY, :] = jnp.zeros((POOL_CARRY, d), F32)
        zext_ref[0:CONV_CARRY, :] = jnp.zeros((CONV_CARRY, d), F32)

    x = x_ref[0]
    hb = (_rms(x, n1_ref[...]) * (1.0 + sc_ref[0]) + sh_ref[0]).astype(BF16)

    def proj(col):
        return _dot(hb, win_ref[:, col * d:(col + 1) * d])

    u = jax.nn.gelu(proj(0))
    vb = _rms(jax.nn.gelu(proj(1)), gng_ref[...]).astype(BF16)
    hd = d // GMLP_HEADS
    row = lax.broadcasted_iota(I32, (CHUNK, CHUNK), 0)
    col = lax.broadcasted_iota(I32, (CHUNK, CHUNK), 1)
    for h in range(GMLP_HEADS):
        wm = jnp.where(row >= col, ws_ref[h], 0.0).astype(BF16)
        bias = bst_ref[:, h:h + 1]
        for ci in range(tm // CHUNK):
            blk = vb[ci * CHUNK:(ci + 1) * CHUNK, h * hd:(h + 1) * hd]
            s_ref[ci * CHUNK:(ci + 1) * CHUNK, h * hd:(h + 1) * hd] = _dot(wm, blk) + bias
    ya = _dot((u * s_ref[...]).astype(BF16), wpa_ref[...])
    mix = jax.nn.sigmoid(proj(6)) * ya

    p = proj(2)
    pext_ref[POOL_CARRY:POOL_CARRY + tm, :] = p
    pos1 = (j * tm + 1 + lax.broadcasted_iota(I32, (tm, 1), 0))
    gd = d // len(POOL_WINDOWS)
    yb_parts = []
    for gi, w in enumerate(POOL_WINDOWS):
        cs = slice(gi * gd, (gi + 1) * gd)
        acc = p[:, cs]
        for k in range(1, w):
            acc = acc + pext_ref[POOL_CARRY - k:POOL_CARRY - k + tm, cs]
        cnt = jnp.minimum(pos1, w).astype(F32)
        dg = (acc / cnt - p[:, cs]).astype(BF16)
        yb_parts.append(_dot(dg, pw_ref[gi]))
    yb = jnp.concatenate(yb_parts, axis=1) * ps_ref[...]
    pext_ref[0:POOL_CARRY, :] = pext_ref[tm:tm + POOL_CARRY, :]
    mix = mix + jax.nn.sigmoid(proj(7)) * yb

    z = proj(5) * proj(3)
    zext_ref[CONV_CARRY:CONV_CARRY + tm, :] = z
    conv = cw_ref[CONV_K - 1:CONV_K, :] * z
    for k in range(CONV_K - 1):
        lag = CONV_K - 1 - k
        conv = conv + cw_ref[k:k + 1, :] * zext_ref[CONV_CARRY - lag:CONV_CARRY - lag + tm, :]
    zext_ref[0:CONV_CARRY, :] = zext_ref[tm:tm + CONV_CARRY, :]
    yc = _dot((proj(4) * conv).astype(BF16), wpc_ref[...])
    mix = mix + jax.nn.sigmoid(proj(8)) * yc

    o_ref[0] = x + gt_ref[0] * _dot(mix.astype(BF16), wout_ref[...])


def _mixer(x, sh, sc, gt, n1, w_in, gng, ws, bs, wpa, pool_w, pool_scale, conv_w, wpc, w_out):
    bsz, seq, d = x.shape
    tm = min(MIX_ROWS, seq)
    vec = pl.BlockSpec((1, 1, d), lambda b, j: (b, 0, 0))
    tile = pl.BlockSpec((1, tm, d), lambda b, j: (b, j, 0))
    return pl.pallas_call(
        _mixer_kernel,
        grid=(bsz, seq // tm),
        in_specs=[
            tile, vec, vec, vec,
            _resident((1, d)),
            _resident(w_in.shape),
            _resident((1, d)),
            _resident(ws.shape),
            _resident((CHUNK, GMLP_HEADS)),
            _resident(wpa.shape),
            _resident(pool_w.shape),
            _resident((1, d)),
            _resident(conv_w.shape),
            _resident(wpc.shape),
            _resident(w_out.shape),
        ],
        out_specs=tile,
        out_shape=jax.ShapeDtypeStruct(x.shape, F32),
        scratch_shapes=[
            pltpu.VMEM((tm, d), F32),
            pltpu.VMEM((POOL_CARRY + tm, d), F32),
            pltpu.VMEM((CONV_CARRY + tm, d), F32),
        ],
        compiler_params=_params(2),
        name="mixer",
    )(x, sh, sc, gt, n1.reshape(1, d), w_in.astype(BF16), gng.reshape(1, d), ws, bs.T,
      wpa.astype(BF16), pool_w.astype(BF16), pool_scale.reshape(1, d), conv_w, wpc.astype(BF16),
      w_out.astype(BF16))


def _router_kernel(x_ref, sh_ref, sc_ref, n2_ref, rwt_ref, rb_ref,
                   hpk_ref, idx_ref, gw_ref, rank_ref, cnt_ref, carry_ref):
    tm, d = x_ref.shape[1], x_ref.shape[2]
    first = jnp.logical_and(pl.program_id(0) == 0, pl.program_id(1) == 0)

    @pl.when(first)
    def _():
        carry_ref[...] = jnp.zeros(carry_ref.shape, F32)

    hb = (_rms(x_ref[0], n2_ref[...]) * (1.0 + sc_ref[0]) + sh_ref[0]).astype(BF16)
    bits = lax.bitcast_convert_type(hb.astype(F32), U32)
    hpk_ref[...] = (bits[:, :d // 2] >> 16) | bits[:, d // 2:]

    logits = lax.dot_general(rwt_ref[...], hb, (((1,), (1,)), ((), ())),
                             preferred_element_type=F32) + rb_ref[...]
    iota_e = lax.broadcasted_iota(I32, logits.shape, 0)
    vals, idxs, sels = [], [], []
    rest = logits
    for _ in range(TOP_K):
        m = jnp.max(rest, axis=0, keepdims=True)
        ik = jnp.min(jnp.where(rest == m, iota_e, N_EXPERTS), axis=0, keepdims=True)
        sel = iota_e == ik
        rest = jnp.where(sel, -jnp.inf, rest)
        vals.append(m)
        idxs.append(ik)
        sels.append(sel)
    exps = [jnp.exp(v - vals[0]) for v in vals]
    denom = exps[0] + exps[1] + exps[2] + exps[3]

    chosen = jnp.logical_or(jnp.logical_or(sels[0], sels[1]), jnp.logical_or(sels[2], sels[3]))
    a = jnp.where(chosen, 1.0, 0.0)
    before = lax.broadcasted_iota(I32, (tm, tm), 0) < lax.broadcasted_iota(I32, (tm, tm), 1)
    prior = _dot(a.astype(BF16), jnp.where(before, 1.0, 0.0).astype(BF16)) + carry_ref[:, 0:1]
    for k in range(TOP_K):
        idx_ref[k:k + 1, :] = idxs[k]
        gw_ref[k:k + 1, :] = exps[k] / denom
        rank_ref[k:k + 1, :] = jnp.sum(jnp.where(sels[k], prior, 0.0), axis=0, keepdims=True).astype(I32)
    total = carry_ref[...] + jnp.sum(a, axis=1, keepdims=True)
    carry_ref[...] = total
    cnt_ref[...] = total.astype(I32)


def _router(x, sh, sc, n2, router_w, router_b):
    bsz, seq, d = x.shape
    n_tok = bsz * seq
    tm = min(ROUTE_ROWS, seq)
    nj = seq // tm
    vec = pl.BlockSpec((1, 1, d), lambda b, j: (b, 0, 0))
    per_tok = pl.BlockSpec((TOP_K, tm), lambda b, j: (0, b * nj + j))
    return pl.pallas_call(
        _router_kernel,
        grid=(bsz, nj),
        in_specs=[
            pl.BlockSpec((1, tm, d), lambda b, j: (b, j, 0)), vec, vec,
            _resident((1, d)), _resident((N_EXPERTS, d)), _resident((N_EXPERTS, 1)),
        ],
        out_specs=[
            pl.BlockSpec((tm, d // 2), lambda b, j: (b * nj + j, 0)),
            per_tok, per_tok, per_tok,
            pl.BlockSpec((N_EXPERTS, 128), lambda b, j: (0, 0)),
        ],
        out_shape=[
            jax.ShapeDtypeStruct((n_tok, d // 2), U32),
            jax.ShapeDtypeStruct((TOP_K, n_tok), I32),
            jax.ShapeDtypeStruct((TOP_K, n_tok), F32),
            jax.ShapeDtypeStruct((TOP_K, n_tok), I32),
            jax.ShapeDtypeStruct((N_EXPERTS, 128), I32),
        ],
        scratch_shapes=[pltpu.VMEM((N_EXPERTS, 128), F32)],
        compiler_params=_params(2),
        name="router",
    )(x, sh, sc, n2.reshape(1, d), router_w.T.astype(BF16), router_b.reshape(N_EXPERTS, 1))


def _dest_kernel(start_ref, idx_ref, rank_ref, o_ref):
    idx = idx_ref[...]
    base = jnp.zeros(idx.shape, I32)
    for e in range(N_EXPERTS):
        base = jnp.where(idx == e, start_ref[e], base)
    o_ref[...] = base + rank_ref[...]


def _dest(pad_start, idx, rank):
    n_tok = idx.shape[1]
    tc = min(DEST_COLS, n_tok)
    blk = pl.BlockSpec((TOP_K, tc), lambda i, s: (0, i))
    return pl.pallas_call(
        _dest_kernel,
        grid_spec=pltpu.PrefetchScalarGridSpec(
            num_scalar_prefetch=1, grid=(n_tok // tc,), in_specs=[blk, blk], out_specs=blk),
        out_shape=jax.ShapeDtypeStruct(idx.shape, I32),
        compiler_params=_params(1),
        name="dest",
    )(pad_start, idx, rank)


def _dispatch(hpk, dest, n_rows):
    n_tok, half = hpk.shape
    win = SC_WINDOW
    assert n_tok % (win * SC_WORKERS) == 0
    per_worker = n_tok // win // SC_WORKERS
    mesh = plsc.VectorSubcoreMesh(core_axis_name="c", subcore_axis_name="s")

    @pl.kernel(out_type=jax.ShapeDtypeStruct((n_rows, half), U32), mesh=mesh,
               scratch_types=[pltpu.VMEM((win, half), U32), pltpu.VMEM((TOP_K, win), I32)])
    def scatter_rows(hpk_hbm, dest_hbm, xs_hbm, xbuf, ibuf):
        worker = lax.axis_index("c") * SC_SUBCORES + lax.axis_index("s")

        @pl.loop(0, per_worker)
        def _(j):
            t0 = pl.multiple_of((worker * per_worker + j) * win, win)
            pltpu.sync_copy(hpk_hbm.at[pl.ds(t0, win)], xbuf)
            pltpu.sync_copy(dest_hbm.at[:, pl.ds(t0, win)], ibuf)
            for k in range(TOP_K):
                pltpu.sync_copy(xbuf, xs_hbm.at[ibuf.at[k]])

    return scatter_rows(hpk, dest)


def _expert_kernel(be_ref, nv_ref, live_ref, xs_ref, wgu_ref, bgu_ref, wd_ref, bd_ref, ys_ref, wgu_bf, wd_bf):
    i = pl.program_id(0)

    @pl.when(i >= nv_ref[0])
    def _():
        ys_ref[...] = jnp.zeros(ys_ref.shape, F32)

    @pl.when(jnp.logical_or(i == 0, be_ref[i] != be_ref[jnp.maximum(i - 1, 0)]))
    def _():
        wgu_bf[...] = wgu_ref[0, 0].astype(BF16)
        wd_bf[...] = wd_ref[0, 0].astype(BF16)

    @pl.when(i < nv_ref[0])
    def _():
        live = lax.broadcasted_iota(I32, (xs_ref.shape[0], 1), 0) < live_ref[i]
        w = xs_ref[...]
        lo = lax.bitcast_convert_type(w << 16, F32)
        hi = lax.bitcast_convert_type(w & jnp.uint32(0xFFFF0000), F32)
        xb = jnp.where(live, jnp.concatenate([lo, hi], axis=1), 0.0).astype(BF16)
        gu = _dot(xb, wgu_bf[...]) + bgu_ref[0, 0]
        ff = gu.shape[1] // 2
        gate = jnp.minimum(gu[:, :ff], SWIGLU_LIMIT)
        up = jnp.clip(gu[:, ff:], -SWIGLU_LIMIT, SWIGLU_LIMIT)
        glu = gate * jax.nn.sigmoid(SWIGLU_ALPHA * gate)
        ys_ref[...] = _dot(((up + 1.0) * glu).astype(BF16), wd_bf[...]) + bd_ref[0, 0]


def _experts(layer, block_e, n_valid, live_rows, xs, w_gu, b_gu, w_down, b_down):
    n_rows, half = xs.shape
    depth, _, d, two_f = w_gu.shape
    tb = EXPERT_ROWS

    def rows(i, be, nv, lv):
        return (jnp.minimum(i, nv[0] - 1), 0)

    def by_expert(i, be, nv, lv):
        return (layer, be[i], 0, 0)

    return pl.pallas_call(
        _expert_kernel,
        grid_spec=pltpu.PrefetchScalarGridSpec(
            num_scalar_prefetch=3,
            grid=(n_rows // tb,),
            in_specs=[
                pl.BlockSpec((tb, half), rows),
                pl.BlockSpec((1, 1, d, two_f), by_expert),
                pl.BlockSpec((1, 1, 1, two_f), by_expert),
                pl.BlockSpec((1, 1, two_f // 2, d), by_expert),
                pl.BlockSpec((1, 1, 1, d), by_expert),
            ],
            out_specs=pl.BlockSpec((tb, d), lambda i, be, nv, lv: (i, 0)),
            scratch_shapes=[pltpu.VMEM((d, two_f), BF16), pltpu.VMEM((two_f // 2, d), BF16)],
        ),
        out_shape=jax.ShapeDtypeStruct((n_rows, d), F32),
        compiler_params=_params(1),
        name="experts",
    )(block_e, n_valid, live_rows, xs, w_gu, b_gu.reshape(depth, N_EXPERTS, 1, two_f), w_down,
      b_down.reshape(depth, N_EXPERTS, 1, d))


def _combine_kernel(dest_ref, ys_ref, x_ref, gw_ref, gt_ref, fg_ref, o_ref, buf_ref, sem, *, final_norm):
    tm = x_ref.shape[0]

    def row_copy(t, k):
        return pltpu.make_async_copy(ys_ref.at[pl.ds(dest_ref[TOP_K * t + k], 1), :],
                                     buf_ref.at[k, pl.ds(t, 1), :], sem)

    def issue(t, c):
        for k in range(TOP_K):
            row_copy(t, k).start(priority=k % 2)
        return c

    def drain(t, c):
        for k in range(TOP_K):
            row_copy(t, k).wait()
        return c

    lax.fori_loop(0, tm, issue, 0, unroll=ISSUE_UNROLL)
    lax.fori_loop(0, tm, drain, 0, unroll=DRAIN_UNROLL)

    gw = gw_ref[...]
    acc = gw[:, 0:1] * buf_ref[0]
    for k in range(1, TOP_K):
        acc = acc + gw[:, k:k + 1] * buf_ref[k]
    y = x_ref[...] + gt_ref[0] * acc
    o_ref[...] = _rms(y, fg_ref[...]) if final_norm else y


def _combine(dest, ys, x, gw, gt, final_g, final_norm):
    bsz, seq, d = x.shape
    n_tok = bsz * seq
    tm = min(COMBINE_ROWS, seq)
    per_seq = seq // tm
    out = pl.pallas_call(
        functools.partial(_combine_kernel, final_norm=final_norm),
        grid=(n_tok // tm,),
        in_specs=[
            pl.BlockSpec((TOP_K * tm,), lambda i: (i,), memory_space=pltpu.SMEM),
            pl.BlockSpec(memory_space=pl.ANY),
            pl.BlockSpec((tm, d), lambda i: (i, 0)),
            pl.BlockSpec((tm, TOP_K), lambda i: (i, 0)),
            pl.BlockSpec((1, 1, d), lambda i: (i // per_seq, 0, 0)),
            pl.BlockSpec((1, d), lambda i: (0, 0)),
        ],
        out_specs=pl.BlockSpec((tm, d), lambda i: (i, 0)),
        out_shape=jax.ShapeDtypeStruct((n_tok, d), F32),
        scratch_shapes=[pltpu.VMEM((TOP_K, tm, d), F32), pltpu.SemaphoreType.DMA],
        compiler_params=_params(1),
        name="combine",
    )(dest, ys, x.reshape(n_tok, d), gw.T, gt, final_g.reshape(1, d))
    return out.reshape(bsz, seq, d)


def _moe(layer, x, sh, sc, gt, n2, router_w, router_b, w_gu, b_gu, w_down, b_down, final_g, final_norm):
    bsz, seq, _ = x.shape
    n_tok = bsz * seq
    tb = EXPERT_ROWS
    n_rows = -(-(n_tok * TOP_K + N_EXPERTS * (tb - 1)) // tb) * tb
    hpk, idx, gw, rank, cnt = _router(x, sh, sc, n2, router_w, router_b)
    counts = cnt[:, 0]
    padded = (counts + tb - 1) // tb * tb
    pad_end = jnp.cumsum(padded)
    pad_start = pad_end - padded
    n_valid = pad_end[-1] // tb
    blocks = jnp.arange(n_rows // tb, dtype=I32)
    block_e = jnp.sum((pad_end[None, :] <= (blocks * tb)[:, None]).astype(I32), axis=1)
    block_e = jnp.minimum(block_e, N_EXPERTS - 1)
    block_e = jnp.where(blocks < n_valid, block_e, block_e[n_valid - 1])
    live_rows = jnp.clip(counts[block_e] - (blocks * tb - pad_start[block_e]), 0, tb)
    live_rows = jnp.where(blocks < n_valid, live_rows, 0).astype(I32)
    dest = _dest(pad_start, idx, rank)
    xs = _dispatch(hpk, dest, n_rows)
    ys = _experts(layer, block_e, n_valid.reshape(1), live_rows, xs, w_gu, b_gu, w_down, b_down)
    return _combine(dest.T.reshape(-1), ys, x, gw, gt, final_g, final_norm)


def kernel(x, c, norm1_g, ada_w, ada_b, w_in, gmlp_norm_g, gmlp_ws, gmlp_bs, w_proj_a, pool_w, pool_scale,
           conv_w, w_proj_c, w_out, norm2_g, router_w, router_b, exp_w_gu, exp_b_gu, exp_w_down,
           exp_b_down, final_g):
    depth = ada_w.shape[0]
    bsz, _, d = x.shape
    mods = _ada(c, ada_w, ada_b)
    for l in range(depth):
        sh1, sc1, g1, sh2, sc2, g2 = [mods[l, :, i * d:(i + 1) * d].reshape(bsz, 1, d) for i in range(6)]
        x = _mixer(x, sh1, sc1, g1, norm1_g[l], w_in[l], gmlp_norm_g[l], gmlp_ws[l], gmlp_bs[l],
                   w_proj_a[l], pool_w[l], pool_scale[l], conv_w[l], w_proj_c[l], w_out[l])
        x = _moe(l, x, sh2, sc2, g2, norm2_g[l], router_w[l], router_b[l], exp_w_gu, exp_b_gu,
                 exp_w_down, exp_b_down, final_g, l == depth - 1)
    return x
```

```python
import functools

import jax
import jax.numpy as jnp
from jax import lax
from jax.experimental import pallas as pl
from jax.experimental.pallas import tpu as pltpu
from jax.experimental.pallas import tpu_sc as plsc

F32 = jnp.float32
BF16 = jnp.bfloat16
I32 = jnp.int32
U32 = jnp.uint32

RMS_EPS = 1e-5
GMLP_HEADS = 8
CHUNK = 128
POOL_WINDOWS = (2, 4, 8, 16)
POOL_CARRY = 16
CONV_K = 3
CONV_CARRY = 8
N_EXPERTS = 32
TOP_K = 4
SWIGLU_LIMIT = 7.0
SWIGLU_ALPHA = 1.702

MIX_ROWS = 512
ROUTE_ROWS = 512
DEST_COLS = 4096
EXPERT_ROWS = 512
COMBINE_ROWS = 512
SC_SUBCORES = 16
SC_WORKERS = 2 * SC_SUBCORES
SC_WINDOW = 128
COL_SPLIT = 2

VMEM_LIMIT = 56 * 1024 * 1024


def _rms(x, g):
    return x * lax.rsqrt(jnp.mean(x * x, axis=-1, keepdims=True) + RMS_EPS) * g


def _dot(a, b):
    return jnp.dot(a, b, preferred_element_type=F32)


def _params(n_axes, vmem=VMEM_LIMIT):
    return pltpu.CompilerParams(dimension_semantics=("arbitrary",) * n_axes, vmem_limit_bytes=vmem)


def _resident(shape):
    zeros = (0,) * len(shape)
    return pl.BlockSpec(shape, lambda *_: zeros, pipeline_mode=pl.Buffered(1))


def _ada_kernel(c_ref, w_ref, b_ref, o_ref):
    c = c_ref[...]
    cond = c * jax.nn.sigmoid(c)
    o_ref[0] = _dot(cond.astype(BF16), w_ref[0].astype(BF16)) + b_ref[0]


def _ada(c, ada_w, ada_b):
    depth, d, six_d = ada_w.shape
    bsz = c.shape[0]
    return pl.pallas_call(
        _ada_kernel,
        grid=(depth, six_d // d),
        in_specs=[
            pl.BlockSpec((bsz, d), lambda l, j: (0, 0)),
            pl.BlockSpec((1, d, d), lambda l, j: (l, 0, j)),
            pl.BlockSpec((1, 1, d), lambda l, j: (l, 0, j)),
        ],
        out_specs=pl.BlockSpec((1, bsz, d), lambda l, j: (l, 0, j)),
        out_shape=jax.ShapeDtypeStruct((depth, bsz, six_d), F32),
        compiler_params=_params(2),
        name="ada",
    )(c, ada_w, ada_b.reshape(depth, 1, six_d))


def _mixer_kernel(x_ref, sh_ref, sc_ref, gt_ref, n1_ref, win_ref, gng_ref, ws_ref, bst_ref, wpa_ref,
                  pw_ref, ps_ref, cw_ref, wpc_ref, wout_ref, o_ref, s_ref, pext_ref, zext_ref):
    tm, d = x_ref.shape[1], x_ref.shape[2]
    j = pl.program_id(1)

    @pl.when(j == 0)
    def _():
        pext_ref[0:POOL_CARRY, :] = jnp.zeros((POOL_CARRY, d), F32)
        zext_ref[0:CONV_CARRY, :] = jnp.zeros((CONV_CARRY, d), F32)

    x = x_ref[0]
    hb = (_rms(x, n1_ref[...]) * (1.0 + sc_ref[0]) + sh_ref[0]).astype(BF16)

    def proj(col):
        return _dot(hb, win_ref[:, col * d:(col + 1) * d])

    u = jax.nn.gelu(proj(0))
    vb = _rms(jax.nn.gelu(proj(1)), gng_ref[...]).astype(BF16)
    hd = d // GMLP_HEADS
    row = lax.broadcasted_iota(I32, (CHUNK, CHUNK), 0)
    col = lax.broadcasted_iota(I32, (CHUNK, CHUNK), 1)
    for h in range(GMLP_HEADS):
        wm = jnp.where(row >= col, ws_ref[h], 0.0).astype(BF16)
        bias = bst_ref[:, h:h + 1]
        for ci in range(tm // CHUNK):
            blk = vb[ci * CHUNK:(ci + 1) * CHUNK, h * hd:(h + 1) * hd]
            s_ref[ci * CHUNK:(ci + 1) * CHUNK, h * hd:(h + 1) * hd] = _dot(wm, blk) + bias
    ya = _dot((u * s_ref[...]).astype(BF16), wpa_ref[...])
    mix = jax.nn.sigmoid(proj(6)) * ya

    p = proj(2)
    pext_ref[POOL_CARRY:POOL_CARRY + tm, :] = p
    pos1 = (j * tm + 1 + lax.broadcasted_iota(I32, (tm, 1), 0))
    gd = d // len(POOL_WINDOWS)
    yb_parts = []
    for gi, w in enumerate(POOL_WINDOWS):
        cs = slice(gi * gd, (gi + 1) * gd)
        acc = p[:, cs]
        for k in range(1, w):
            acc = acc + pext_ref[POOL_CARRY - k:POOL_CARRY - k + tm, cs]
        cnt = jnp.minimum(pos1, w).astype(F32)
        dg = (acc / cnt - p[:, cs]).astype(BF16)
        yb_parts.append(_dot(dg, pw_ref[gi]))
    yb = jnp.concatenate(yb_parts, axis=1) * ps_ref[...]
    pext_ref[0:POOL_CARRY, :] = pext_ref[tm:tm + POOL_CARRY, :]
    mix = mix + jax.nn.sigmoid(proj(7)) * yb

    z = proj(5) * proj(3)
    zext_ref[CONV_CARRY:CONV_CARRY + tm, :] = z
    conv = cw_ref[CONV_K - 1:CONV_K, :] * z
    for k in range(CONV_K - 1):
        lag = CONV_K - 1 - k
        conv = conv + cw_ref[k:k + 1, :] * zext_ref[CONV_CARRY - lag:CONV_CARRY - lag + tm, :]
    zext_ref[0:CONV_CARRY, :] = zext_ref[tm:tm + CONV_CARRY, :]
    yc = _dot((proj(4) * conv).astype(BF16), wpc_ref[...])
    mix = mix + jax.nn.sigmoid(proj(8)) * yc

    o_ref[0] = x + gt_ref[0] * _dot(mix.astype(BF16), wout_ref[...])


def _mixer(x, sh, sc, gt, n1, w_in, gng, ws, bs, wpa, pool_w, pool_scale, conv_w, wpc, w_out):
    bsz, seq, d = x.shape
    tm = min(MIX_ROWS, seq)
    vec = pl.BlockSpec((1, 1, d), lambda b, j: (b, 0, 0))
    tile = pl.BlockSpec((1, tm, d), lambda b, j: (b, j, 0))
    return pl.pallas_call(
        _mixer_kernel,
        grid=(bsz, seq // tm),
        in_specs=[
            tile, vec, vec, vec,
            _resident((1, d)),
            _resident(w_in.shape),
            _resident((1, d)),
            _resident(ws.shape),
            _resident((CHUNK, GMLP_HEADS)),
            _resident(wpa.shape),
            _resident(pool_w.shape),
            _resident((1, d)),
            _resident(conv_w.shape),
            _resident(wpc.shape),
            _resident(w_out.shape),
        ],
        out_specs=tile,
        out_shape=jax.ShapeDtypeStruct(x.shape, F32),
        scratch_shapes=[
            pltpu.VMEM((tm, d), F32),
            pltpu.VMEM((POOL_CARRY + tm, d), F32),
            pltpu.VMEM((CONV_CARRY + tm, d), F32),
        ],
        compiler_params=_params(2),
        name="mixer",
    )(x, sh, sc, gt, n1.reshape(1, d), w_in.astype(BF16), gng.reshape(1, d), ws, bs.T,
      wpa.astype(BF16), pool_w.astype(BF16), pool_scale.reshape(1, d), conv_w, wpc.astype(BF16),
      w_out.astype(BF16))


def _router_kernel(x_ref, sh_ref, sc_ref, n2_ref, rwt_ref, rb_ref,
                   hpk_ref, idx_ref, gw_ref, rank_ref, cnt_ref, carry_ref):
    tm, d = x_ref.shape[1], x_ref.shape[2]
    first = jnp.logical_and(pl.program_id(0) == 0, pl.program_id(1) == 0)

    @pl.when(first)
    def _():
        carry_ref[...] = jnp.zeros(carry_ref.shape, F32)

    hb = (_rms(x_ref[0], n2_ref[...]) * (1.0 + sc_ref[0]) + sh_ref[0]).astype(BF16)
    bits = lax.bitcast_convert_type(hb.astype(F32), U32)
    hpk_ref[...] = (bits[:, :d // 2] >> 16) | bits[:, d // 2:]

    logits = lax.dot_general(rwt_ref[...], hb, (((1,), (1,)), ((), ())),
                             preferred_element_type=F32) + rb_ref[...]
    iota_e = lax.broadcasted_iota(I32, logits.shape, 0)
    vals, idxs, sels = [], [], []
    rest = logits
    for _ in range(TOP_K):
        m = jnp.max(rest, axis=0, keepdims=True)
        ik = jnp.min(jnp.where(rest == m, iota_e, N_EXPERTS), axis=0, keepdims=True)
        sel = iota_e == ik
        rest = jnp.where(sel, -jnp.inf, rest)
        vals.append(m)
        idxs.append(ik)
        sels.append(sel)
    exps = [jnp.exp(v - vals[0]) for v in vals]
    denom = exps[0] + exps[1] + exps[2] + exps[3]

    chosen = jnp.logical_or(jnp.logical_or(sels[0], sels[1]), jnp.logical_or(sels[2], sels[3]))
    a = jnp.where(chosen, 1.0, 0.0)
    before = lax.broadcasted_iota(I32, (tm, tm), 0) < lax.broadcasted_iota(I32, (tm, tm), 1)
    prior = _dot(a.astype(BF16), jnp.where(before, 1.0, 0.0).astype(BF16)) + carry_ref[:, 0:1]
    for k in range(TOP_K):
        idx_ref[k:k + 1, :] = idxs[k]
        gw_ref[k:k + 1, :] = exps[k] / denom
        rank_ref[k:k + 1, :] = jnp.sum(jnp.where(sels[k], prior, 0.0), axis=0, keepdims=True).astype(I32)
    total = carry_ref[...] + jnp.sum(a, axis=1, keepdims=True)
    carry_ref[...] = total
    cnt_ref[...] = total.astype(I32)


def _router(x, sh, sc, n2, router_w, router_b):
    bsz, seq, d = x.shape
    n_tok = bsz * seq
    tm = min(ROUTE_ROWS, seq)
    nj = seq // tm
    vec = pl.BlockSpec((1, 1, d), lambda b, j: (b, 0, 0))
    per_tok = pl.BlockSpec((TOP_K, tm), lambda b, j: (0, b * nj + j))
    return pl.pallas_call(
        _router_kernel,
        grid=(bsz, nj),
        in_specs=[
            pl.BlockSpec((1, tm, d), lambda b, j: (b, j, 0)), vec, vec,
            _resident((1, d)), _resident((N_EXPERTS, d)), _resident((N_EXPERTS, 1)),
        ],
        out_specs=[
            pl.BlockSpec((tm, d // 2), lambda b, j: (b * nj + j, 0)),
            per_tok, per_tok, per_tok,
            pl.BlockSpec((N_EXPERTS, 128), lambda b, j: (0, 0)),
        ],
        out_shape=[
            jax.ShapeDtypeStruct((n_tok, d // 2), U32),
            jax.ShapeDtypeStruct((TOP_K, n_tok), I32),
            jax.ShapeDtypeStruct((TOP_K, n_tok), F32),
            jax.ShapeDtypeStruct((TOP_K, n_tok), I32),
            jax.ShapeDtypeStruct((N_EXPERTS, 128), I32),
        ],
        scratch_shapes=[pltpu.VMEM((N_EXPERTS, 128), F32)],
        compiler_params=_params(2),
        name="router",
    )(x, sh, sc, n2.reshape(1, d), router_w.T.astype(BF16), router_b.reshape(N_EXPERTS, 1))


def _dest_kernel(start_ref, idx_ref, rank_ref, o_ref):
    idx = idx_ref[...]
    base = jnp.zeros(idx.shape, I32)
    for e in range(N_EXPERTS):
        base = jnp.where(idx == e, start_ref[e], base)
    o_ref[...] = base + rank_ref[...]


def _dest(pad_start, idx, rank):
    n_tok = idx.shape[1]
    tc = min(DEST_COLS, n_tok)
    blk = pl.BlockSpec((TOP_K, tc), lambda i, s: (0, i))
    return pl.pallas_call(
        _dest_kernel,
        grid_spec=pltpu.PrefetchScalarGridSpec(
            num_scalar_prefetch=1, grid=(n_tok // tc,), in_specs=[blk, blk], out_specs=blk),
        out_shape=jax.ShapeDtypeStruct(idx.shape, I32),
        compiler_params=_params(1),
        name="dest",
    )(pad_start, idx, rank)


def _dispatch(hpk, dest, n_rows):
    n_tok, half = hpk.shape
    win = SC_WINDOW
    assert n_tok % (win * SC_WORKERS) == 0
    per_worker = n_tok // win // SC_WORKERS
    mesh = plsc.VectorSubcoreMesh(core_axis_name="c", subcore_axis_name="s")

    @pl.kernel(out_type=jax.ShapeDtypeStruct((n_rows, half), U32), mesh=mesh,
               scratch_types=[pltpu.VMEM((win, half), U32), pltpu.VMEM((TOP_K, win), I32)])
    def scatter_rows(hpk_hbm, dest_hbm, xs_hbm, xbuf, ibuf):
        worker = lax.axis_index("c") * SC_SUBCORES + lax.axis_index("s")

        @pl.loop(0, per_worker)
        def _(j):
            t0 = pl.multiple_of((worker * per_worker + j) * win, win)
            pltpu.sync_copy(hpk_hbm.at[pl.ds(t0, win)], xbuf)
            pltpu.sync_copy(dest_hbm.at[:, pl.ds(t0, win)], ibuf)
            for k in range(TOP_K):
                pltpu.sync_copy(xbuf, xs_hbm.at[ibuf.at[k]])

    return scatter_rows(hpk, dest)


def _expert_kernel(be_ref, nv_ref, live_ref, xs_ref, wgu_ref, bgu_ref, wd_ref, bd_ref, ys_ref, wgu_bf, wd_bf):
    i = pl.program_id(0)

    @pl.when(i >= nv_ref[0])
    def _():
        ys_ref[...] = jnp.zeros(ys_ref.shape, F32)

    @pl.when(jnp.logical_or(i == 0, be_ref[i] != be_ref[jnp.maximum(i - 1, 0)]))
    def _():
        wgu_bf[...] = wgu_ref[0, 0].astype(BF16)
        wd_bf[...] = wd_ref[0, 0].astype(BF16)

    @pl.when(i < nv_ref[0])
    def _():
        live = lax.broadcasted_iota(I32, (xs_ref.shape[0], 1), 0) < live_ref[i]
        w = xs_ref[...]
        lo = lax.bitcast_convert_type(w << 16, F32)
        hi = lax.bitcast_convert_type(w & jnp.uint32(0xFFFF0000), F32)
        xb = jnp.where(live, jnp.concatenate([lo, hi], axis=1), 0.0).astype(BF16)
        gu = _dot(xb, wgu_bf[...]) + bgu_ref[0, 0]
        ff = gu.shape[1] // 2
        gate = jnp.minimum(gu[:, :ff], SWIGLU_LIMIT)
        up = jnp.clip(gu[:, ff:], -SWIGLU_LIMIT, SWIGLU_LIMIT)
        glu = gate * jax.nn.sigmoid(SWIGLU_ALPHA * gate)
        y = _dot(((up + 1.0) * glu).astype(BF16), wd_bf[...]) + bd_ref[0, 0]
        part = ys_ref.shape[2]
        for h in range(ys_ref.shape[0]):
            ys_ref[h] = y[:, h * part:(h + 1) * part]


def _experts(layer, block_e, n_valid, live_rows, xs, w_gu, b_gu, w_down, b_down):
    n_rows, half = xs.shape
    depth, _, d, two_f = w_gu.shape
    tb = EXPERT_ROWS

    def rows(i, be, nv, lv):
        return (jnp.minimum(i, nv[0] - 1), 0)

    def by_expert(i, be, nv, lv):
        return (layer, be[i], 0, 0)

    return pl.pallas_call(
        _expert_kernel,
        grid_spec=pltpu.PrefetchScalarGridSpec(
            num_scalar_prefetch=3,
            grid=(n_rows // tb,),
            in_specs=[
                pl.BlockSpec((tb, half), rows),
                pl.BlockSpec((1, 1, d, two_f), by_expert),
                pl.BlockSpec((1, 1, 1, two_f), by_expert),
                pl.BlockSpec((1, 1, two_f // 2, d), by_expert),
                pl.BlockSpec((1, 1, 1, d), by_expert),
            ],
            out_specs=pl.BlockSpec((COL_SPLIT, tb, d // COL_SPLIT), lambda i, be, nv, lv: (0, i, 0)),
            scratch_shapes=[pltpu.VMEM((d, two_f), BF16), pltpu.VMEM((two_f // 2, d), BF16)],
        ),
        out_shape=jax.ShapeDtypeStruct((COL_SPLIT, n_rows, d // COL_SPLIT), F32),
        compiler_params=_params(1),
        name="experts",
    )(block_e, n_valid, live_rows, xs, w_gu, b_gu.reshape(depth, N_EXPERTS, 1, two_f), w_down,
      b_down.reshape(depth, N_EXPERTS, 1, d))


def _gather_rows(dest, ys, n_tok):
    n_rows, part = ys.shape[1], ys.shape[2]
    win = SC_WINDOW
    n_units = n_tok * COL_SPLIT
    assert n_units % (win * SC_WORKERS) == 0
    per_worker = n_units // win // SC_WORKERS
    unit_idx = jnp.concatenate([dest + h * n_rows for h in range(COL_SPLIT)], axis=1)
    mesh = plsc.VectorSubcoreMesh(core_axis_name="c", subcore_axis_name="s")

    @pl.kernel(out_type=jax.ShapeDtypeStruct((TOP_K, n_units, part), F32), mesh=mesh,
               scratch_types=[pltpu.VMEM((win, part), F32), pltpu.VMEM((TOP_K, win), I32)])
    def gather(ys_hbm, idx_hbm, g_hbm, buf, ibuf):
        worker = lax.axis_index("c") * SC_SUBCORES + lax.axis_index("s")

        @pl.loop(0, per_worker)
        def _(j):
            u0 = pl.multiple_of((worker * per_worker + j) * win, win)
            pltpu.sync_copy(idx_hbm.at[:, pl.ds(u0, win)], ibuf)
            for k in range(TOP_K):
                pltpu.sync_copy(ys_hbm.at[ibuf.at[k]], buf)
                pltpu.sync_copy(buf, g_hbm.at[k, pl.ds(u0, win)])

    g = gather(ys.reshape(COL_SPLIT * n_rows, part), unit_idx)
    return g.reshape(TOP_K, COL_SPLIT, n_tok, part)


def _combine_kernel(g_ref, x_ref, gw_ref, gt_ref, fg_ref, o_ref, *, final_norm):
    gw = gw_ref[...]
    acc = None
    for k in range(TOP_K):
        row = jnp.concatenate([g_ref[k, h] for h in range(g_ref.shape[1])], axis=1)
        term = gw[:, k:k + 1] * row
        acc = term if acc is None else acc + term
    y = x_ref[...] + gt_ref[0] * acc
    o_ref[...] = _rms(y, fg_ref[...]) if final_norm else y


def _combine(g, x, gw, gt, final_g, final_norm):
    bsz, seq, d = x.shape
    n_tok = bsz * seq
    tm = min(COMBINE_ROWS, seq)
    per_seq = seq // tm
    out = pl.pallas_call(
        functools.partial(_combine_kernel, final_norm=final_norm),
        grid=(n_tok // tm,),
        in_specs=[
            pl.BlockSpec((TOP_K, COL_SPLIT, tm, d // COL_SPLIT), lambda i: (0, 0, i, 0)),
            pl.BlockSpec((tm, d), lambda i: (i, 0)),
            pl.BlockSpec((tm, TOP_K), lambda i: (i, 0)),
            pl.BlockSpec((1, 1, d), lambda i: (i // per_seq, 0, 0)),
            pl.BlockSpec((1, d), lambda i: (0, 0)),
        ],
        out_specs=pl.BlockSpec((tm, d), lambda i: (i, 0)),
        out_shape=jax.ShapeDtypeStruct((n_tok, d), F32),
        compiler_params=_params(1),
        name="combine",
    )(g, x.reshape(n_tok, d), gw.T, gt, final_g.reshape(1, d))
    return out.reshape(bsz, seq, d)


def _moe(layer, x, sh, sc, gt, n2, router_w, router_b, w_gu, b_gu, w_down, b_down, final_g, final_norm):
    bsz, seq, _ = x.shape
    n_tok = bsz * seq
    tb = EXPERT_ROWS
    n_rows = -(-(n_tok * TOP_K + N_EXPERTS * (tb - 1)) // tb) * tb
    hpk, idx, gw, rank, cnt = _router(x, sh, sc, n2, router_w, router_b)
    counts = cnt[:, 0]
    padded = (counts + tb - 1) // tb * tb
    pad_end = jnp.cumsum(padded)
    pad_start = pad_end - padded
    n_valid = pad_end[-1] // tb
    blocks = jnp.arange(n_rows // tb, dtype=I32)
    block_e = jnp.sum((pad_end[None, :] <= (blocks * tb)[:, None]).astype(I32), axis=1)
    block_e = jnp.minimum(block_e, N_EXPERTS - 1)
    block_e = jnp.where(blocks < n_valid, block_e, block_e[n_valid - 1])
    live_rows = jnp.clip(counts[block_e] - (blocks * tb - pad_start[block_e]), 0, tb)
    live_rows = jnp.where(blocks < n_valid, live_rows, 0).astype(I32)
    dest = _dest(pad_start, idx, rank)
    xs = _dispatch(hpk, dest, n_rows)
    ys = _experts(layer, block_e, n_valid.reshape(1), live_rows, xs, w_gu, b_gu, w_down, b_down)
    return _combine(_gather_rows(dest, ys, n_tok), x, gw, gt, final_g, final_norm)


def kernel(x, c, norm1_g, ada_w, ada_b, w_in, gmlp_norm_g, gmlp_ws, gmlp_bs, w_proj_a, pool_w, pool_scale,
           conv_w, w_proj_c, w_out, norm2_g, router_w, router_b, exp_w_gu, exp_b_gu, exp_w_down,
           exp_b_down, final_g):
    depth = ada_w.shape[0]
    bsz, _, d = x.shape
    mods = _ada(c, ada_w, ada_b)
    for l in range(depth):
        sh1, sc1, g1, sh2, sc2, g2 = [mods[l, :, i * d:(i + 1) * d].reshape(bsz, 1, d) for i in range(6)]
        x = _mixer(x, sh1, sc1, g1, norm1_g[l], w_in[l], gmlp_norm_g[l], gmlp_ws[l], gmlp_bs[l],
                   w_proj_a[l], pool_w[l], pool_scale[l], conv_w[l], w_proj_c[l], w_out[l])
        x = _moe(l, x, sh2, sc2, g2, norm2_g[l], router_w[l], router_b[l], exp_w_gu, exp_b_gu,
                 exp_w_down, exp_b_down, final_g, l == depth - 1)
    return x
```

```python
import functools

import jax
import jax.numpy as jnp
from jax import lax
from jax.experimental import pallas as pl
from jax.experimental.pallas import tpu as pltpu
from jax.experimental.pallas import tpu_sc as plsc

F32 = jnp.float32
BF16 = jnp.bfloat16
I32 = jnp.int32
U32 = jnp.uint32

RMS_EPS = 1e-5
GMLP_HEADS = 8
CHUNK = 128
POOL_WINDOWS = (2, 4, 8, 16)
POOL_CARRY = 16
CONV_K = 3
CONV_CARRY = 8
N_EXPERTS = 32
TOP_K = 4
SWIGLU_LIMIT = 7.0
SWIGLU_ALPHA = 1.702

MIX_ROWS = 512
ROUTE_ROWS = 512
DEST_COLS = 4096
EXPERT_ROWS = 512
COMBINE_ROWS = 512
SC_SUBCORES = 16
SC_WORKERS = 2 * SC_SUBCORES
SC_WINDOW = 128

VMEM_LIMIT = 56 * 1024 * 1024


def _rms(x, g):
    return x * lax.rsqrt(jnp.mean(x * x, axis=-1, keepdims=True) + RMS_EPS) * g


def _dot(a, b):
    return jnp.dot(a, b, preferred_element_type=F32)


def _params(n_axes, vmem=VMEM_LIMIT):
    return pltpu.CompilerParams(dimension_semantics=("arbitrary",) * n_axes, vmem_limit_bytes=vmem)


def _pack_bf16_pairs(x):
    half = x.shape[1] // 2
    bits = lax.bitcast_convert_type(x.astype(BF16).astype(F32), U32)
    return (bits[:, :half] >> 16) | bits[:, half:]


def _unpack_bf16_pairs(w):
    lo = lax.bitcast_convert_type(w << 16, F32)
    hi = lax.bitcast_convert_type(w & jnp.uint32(0xFFFF0000), F32)
    return jnp.concatenate([lo, hi], axis=1)


def _resident(shape):
    zeros = (0,) * len(shape)
    return pl.BlockSpec(shape, lambda *_: zeros, pipeline_mode=pl.Buffered(1))


def _ada_kernel(c_ref, w_ref, b_ref, o_ref):
    c = c_ref[...]
    cond = c * jax.nn.sigmoid(c)
    o_ref[0] = _dot(cond.astype(BF16), w_ref[0].astype(BF16)) + b_ref[0]


def _ada(c, ada_w, ada_b):
    depth, d, six_d = ada_w.shape
    bsz = c.shape[0]
    return pl.pallas_call(
        _ada_kernel,
        grid=(depth, six_d // d),
        in_specs=[
            pl.BlockSpec((bsz, d), lambda l, j: (0, 0)),
            pl.BlockSpec((1, d, d), lambda l, j: (l, 0, j)),
            pl.BlockSpec((1, 1, d), lambda l, j: (l, 0, j)),
        ],
        out_specs=pl.BlockSpec((1, bsz, d), lambda l, j: (l, 0, j)),
        out_shape=jax.ShapeDtypeStruct((depth, bsz, six_d), F32),
        compiler_params=_params(2),
        name="ada",
    )(c, ada_w, ada_b.reshape(depth, 1, six_d))


def _mixer_kernel(x_ref, sh_ref, sc_ref, gt_ref, n1_ref, win_ref, gng_ref, ws_ref, bst_ref, wpa_ref,
                  pw_ref, ps_ref, cw_ref, wpc_ref, wout_ref, o_ref, s_ref, pext_ref, zext_ref):
    tm, d = x_ref.shape[1], x_ref.shape[2]
    j = pl.program_id(1)

    @pl.when(j == 0)
    def _():
        pext_ref[0:POOL_CARRY, :] = jnp.zeros((POOL_CARRY, d), F32)
        zext_ref[0:CONV_CARRY, :] = jnp.zeros((CONV_CARRY, d), F32)

    x = x_ref[0]
    hb = (_rms(x, n1_ref[...]) * (1.0 + sc_ref[0]) + sh_ref[0]).astype(BF16)

    def proj(col):
        return _dot(hb, win_ref[:, col * d:(col + 1) * d])

    u = jax.nn.gelu(proj(0))
    vb = _rms(jax.nn.gelu(proj(1)), gng_ref[...]).astype(BF16)
    hd = d // GMLP_HEADS
    row = lax.broadcasted_iota(I32, (CHUNK, CHUNK), 0)
    col = lax.broadcasted_iota(I32, (CHUNK, CHUNK), 1)
    for h in range(GMLP_HEADS):
        wm = jnp.where(row >= col, ws_ref[h], 0.0).astype(BF16)
        bias = bst_ref[:, h:h + 1]
        for ci in range(tm // CHUNK):
            blk = vb[ci * CHUNK:(ci + 1) * CHUNK, h * hd:(h + 1) * hd]
            s_ref[ci * CHUNK:(ci + 1) * CHUNK, h * hd:(h + 1) * hd] = _dot(wm, blk) + bias
    ya = _dot((u * s_ref[...]).astype(BF16), wpa_ref[...])
    mix = jax.nn.sigmoid(proj(6)) * ya

    p = proj(2)
    pext_ref[POOL_CARRY:POOL_CARRY + tm, :] = p
    pos1 = (j * tm + 1 + lax.broadcasted_iota(I32, (tm, 1), 0))
    gd = d // len(POOL_WINDOWS)
    yb_parts = []
    for gi, w in enumerate(POOL_WINDOWS):
        cs = slice(gi * gd, (gi + 1) * gd)
        acc = p[:, cs]
        for k in range(1, w):
            acc = acc + pext_ref[POOL_CARRY - k:POOL_CARRY - k + tm, cs]
        cnt = jnp.minimum(pos1, w).astype(F32)
        dg = (acc / cnt - p[:, cs]).astype(BF16)
        yb_parts.append(_dot(dg, pw_ref[gi]))
    yb = jnp.concatenate(yb_parts, axis=1) * ps_ref[...]
    pext_ref[0:POOL_CARRY, :] = pext_ref[tm:tm + POOL_CARRY, :]
    mix = mix + jax.nn.sigmoid(proj(7)) * yb

    z = proj(5) * proj(3)
    zext_ref[CONV_CARRY:CONV_CARRY + tm, :] = z
    conv = cw_ref[CONV_K - 1:CONV_K, :] * z
    for k in range(CONV_K - 1):
        lag = CONV_K - 1 - k
        conv = conv + cw_ref[k:k + 1, :] * zext_ref[CONV_CARRY - lag:CONV_CARRY - lag + tm, :]
    zext_ref[0:CONV_CARRY, :] = zext_ref[tm:tm + CONV_CARRY, :]
    yc = _dot((proj(4) * conv).astype(BF16), wpc_ref[...])
    mix = mix + jax.nn.sigmoid(proj(8)) * yc

    o_ref[0] = x + gt_ref[0] * _dot(mix.astype(BF16), wout_ref[...])


def _mixer(x, sh, sc, gt, n1, w_in, gng, ws, bs, wpa, pool_w, pool_scale, conv_w, wpc, w_out):
    bsz, seq, d = x.shape
    tm = min(MIX_ROWS, seq)
    vec = pl.BlockSpec((1, 1, d), lambda b, j: (b, 0, 0))
    tile = pl.BlockSpec((1, tm, d), lambda b, j: (b, j, 0))
    return pl.pallas_call(
        _mixer_kernel,
        grid=(bsz, seq // tm),
        in_specs=[
            tile, vec, vec, vec,
            _resident((1, d)),
            _resident(w_in.shape),
            _resident((1, d)),
            _resident(ws.shape),
            _resident((CHUNK, GMLP_HEADS)),
            _resident(wpa.shape),
            _resident(pool_w.shape),
            _resident((1, d)),
            _resident(conv_w.shape),
            _resident(wpc.shape),
            _resident(w_out.shape),
        ],
        out_specs=tile,
        out_shape=jax.ShapeDtypeStruct(x.shape, F32),
        scratch_shapes=[
            pltpu.VMEM((tm, d), F32),
            pltpu.VMEM((POOL_CARRY + tm, d), F32),
            pltpu.VMEM((CONV_CARRY + tm, d), F32),
        ],
        compiler_params=_params(2),
        name="mixer",
    )(x, sh, sc, gt, n1.reshape(1, d), w_in.astype(BF16), gng.reshape(1, d), ws, bs.T,
      wpa.astype(BF16), pool_w.astype(BF16), pool_scale.reshape(1, d), conv_w, wpc.astype(BF16),
      w_out.astype(BF16))


def _router_kernel(x_ref, sh_ref, sc_ref, n2_ref, rwt_ref, rb_ref,
                   hpk_ref, idx_ref, gw_ref, rank_ref, cnt_ref, carry_ref):
    tm, d = x_ref.shape[1], x_ref.shape[2]
    first = jnp.logical_and(pl.program_id(0) == 0, pl.program_id(1) == 0)

    @pl.when(first)
    def _():
        carry_ref[...] = jnp.zeros(carry_ref.shape, F32)

    h = _rms(x_ref[0], n2_ref[...]) * (1.0 + sc_ref[0]) + sh_ref[0]
    hb = h.astype(BF16)
    hpk_ref[...] = _pack_bf16_pairs(h)

    logits = lax.dot_general(rwt_ref[...], hb, (((1,), (1,)), ((), ())),
                             preferred_element_type=F32) + rb_ref[...]
    iota_e = lax.broadcasted_iota(I32, logits.shape, 0)
    vals, idxs, sels = [], [], []
    rest = logits
    for _ in range(TOP_K):
        m = jnp.max(rest, axis=0, keepdims=True)
        ik = jnp.min(jnp.where(rest == m, iota_e, N_EXPERTS), axis=0, keepdims=True)
        sel = iota_e == ik
        rest = jnp.where(sel, -jnp.inf, rest)
        vals.append(m)
        idxs.append(ik)
        sels.append(sel)
    exps = [jnp.exp(v - vals[0]) for v in vals]
    denom = exps[0] + exps[1] + exps[2] + exps[3]

    chosen = jnp.logical_or(jnp.logical_or(sels[0], sels[1]), jnp.logical_or(sels[2], sels[3]))
    a = jnp.where(chosen, 1.0, 0.0)
    before = lax.broadcasted_iota(I32, (tm, tm), 0) < lax.broadcasted_iota(I32, (tm, tm), 1)
    prior = _dot(a.astype(BF16), jnp.where(before, 1.0, 0.0).astype(BF16)) + carry_ref[:, 0:1]
    for k in range(TOP_K):
        idx_ref[k:k + 1, :] = idxs[k]
        gw_ref[k:k + 1, :] = exps[k] / denom
        rank_ref[k:k + 1, :] = jnp.sum(jnp.where(sels[k], prior, 0.0), axis=0, keepdims=True).astype(I32)
    total = carry_ref[...] + jnp.sum(a, axis=1, keepdims=True)
    carry_ref[...] = total
    cnt_ref[...] = total.astype(I32)


def _router(x, sh, sc, n2, router_w, router_b):
    bsz, seq, d = x.shape
    n_tok = bsz * seq
    tm = min(ROUTE_ROWS, seq)
    nj = seq // tm
    vec = pl.BlockSpec((1, 1, d), lambda b, j: (b, 0, 0))
    per_tok = pl.BlockSpec((TOP_K, tm), lambda b, j: (0, b * nj + j))
    return pl.pallas_call(
        _router_kernel,
        grid=(bsz, nj),
        in_specs=[
            pl.BlockSpec((1, tm, d), lambda b, j: (b, j, 0)), vec, vec,
            _resident((1, d)), _resident((N_EXPERTS, d)), _resident((N_EXPERTS, 1)),
        ],
        out_specs=[
            pl.BlockSpec((tm, d // 2), lambda b, j: (b * nj + j, 0)),
            per_tok, per_tok, per_tok,
            pl.BlockSpec((N_EXPERTS, 128), lambda b, j: (0, 0)),
        ],
        out_shape=[
            jax.ShapeDtypeStruct((n_tok, d // 2), U32),
            jax.ShapeDtypeStruct((TOP_K, n_tok), I32),
            jax.ShapeDtypeStruct((TOP_K, n_tok), F32),
            jax.ShapeDtypeStruct((TOP_K, n_tok), I32),
            jax.ShapeDtypeStruct((N_EXPERTS, 128), I32),
        ],
        scratch_shapes=[pltpu.VMEM((N_EXPERTS, 128), F32)],
        compiler_params=_params(2),
        name="router",
    )(x, sh, sc, n2.reshape(1, d), router_w.T.astype(BF16), router_b.reshape(N_EXPERTS, 1))


def _dest_kernel(start_ref, idx_ref, rank_ref, o_ref):
    idx = idx_ref[...]
    base = jnp.zeros(idx.shape, I32)
    for e in range(N_EXPERTS):
        base = jnp.where(idx == e, start_ref[e], base)
    o_ref[...] = base + rank_ref[...]


def _dest(pad_start, idx, rank):
    n_tok = idx.shape[1]
    tc = min(DEST_COLS, n_tok)
    blk = pl.BlockSpec((TOP_K, tc), lambda i, s: (0, i))
    return pl.pallas_call(
        _dest_kernel,
        grid_spec=pltpu.PrefetchScalarGridSpec(
            num_scalar_prefetch=1, grid=(n_tok // tc,), in_specs=[blk, blk], out_specs=blk),
        out_shape=jax.ShapeDtypeStruct(idx.shape, I32),
        compiler_params=_params(1),
        name="dest",
    )(pad_start, idx, rank)


def _dispatch(hpk, dest, n_rows):
    n_tok, half = hpk.shape
    win = SC_WINDOW
    assert n_tok % (win * SC_WORKERS) == 0
    per_worker = n_tok // win // SC_WORKERS
    mesh = plsc.VectorSubcoreMesh(core_axis_name="c", subcore_axis_name="s")

    @pl.kernel(out_type=jax.ShapeDtypeStruct((n_rows, half), U32), mesh=mesh,
               scratch_types=[pltpu.VMEM((win, half), U32), pltpu.VMEM((TOP_K, win), I32)])
    def scatter_rows(hpk_hbm, dest_hbm, xs_hbm, xbuf, ibuf):
        worker = lax.axis_index("c") * SC_SUBCORES + lax.axis_index("s")

        @pl.loop(0, per_worker)
        def _(j):
            t0 = pl.multiple_of((worker * per_worker + j) * win, win)
            pltpu.sync_copy(hpk_hbm.at[pl.ds(t0, win)], xbuf)
            pltpu.sync_copy(dest_hbm.at[:, pl.ds(t0, win)], ibuf)
            for k in range(TOP_K):
                pltpu.sync_copy(xbuf, xs_hbm.at[ibuf.at[k]])

    return scatter_rows(hpk, dest)


def _expert_kernel(be_ref, nv_ref, live_ref, xs_ref, wgu_ref, bgu_ref, wd_ref, bd_ref, ys_ref, wgu_bf, wd_bf):
    i = pl.program_id(0)

    @pl.when(i >= nv_ref[0])
    def _():
        ys_ref[...] = jnp.zeros(ys_ref.shape, U32)

    @pl.when(jnp.logical_or(i == 0, be_ref[i] != be_ref[jnp.maximum(i - 1, 0)]))
    def _():
        wgu_bf[...] = wgu_ref[0, 0].astype(BF16)
        wd_bf[...] = wd_ref[0, 0].astype(BF16)

    @pl.when(i < nv_ref[0])
    def _():
        live = lax.broadcasted_iota(I32, (xs_ref.shape[0], 1), 0) < live_ref[i]
        xb = jnp.where(live, _unpack_bf16_pairs(xs_ref[...]), 0.0).astype(BF16)
        gu = _dot(xb, wgu_bf[...]) + bgu_ref[0, 0]
        ff = gu.shape[1] // 2
        gate = jnp.minimum(gu[:, :ff], SWIGLU_LIMIT)
        up = jnp.clip(gu[:, ff:], -SWIGLU_LIMIT, SWIGLU_LIMIT)
        glu = gate * jax.nn.sigmoid(SWIGLU_ALPHA * gate)
        ys_ref[...] = _pack_bf16_pairs(_dot(((up + 1.0) * glu).astype(BF16), wd_bf[...]) + bd_ref[0, 0])


def _experts(layer, block_e, n_valid, live_rows, xs, w_gu, b_gu, w_down, b_down):
    n_rows, half = xs.shape
    depth, _, d, two_f = w_gu.shape
    tb = EXPERT_ROWS

    def rows(i, be, nv, lv):
        return (jnp.minimum(i, nv[0] - 1), 0)

    def by_expert(i, be, nv, lv):
        return (layer, be[i], 0, 0)

    return pl.pallas_call(
        _expert_kernel,
        grid_spec=pltpu.PrefetchScalarGridSpec(
            num_scalar_prefetch=3,
            grid=(n_rows // tb,),
            in_specs=[
                pl.BlockSpec((tb, half), rows),
                pl.BlockSpec((1, 1, d, two_f), by_expert),
                pl.BlockSpec((1, 1, 1, two_f), by_expert),
                pl.BlockSpec((1, 1, two_f // 2, d), by_expert),
                pl.BlockSpec((1, 1, 1, d), by_expert),
            ],
            out_specs=pl.BlockSpec((tb, half), lambda i, be, nv, lv: (i, 0)),
            scratch_shapes=[pltpu.VMEM((d, two_f), BF16), pltpu.VMEM((two_f // 2, d), BF16)],
        ),
        out_shape=jax.ShapeDtypeStruct((n_rows, half), U32),
        compiler_params=_params(1),
        name="experts",
    )(block_e, n_valid, live_rows, xs, w_gu, b_gu.reshape(depth, N_EXPERTS, 1, two_f), w_down,
      b_down.reshape(depth, N_EXPERTS, 1, d))


def _gather_rows(dest, ys):
    n_tok = dest.shape[1]
    width = ys.shape[1]
    win = SC_WINDOW
    assert n_tok % (win * SC_WORKERS) == 0
    per_worker = n_tok // win // SC_WORKERS
    mesh = plsc.VectorSubcoreMesh(core_axis_name="c", subcore_axis_name="s")

    @pl.kernel(out_type=jax.ShapeDtypeStruct((TOP_K, n_tok, width), ys.dtype), mesh=mesh,
               scratch_types=[pltpu.VMEM((win, width), ys.dtype), pltpu.VMEM((TOP_K, win), I32)])
    def gather(ys_hbm, idx_hbm, g_hbm, buf, ibuf):
        worker = lax.axis_index("c") * SC_SUBCORES + lax.axis_index("s")

        @pl.loop(0, per_worker)
        def _(j):
            t0 = pl.multiple_of((worker * per_worker + j) * win, win)
            pltpu.sync_copy(idx_hbm.at[:, pl.ds(t0, win)], ibuf)
            for k in range(TOP_K):
                pltpu.sync_copy(ys_hbm.at[ibuf.at[k]], buf)
                pltpu.sync_copy(buf, g_hbm.at[k, pl.ds(t0, win)])

    return gather(ys, dest)


def _combine_kernel(g_ref, x_ref, gw_ref, gt_ref, fg_ref, o_ref, *, final_norm):
    gw = gw_ref[...]
    acc = None
    for k in range(TOP_K):
        term = gw[:, k:k + 1] * _unpack_bf16_pairs(g_ref[k])
        acc = term if acc is None else acc + term
    y = x_ref[...] + gt_ref[0] * acc
    o_ref[...] = _rms(y, fg_ref[...]) if final_norm else y


def _combine(g, x, gw, gt, final_g, final_norm):
    bsz, seq, d = x.shape
    n_tok = bsz * seq
    tm = min(COMBINE_ROWS, seq)
    per_seq = seq // tm
    out = pl.pallas_call(
        functools.partial(_combine_kernel, final_norm=final_norm),
        grid=(n_tok // tm,),
        in_specs=[
            pl.BlockSpec((TOP_K, tm, d // 2), lambda i: (0, i, 0)),
            pl.BlockSpec((tm, d), lambda i: (i, 0)),
            pl.BlockSpec((tm, TOP_K), lambda i: (i, 0)),
            pl.BlockSpec((1, 1, d), lambda i: (i // per_seq, 0, 0)),
            pl.BlockSpec((1, d), lambda i: (0, 0)),
        ],
        out_specs=pl.BlockSpec((tm, d), lambda i: (i, 0)),
        out_shape=jax.ShapeDtypeStruct((n_tok, d), F32),
        compiler_params=_params(1),
        name="combine",
    )(g, x.reshape(n_tok, d), gw.T, gt, final_g.reshape(1, d))
    return out.reshape(bsz, seq, d)


def _moe(layer, x, sh, sc, gt, n2, router_w, router_b, w_gu, b_gu, w_down, b_down, final_g, final_norm):
    bsz, seq, _ = x.shape
    n_tok = bsz * seq
    tb = EXPERT_ROWS
    n_rows = -(-(n_tok * TOP_K + N_EXPERTS * (tb - 1)) // tb) * tb
    hpk, idx, gw, rank, cnt = _router(x, sh, sc, n2, router_w, router_b)
    counts = cnt[:, 0]
    padded = (counts + tb - 1) // tb * tb
    pad_end = jnp.cumsum(padded)
    pad_start = pad_end - padded
    n_valid = pad_end[-1] // tb
    blocks = jnp.arange(n_rows // tb, dtype=I32)
    block_e = jnp.sum((pad_end[None, :] <= (blocks * tb)[:, None]).astype(I32), axis=1)
    block_e = jnp.minimum(block_e, N_EXPERTS - 1)
    block_e = jnp.where(blocks < n_valid, block_e, block_e[n_valid - 1])
    live_rows = jnp.clip(counts[block_e] - (blocks * tb - pad_start[block_e]), 0, tb)
    live_rows = jnp.where(blocks < n_valid, live_rows, 0).astype(I32)
    dest = _dest(pad_start, idx, rank)
    xs = _dispatch(hpk, dest, n_rows)
    ys = _experts(layer, block_e, n_valid.reshape(1), live_rows, xs, w_gu, b_gu, w_down, b_down)
    return _combine(_gather_rows(dest, ys), x, gw, gt, final_g, final_norm)


def kernel(x, c, norm1_g, ada_w, ada_b, w_in, gmlp_norm_g, gmlp_ws, gmlp_bs, w_proj_a, pool_w, pool_scale,
           conv_w, w_proj_c, w_out, norm2_g, router_w, router_b, exp_w_gu, exp_b_gu, exp_w_down,
           exp_b_down, final_g):
    depth = ada_w.shape[0]
    bsz, _, d = x.shape
    mods = _ada(c, ada_w, ada_b)
    for l in range(depth):
        sh1, sc1, g1, sh2, sc2, g2 = [mods[l, :, i * d:(i + 1) * d].reshape(bsz, 1, d) for i in range(6)]
        x = _mixer(x, sh1, sc1, g1, norm1_g[l], w_in[l], gmlp_norm_g[l], gmlp_ws[l], gmlp_bs[l],
                   w_proj_a[l], pool_w[l], pool_scale[l], conv_w[l], w_proj_c[l], w_out[l])
        x = _moe(l, x, sh2, sc2, g2, norm2_g[l], router_w[l], router_b[l], exp_w_gu, exp_b_gu,
                 exp_w_down, exp_b_down, final_g, l == depth - 1)
    return x
```

```python
import functools

import jax
import jax.numpy as jnp
from jax import lax
from jax.experimental import pallas as pl
from jax.experimental.pallas import tpu as pltpu
from jax.experimental.pallas import tpu_sc as plsc

F32 = jnp.float32
BF16 = jnp.bfloat16
I32 = jnp.int32
U32 = jnp.uint32

RMS_EPS = 1e-5
GMLP_HEADS = 8
CHUNK = 128
POOL_WINDOWS = (2, 4, 8, 16)
POOL_CARRY = 16
CONV_K = 3
CONV_CARRY = 8
N_EXPERTS = 32
TOP_K = 4
SWIGLU_LIMIT = 7.0
SWIGLU_ALPHA = 1.702

MIX_ROWS = 512
DEST_COLS = 4096
EXPERT_ROWS = 512
COMBINE_ROWS = 512
SC_SUBCORES = 16
SC_WORKERS = 2 * SC_SUBCORES
SC_WINDOW = 128

VMEM_LIMIT = 56 * 1024 * 1024


def _rms(x, g):
    return x * lax.rsqrt(jnp.mean(x * x, axis=-1, keepdims=True) + RMS_EPS) * g


def _dot(a, b):
    return jnp.dot(a, b, preferred_element_type=F32)


def _params(n_axes, vmem=VMEM_LIMIT):
    return pltpu.CompilerParams(dimension_semantics=("arbitrary",) * n_axes, vmem_limit_bytes=vmem)


def _pack_bf16_pairs(x):
    half = x.shape[1] // 2
    bits = lax.bitcast_convert_type(x.astype(BF16).astype(F32), U32)
    return (bits[:, :half] >> 16) | bits[:, half:]


def _unpack_bf16_pairs(w):
    lo = lax.bitcast_convert_type(w << 16, F32)
    hi = lax.bitcast_convert_type(w & jnp.uint32(0xFFFF0000), F32)
    return jnp.concatenate([lo, hi], axis=1)


def _resident(shape):
    zeros = (0,) * len(shape)
    return pl.BlockSpec(shape, lambda *_: zeros, pipeline_mode=pl.Buffered(1))


def _ada_kernel(c_ref, w_ref, b_ref, o_ref):
    c = c_ref[...]
    cond = c * jax.nn.sigmoid(c)
    o_ref[0] = _dot(cond.astype(BF16), w_ref[0].astype(BF16)) + b_ref[0]


def _ada(c, ada_w, ada_b):
    depth, d, six_d = ada_w.shape
    bsz = c.shape[0]
    return pl.pallas_call(
        _ada_kernel,
        grid=(depth, six_d // d),
        in_specs=[
            pl.BlockSpec((bsz, d), lambda l, j: (0, 0)),
            pl.BlockSpec((1, d, d), lambda l, j: (l, 0, j)),
            pl.BlockSpec((1, 1, d), lambda l, j: (l, 0, j)),
        ],
        out_specs=pl.BlockSpec((1, bsz, d), lambda l, j: (l, 0, j)),
        out_shape=jax.ShapeDtypeStruct((depth, bsz, six_d), F32),
        compiler_params=_params(2),
        name="ada",
    )(c, ada_w, ada_b.reshape(depth, 1, six_d))


def _mixer_kernel(x_ref, sh_ref, sc_ref, gt_ref, sh2_ref, sc2_ref, n1_ref, win_ref, gng_ref, ws_ref, bst_ref,
                  wpa_ref, pw_ref, ps_ref, cw_ref, wpc_ref, wout_ref, n2_ref, rwt_ref, rb_ref,
                  o_ref, hpk_ref, idx_ref, gw_ref, rank_ref, cnt_ref,
                  s_ref, pext_ref, zext_ref, carry_ref):
    tm, d = x_ref.shape[1], x_ref.shape[2]
    j = pl.program_id(1)

    @pl.when(j == 0)
    def _():
        pext_ref[0:POOL_CARRY, :] = jnp.zeros((POOL_CARRY, d), F32)
        zext_ref[0:CONV_CARRY, :] = jnp.zeros((CONV_CARRY, d), F32)

    x = x_ref[0]
    hb = (_rms(x, n1_ref[...]) * (1.0 + sc_ref[0]) + sh_ref[0]).astype(BF16)

    def proj(col):
        return _dot(hb, win_ref[:, col * d:(col + 1) * d])

    u = jax.nn.gelu(proj(0))
    vb = _rms(jax.nn.gelu(proj(1)), gng_ref[...]).astype(BF16)
    hd = d // GMLP_HEADS
    row = lax.broadcasted_iota(I32, (CHUNK, CHUNK), 0)
    col = lax.broadcasted_iota(I32, (CHUNK, CHUNK), 1)
    for h in range(GMLP_HEADS):
        wm = jnp.where(row >= col, ws_ref[h], 0.0).astype(BF16)
        bias = bst_ref[:, h:h + 1]
        for ci in range(tm // CHUNK):
            blk = vb[ci * CHUNK:(ci + 1) * CHUNK, h * hd:(h + 1) * hd]
            s_ref[ci * CHUNK:(ci + 1) * CHUNK, h * hd:(h + 1) * hd] = _dot(wm, blk) + bias
    ya = _dot((u * s_ref[...]).astype(BF16), wpa_ref[...])
    mix = jax.nn.sigmoid(proj(6)) * ya

    p = proj(2)
    pext_ref[POOL_CARRY:POOL_CARRY + tm, :] = p
    pos1 = (j * tm + 1 + lax.broadcasted_iota(I32, (tm, 1), 0))
    gd = d // len(POOL_WINDOWS)
    yb_parts = []
    for gi, w in enumerate(POOL_WINDOWS):
        cs = slice(gi * gd, (gi + 1) * gd)
        acc = p[:, cs]
        for k in range(1, w):
            acc = acc + pext_ref[POOL_CARRY - k:POOL_CARRY - k + tm, cs]
        cnt = jnp.minimum(pos1, w).astype(F32)
        dg = (acc / cnt - p[:, cs]).astype(BF16)
        yb_parts.append(_dot(dg, pw_ref[gi]))
    yb = jnp.concatenate(yb_parts, axis=1) * ps_ref[...]
    pext_ref[0:POOL_CARRY, :] = pext_ref[tm:tm + POOL_CARRY, :]
    mix = mix + jax.nn.sigmoid(proj(7)) * yb

    z = proj(5) * proj(3)
    zext_ref[CONV_CARRY:CONV_CARRY + tm, :] = z
    conv = cw_ref[CONV_K - 1:CONV_K, :] * z
    for k in range(CONV_K - 1):
        lag = CONV_K - 1 - k
        conv = conv + cw_ref[k:k + 1, :] * zext_ref[CONV_CARRY - lag:CONV_CARRY - lag + tm, :]
    zext_ref[0:CONV_CARRY, :] = zext_ref[tm:tm + CONV_CARRY, :]
    yc = _dot((proj(4) * conv).astype(BF16), wpc_ref[...])
    mix = mix + jax.nn.sigmoid(proj(8)) * yc

    x_new = x + gt_ref[0] * _dot(mix.astype(BF16), wout_ref[...])
    o_ref[0] = x_new
    first = jnp.logical_and(pl.program_id(0) == 0, j == 0)
    _route_tile(x_new, sh2_ref[0], sc2_ref[0], n2_ref[...], rwt_ref[...], rb_ref[...], first,
                hpk_ref, idx_ref, gw_ref, rank_ref, cnt_ref, carry_ref)


def _mixer(x, sh, sc, gt, sh2, sc2, n1, w_in, gng, ws, bs, wpa, pool_w, pool_scale, conv_w, wpc, w_out,
           n2, router_w, router_b):
    bsz, seq, d = x.shape
    n_tok = bsz * seq
    tm = min(MIX_ROWS, seq)
    nj = seq // tm
    vec = pl.BlockSpec((1, 1, d), lambda b, j: (b, 0, 0))
    tile = pl.BlockSpec((1, tm, d), lambda b, j: (b, j, 0))
    per_tok = pl.BlockSpec((TOP_K, tm), lambda b, j: (0, b * nj + j))
    return pl.pallas_call(
        _mixer_kernel,
        grid=(bsz, nj),
        in_specs=[
            tile, vec, vec, vec, vec, vec,
            _resident((1, d)),
            _resident(w_in.shape),
            _resident((1, d)),
            _resident(ws.shape),
            _resident((CHUNK, GMLP_HEADS)),
            _resident(wpa.shape),
            _resident(pool_w.shape),
            _resident((1, d)),
            _resident(conv_w.shape),
            _resident(wpc.shape),
            _resident(w_out.shape),
            _resident((1, d)), _resident((N_EXPERTS, d)), _resident((N_EXPERTS, 1)),
        ],
        out_specs=[
            tile,
            pl.BlockSpec((tm, d // 2), lambda b, j: (b * nj + j, 0)),
            per_tok, per_tok, per_tok,
            pl.BlockSpec((N_EXPERTS, 128), lambda b, j: (0, 0)),
        ],
        out_shape=[
            jax.ShapeDtypeStruct(x.shape, F32),
            jax.ShapeDtypeStruct((n_tok, d // 2), U32),
            jax.ShapeDtypeStruct((TOP_K, n_tok), I32),
            jax.ShapeDtypeStruct((TOP_K, n_tok), F32),
            jax.ShapeDtypeStruct((TOP_K, n_tok), I32),
            jax.ShapeDtypeStruct((N_EXPERTS, 128), I32),
        ],
        scratch_shapes=[
            pltpu.VMEM((tm, d), F32),
            pltpu.VMEM((POOL_CARRY + tm, d), F32),
            pltpu.VMEM((CONV_CARRY + tm, d), F32),
            pltpu.VMEM((N_EXPERTS, 128), F32),
        ],
        compiler_params=_params(2),
        name="mixer",
    )(x, sh, sc, gt, sh2, sc2, n1.reshape(1, d), w_in.astype(BF16), gng.reshape(1, d), ws, bs.T,
      wpa.astype(BF16), pool_w.astype(BF16), pool_scale.reshape(1, d), conv_w, wpc.astype(BF16),
      w_out.astype(BF16), n2.reshape(1, d), router_w.T.astype(BF16), router_b.reshape(N_EXPERTS, 1))


def _route_tile(x, sh, sc, n2, rwt, rb, first, hpk_ref, idx_ref, gw_ref, rank_ref, cnt_ref, carry_ref):
    tm = x.shape[0]

    @pl.when(first)
    def _():
        carry_ref[...] = jnp.zeros(carry_ref.shape, F32)

    h = _rms(x, n2) * (1.0 + sc) + sh
    hb = h.astype(BF16)
    hpk_ref[...] = _pack_bf16_pairs(h)

    logits = lax.dot_general(rwt, hb, (((1,), (1,)), ((), ())), preferred_element_type=F32) + rb
    iota_e = lax.broadcasted_iota(I32, logits.shape, 0)
    vals, idxs, sels = [], [], []
    rest = logits
    for _ in range(TOP_K):
        m = jnp.max(rest, axis=0, keepdims=True)
        ik = jnp.min(jnp.where(rest == m, iota_e, N_EXPERTS), axis=0, keepdims=True)
        sel = iota_e == ik
        rest = jnp.where(sel, -jnp.inf, rest)
        vals.append(m)
        idxs.append(ik)
        sels.append(sel)
    exps = [jnp.exp(v - vals[0]) for v in vals]
    denom = exps[0] + exps[1] + exps[2] + exps[3]

    chosen = jnp.logical_or(jnp.logical_or(sels[0], sels[1]), jnp.logical_or(sels[2], sels[3]))
    a = jnp.where(chosen, 1.0, 0.0)
    before = lax.broadcasted_iota(I32, (tm, tm), 0) < lax.broadcasted_iota(I32, (tm, tm), 1)
    prior = _dot(a.astype(BF16), jnp.where(before, 1.0, 0.0).astype(BF16)) + carry_ref[:, 0:1]
    for k in range(TOP_K):
        idx_ref[k:k + 1, :] = idxs[k]
        gw_ref[k:k + 1, :] = exps[k] / denom
        rank_ref[k:k + 1, :] = jnp.sum(jnp.where(sels[k], prior, 0.0), axis=0, keepdims=True).astype(I32)
    total = carry_ref[...] + jnp.sum(a, axis=1, keepdims=True)
    carry_ref[...] = total
    cnt_ref[...] = total.astype(I32)


def _dest_kernel(start_ref, idx_ref, rank_ref, o_ref):
    idx = idx_ref[...]
    base = jnp.zeros(idx.shape, I32)
    for e in range(N_EXPERTS):
        base = jnp.where(idx == e, start_ref[e], base)
    o_ref[...] = base + rank_ref[...]


def _dest(pad_start, idx, rank):
    n_tok = idx.shape[1]
    tc = min(DEST_COLS, n_tok)
    blk = pl.BlockSpec((TOP_K, tc), lambda i, s: (0, i))
    return pl.pallas_call(
        _dest_kernel,
        grid_spec=pltpu.PrefetchScalarGridSpec(
            num_scalar_prefetch=1, grid=(n_tok // tc,), in_specs=[blk, blk], out_specs=blk),
        out_shape=jax.ShapeDtypeStruct(idx.shape, I32),
        compiler_params=_params(1),
        name="dest",
    )(pad_start, idx, rank)


def _dispatch(hpk, dest, n_rows):
    n_tok, half = hpk.shape
    win = SC_WINDOW
    assert n_tok % (win * SC_WORKERS) == 0
    per_worker = n_tok // win // SC_WORKERS
    mesh = plsc.VectorSubcoreMesh(core_axis_name="c", subcore_axis_name="s")

    @pl.kernel(out_type=jax.ShapeDtypeStruct((n_rows, half), U32), mesh=mesh,
               scratch_types=[pltpu.VMEM((win, half), U32), pltpu.VMEM((TOP_K, win), I32)])
    def scatter_rows(hpk_hbm, dest_hbm, xs_hbm, xbuf, ibuf):
        worker = lax.axis_index("c") * SC_SUBCORES + lax.axis_index("s")

        @pl.loop(0, per_worker)
        def _(j):
            t0 = pl.multiple_of((worker * per_worker + j) * win, win)
            pltpu.sync_copy(hpk_hbm.at[pl.ds(t0, win)], xbuf)
            pltpu.sync_copy(dest_hbm.at[:, pl.ds(t0, win)], ibuf)
            for k in range(TOP_K):
                pltpu.sync_copy(xbuf, xs_hbm.at[ibuf.at[k]])

    return scatter_rows(hpk, dest)


def _expert_kernel(be_ref, nv_ref, live_ref, xs_ref, wgu_ref, bgu_ref, wd_ref, bd_ref, ys_ref, wgu_bf, wd_bf):
    i = pl.program_id(0)

    @pl.when(i >= nv_ref[0])
    def _():
        ys_ref[...] = jnp.zeros(ys_ref.shape, U32)

    @pl.when(jnp.logical_or(i == 0, be_ref[i] != be_ref[jnp.maximum(i - 1, 0)]))
    def _():
        wgu_bf[...] = wgu_ref[0, 0].astype(BF16)
        wd_bf[...] = wd_ref[0, 0].astype(BF16)

    @pl.when(i < nv_ref[0])
    def _():
        live = lax.broadcasted_iota(I32, (xs_ref.shape[0], 1), 0) < live_ref[i]
        xb = jnp.where(live, _unpack_bf16_pairs(xs_ref[...]), 0.0).astype(BF16)
        gu = _dot(xb, wgu_bf[...]) + bgu_ref[0, 0]
        ff = gu.shape[1] // 2
        gate = jnp.minimum(gu[:, :ff], SWIGLU_LIMIT)
        up = jnp.clip(gu[:, ff:], -SWIGLU_LIMIT, SWIGLU_LIMIT)
        glu = gate * jax.nn.sigmoid(SWIGLU_ALPHA * gate)
        ys_ref[...] = _pack_bf16_pairs(_dot(((up + 1.0) * glu).astype(BF16), wd_bf[...]) + bd_ref[0, 0])


def _experts(layer, block_e, n_valid, live_rows, xs, w_gu, b_gu, w_down, b_down):
    n_rows, half = xs.shape
    depth, _, d, two_f = w_gu.shape
    tb = EXPERT_ROWS

    def rows(i, be, nv, lv):
        return (jnp.minimum(i, nv[0] - 1), 0)

    def by_expert(i, be, nv, lv):
        return (layer, be[i], 0, 0)

    return pl.pallas_call(
        _expert_kernel,
        grid_spec=pltpu.PrefetchScalarGridSpec(
            num_scalar_prefetch=3,
            grid=(n_rows // tb,),
            in_specs=[
                pl.BlockSpec((tb, half), rows),
                pl.BlockSpec((1, 1, d, two_f), by_expert),
                pl.BlockSpec((1, 1, 1, two_f), by_expert),
                pl.BlockSpec((1, 1, two_f // 2, d), by_expert),
                pl.BlockSpec((1, 1, 1, d), by_expert),
            ],
            out_specs=pl.BlockSpec((tb, half), lambda i, be, nv, lv: (i, 0)),
            scratch_shapes=[pltpu.VMEM((d, two_f), BF16), pltpu.VMEM((two_f // 2, d), BF16)],
        ),
        out_shape=jax.ShapeDtypeStruct((n_rows, half), U32),
        compiler_params=_params(1),
        name="experts",
    )(block_e, n_valid, live_rows, xs, w_gu, b_gu.reshape(depth, N_EXPERTS, 1, two_f), w_down,
      b_down.reshape(depth, N_EXPERTS, 1, d))


def _gather_rows(dest, ys):
    n_tok = dest.shape[1]
    width = ys.shape[1]
    win = SC_WINDOW
    assert n_tok % (win * SC_WORKERS) == 0
    per_worker = n_tok // win // SC_WORKERS
    mesh = plsc.VectorSubcoreMesh(core_axis_name="c", subcore_axis_name="s")

    @pl.kernel(out_type=jax.ShapeDtypeStruct((TOP_K, n_tok, width), ys.dtype), mesh=mesh,
               scratch_types=[pltpu.VMEM((win, width), ys.dtype), pltpu.VMEM((TOP_K, win), I32)])
    def gather(ys_hbm, idx_hbm, g_hbm, buf, ibuf):
        worker = lax.axis_index("c") * SC_SUBCORES + lax.axis_index("s")

        @pl.loop(0, per_worker)
        def _(j):
            t0 = pl.multiple_of((worker * per_worker + j) * win, win)
            pltpu.sync_copy(idx_hbm.at[:, pl.ds(t0, win)], ibuf)
            for k in range(TOP_K):
                pltpu.sync_copy(ys_hbm.at[ibuf.at[k]], buf)
                pltpu.sync_copy(buf, g_hbm.at[k, pl.ds(t0, win)])

    return gather(ys, dest)


def _combine_kernel(g_ref, x_ref, gw_ref, gt_ref, fg_ref, o_ref, *, final_norm):
    gw = gw_ref[...]
    acc = None
    for k in range(TOP_K):
        term = gw[:, k:k + 1] * _unpack_bf16_pairs(g_ref[k])
        acc = term if acc is None else acc + term
    y = x_ref[...] + gt_ref[0] * acc
    o_ref[...] = _rms(y, fg_ref[...]) if final_norm else y


def _combine(g, x, gw, gt, final_g, final_norm):
    bsz, seq, d = x.shape
    n_tok = bsz * seq
    tm = min(COMBINE_ROWS, seq)
    per_seq = seq // tm
    out = pl.pallas_call(
        functools.partial(_combine_kernel, final_norm=final_norm),
        grid=(n_tok // tm,),
        in_specs=[
            pl.BlockSpec((TOP_K, tm, d // 2), lambda i: (0, i, 0)),
            pl.BlockSpec((tm, d), lambda i: (i, 0)),
            pl.BlockSpec((tm, TOP_K), lambda i: (i, 0)),
            pl.BlockSpec((1, 1, d), lambda i: (i // per_seq, 0, 0)),
            pl.BlockSpec((1, d), lambda i: (0, 0)),
        ],
        out_specs=pl.BlockSpec((tm, d), lambda i: (i, 0)),
        out_shape=jax.ShapeDtypeStruct((n_tok, d), F32),
        compiler_params=_params(1),
        name="combine",
    )(g, x.reshape(n_tok, d), gw.T, gt, final_g.reshape(1, d))
    return out.reshape(bsz, seq, d)


def _moe(layer, x, routed, gt, w_gu, b_gu, w_down, b_down, final_g, final_norm):
    bsz, seq, _ = x.shape
    n_tok = bsz * seq
    tb = EXPERT_ROWS
    n_rows = -(-(n_tok * TOP_K + N_EXPERTS * (tb - 1)) // tb) * tb
    hpk, idx, gw, rank, cnt = routed
    counts = cnt[:, 0]
    padded = (counts + tb - 1) // tb * tb
    pad_end = jnp.cumsum(padded)
    pad_start = pad_end - padded
    n_valid = pad_end[-1] // tb
    blocks = jnp.arange(n_rows // tb, dtype=I32)
    block_e = jnp.sum((pad_end[None, :] <= (blocks * tb)[:, None]).astype(I32), axis=1)
    block_e = jnp.minimum(block_e, N_EXPERTS - 1)
    block_e = jnp.where(blocks < n_valid, block_e, block_e[n_valid - 1])
    live_rows = jnp.clip(counts[block_e] - (blocks * tb - pad_start[block_e]), 0, tb)
    live_rows = jnp.where(blocks < n_valid, live_rows, 0).astype(I32)
    dest = _dest(pad_start, idx, rank)
    xs = _dispatch(hpk, dest, n_rows)
    ys = _experts(layer, block_e, n_valid.reshape(1), live_rows, xs, w_gu, b_gu, w_down, b_down)
    return _combine(_gather_rows(dest, ys), x, gw, gt, final_g, final_norm)


def kernel(x, c, norm1_g, ada_w, ada_b, w_in, gmlp_norm_g, gmlp_ws, gmlp_bs, w_proj_a, pool_w, pool_scale,
           conv_w, w_proj_c, w_out, norm2_g, router_w, router_b, exp_w_gu, exp_b_gu, exp_w_down,
           exp_b_down, final_g):
    depth = ada_w.shape[0]
    bsz, _, d = x.shape
    mods = _ada(c, ada_w, ada_b)
    for l in range(depth):
        sh1, sc1, g1, sh2, sc2, g2 = [mods[l, :, i * d:(i + 1) * d].reshape(bsz, 1, d) for i in range(6)]
        x, *routed = _mixer(x, sh1, sc1, g1, sh2, sc2, norm1_g[l], w_in[l], gmlp_norm_g[l], gmlp_ws[l],
                            gmlp_bs[l], w_proj_a[l], pool_w[l], pool_scale[l], conv_w[l], w_proj_c[l],
                            w_out[l], norm2_g[l], router_w[l], router_b[l])
        x = _moe(l, x, routed, g2, exp_w_gu, exp_b_gu, exp_w_down, exp_b_down, final_g, l == depth - 1)
    return x
```

```python
import functools

import jax
import jax.numpy as jnp
from jax import lax
from jax.experimental import pallas as pl
from jax.experimental.pallas import tpu as pltpu
from jax.experimental.pallas import tpu_sc as plsc

F32 = jnp.float32
BF16 = jnp.bfloat16
I32 = jnp.int32
U32 = jnp.uint32

RMS_EPS = 1e-5
GMLP_HEADS = 8
CHUNK = 128
POOL_WINDOWS = (2, 4, 8, 16)
POOL_CARRY = 16
CONV_K = 3
CONV_CARRY = 8
N_EXPERTS = 32
TOP_K = 4
SWIGLU_LIMIT = 7.0
SWIGLU_ALPHA = 1.702

MIX_ROWS = 512
MIX_ROWS_PENDING = 256
DEST_COLS = 4096
EXPERT_ROWS = 512
FINAL_ROWS = 512
SC_SUBCORES = 16
SC_WORKERS = 2 * SC_SUBCORES
SC_WINDOW = 128

VMEM_LIMIT = 56 * 1024 * 1024


def _rms(x, g):
    return x * lax.rsqrt(jnp.mean(x * x, axis=-1, keepdims=True) + RMS_EPS) * g


def _dot(a, b):
    return jnp.dot(a, b, preferred_element_type=F32)


def _params(n_axes, vmem=VMEM_LIMIT):
    return pltpu.CompilerParams(dimension_semantics=("arbitrary",) * n_axes, vmem_limit_bytes=vmem)


def _pack_bf16_pairs(x):
    half = x.shape[1] // 2
    bits = lax.bitcast_convert_type(x.astype(BF16).astype(F32), U32)
    return (bits[:, :half] >> 16) | bits[:, half:]


def _unpack_bf16_pairs(w):
    lo = lax.bitcast_convert_type(w << 16, F32)
    hi = lax.bitcast_convert_type(w & jnp.uint32(0xFFFF0000), F32)
    return jnp.concatenate([lo, hi], axis=1)


def _resident(shape):
    zeros = (0,) * len(shape)
    return pl.BlockSpec(shape, lambda *_: zeros, pipeline_mode=pl.Buffered(1))


def _ada_kernel(c_ref, w_ref, b_ref, o_ref):
    c = c_ref[...]
    cond = c * jax.nn.sigmoid(c)
    o_ref[0] = _dot(cond.astype(BF16), w_ref[0].astype(BF16)) + b_ref[0]


def _ada(c, ada_w, ada_b):
    depth, d, six_d = ada_w.shape
    bsz = c.shape[0]
    return pl.pallas_call(
        _ada_kernel,
        grid=(depth, six_d // d),
        in_specs=[
            pl.BlockSpec((bsz, d), lambda l, j: (0, 0)),
            pl.BlockSpec((1, d, d), lambda l, j: (l, 0, j)),
            pl.BlockSpec((1, 1, d), lambda l, j: (l, 0, j)),
        ],
        out_specs=pl.BlockSpec((1, bsz, d), lambda l, j: (l, 0, j)),
        out_shape=jax.ShapeDtypeStruct((depth, bsz, six_d), F32),
        compiler_params=_params(2),
        name="ada",
    )(c, ada_w, ada_b.reshape(depth, 1, six_d))


def _weighted_rows(g_ref, gw):
    acc = gw[:, 0:1] * _unpack_bf16_pairs(g_ref[0])
    for k in range(1, TOP_K):
        acc = acc + gw[:, k:k + 1] * _unpack_bf16_pairs(g_ref[k])
    return acc


def _mixer_kernel(*refs, pending):
    if pending:
        g_ref, gwp_ref, gtp_ref, *refs = refs
    (x_ref, sh_ref, sc_ref, gt_ref, sh2_ref, sc2_ref, n1_ref, win_ref, gng_ref, ws_ref, bst_ref,
     wpa_ref, pw_ref, ps_ref, cw_ref, wpc_ref, wout_ref, n2_ref, rwt_ref, rb_ref,
     o_ref, hpk_ref, idx_ref, gw_ref, rank_ref, cnt_ref,
     s_ref, pext_ref, zext_ref, carry_ref) = refs
    tm, d = x_ref.shape[1], x_ref.shape[2]
    j = pl.program_id(1)

    @pl.when(j == 0)
    def _():
        pext_ref[0:POOL_CARRY, :] = jnp.zeros((POOL_CARRY, d), F32)
        zext_ref[0:CONV_CARRY, :] = jnp.zeros((CONV_CARRY, d), F32)

    x = x_ref[0]
    if pending:
        x = x + gtp_ref[0] * _weighted_rows(g_ref, gwp_ref[...])
    hb = (_rms(x, n1_ref[...]) * (1.0 + sc_ref[0]) + sh_ref[0]).astype(BF16)

    def proj(col):
        return _dot(hb, win_ref[:, col * d:(col + 1) * d])

    u = jax.nn.gelu(proj(0))
    vb = _rms(jax.nn.gelu(proj(1)), gng_ref[...]).astype(BF16)
    hd = d // GMLP_HEADS
    row = lax.broadcasted_iota(I32, (CHUNK, CHUNK), 0)
    col = lax.broadcasted_iota(I32, (CHUNK, CHUNK), 1)
    for h in range(GMLP_HEADS):
        wm = jnp.where(row >= col, ws_ref[h], 0.0).astype(BF16)
        bias = bst_ref[:, h:h + 1]
        for ci in range(tm // CHUNK):
            blk = vb[ci * CHUNK:(ci + 1) * CHUNK, h * hd:(h + 1) * hd]
            s_ref[ci * CHUNK:(ci + 1) * CHUNK, h * hd:(h + 1) * hd] = _dot(wm, blk) + bias
    ya = _dot((u * s_ref[...]).astype(BF16), wpa_ref[...])
    mix = jax.nn.sigmoid(proj(6)) * ya

    p = proj(2)
    pext_ref[POOL_CARRY:POOL_CARRY + tm, :] = p
    pos1 = (j * tm + 1 + lax.broadcasted_iota(I32, (tm, 1), 0))
    gd = d // len(POOL_WINDOWS)
    yb_parts = []
    for gi, w in enumerate(POOL_WINDOWS):
        cs = slice(gi * gd, (gi + 1) * gd)
        acc = p[:, cs]
        for k in range(1, w):
            acc = acc + pext_ref[POOL_CARRY - k:POOL_CARRY - k + tm, cs]
        cnt = jnp.minimum(pos1, w).astype(F32)
        dg = (acc / cnt - p[:, cs]).astype(BF16)
        yb_parts.append(_dot(dg, pw_ref[gi]))
    yb = jnp.concatenate(yb_parts, axis=1) * ps_ref[...]
    pext_ref[0:POOL_CARRY, :] = pext_ref[tm:tm + POOL_CARRY, :]
    mix = mix + jax.nn.sigmoid(proj(7)) * yb

    z = proj(5) * proj(3)
    zext_ref[CONV_CARRY:CONV_CARRY + tm, :] = z
    conv = cw_ref[CONV_K - 1:CONV_K, :] * z
    for k in range(CONV_K - 1):
        lag = CONV_K - 1 - k
        conv = conv + cw_ref[k:k + 1, :] * zext_ref[CONV_CARRY - lag:CONV_CARRY - lag + tm, :]
    zext_ref[0:CONV_CARRY, :] = zext_ref[tm:tm + CONV_CARRY, :]
    yc = _dot((proj(4) * conv).astype(BF16), wpc_ref[...])
    mix = mix + jax.nn.sigmoid(proj(8)) * yc

    x_new = x + gt_ref[0] * _dot(mix.astype(BF16), wout_ref[...])
    o_ref[0] = x_new
    first = jnp.logical_and(pl.program_id(0) == 0, j == 0)
    _route_tile(x_new, sh2_ref[0], sc2_ref[0], n2_ref[...], rwt_ref[...], rb_ref[...], first,
                hpk_ref, idx_ref, gw_ref, rank_ref, cnt_ref, carry_ref)


def _mixer(x, pending, sh, sc, gt, sh2, sc2, n1, w_in, gng, ws, bs, wpa, pool_w, pool_scale, conv_w, wpc,
           w_out, n2, router_w, router_b):
    bsz, seq, d = x.shape
    n_tok = bsz * seq
    tm = min(MIX_ROWS if pending is None else MIX_ROWS_PENDING, seq)
    nj = seq // tm
    vec = pl.BlockSpec((1, 1, d), lambda b, j: (b, 0, 0))
    tile = pl.BlockSpec((1, tm, d), lambda b, j: (b, j, 0))
    per_tok = pl.BlockSpec((TOP_K, tm), lambda b, j: (0, b * nj + j))
    pending_specs, pending_args = [], []
    if pending is not None:
        g, gw, gt_prev = pending
        pending_specs = [pl.BlockSpec((TOP_K, tm, d // 2), lambda b, j: (0, b * nj + j, 0)),
                         pl.BlockSpec((tm, TOP_K), lambda b, j: (b * nj + j, 0)), vec]
        pending_args = [g, gw.T, gt_prev]
    return pl.pallas_call(
        functools.partial(_mixer_kernel, pending=pending is not None),
        grid=(bsz, nj),
        in_specs=pending_specs + [
            tile, vec, vec, vec, vec, vec,
            _resident((1, d)),
            _resident(w_in.shape),
            _resident((1, d)),
            _resident(ws.shape),
            _resident((CHUNK, GMLP_HEADS)),
            _resident(wpa.shape),
            _resident(pool_w.shape),
            _resident((1, d)),
            _resident(conv_w.shape),
            _resident(wpc.shape),
            _resident(w_out.shape),
            _resident((1, d)), _resident((N_EXPERTS, d)), _resident((N_EXPERTS, 1)),
        ],
        out_specs=[
            tile,
            pl.BlockSpec((tm, d // 2), lambda b, j: (b * nj + j, 0)),
            per_tok, per_tok, per_tok,
            pl.BlockSpec((N_EXPERTS, 128), lambda b, j: (0, 0)),
        ],
        out_shape=[
            jax.ShapeDtypeStruct(x.shape, F32),
            jax.ShapeDtypeStruct((n_tok, d // 2), U32),
            jax.ShapeDtypeStruct((TOP_K, n_tok), I32),
            jax.ShapeDtypeStruct((TOP_K, n_tok), F32),
            jax.ShapeDtypeStruct((TOP_K, n_tok), I32),
            jax.ShapeDtypeStruct((N_EXPERTS, 128), I32),
        ],
        scratch_shapes=[
            pltpu.VMEM((tm, d), F32),
            pltpu.VMEM((POOL_CARRY + tm, d), F32),
            pltpu.VMEM((CONV_CARRY + tm, d), F32),
            pltpu.VMEM((N_EXPERTS, 128), F32),
        ],
        compiler_params=_params(2),
        name="mixer",
    )(*pending_args, x, sh, sc, gt, sh2, sc2, n1.reshape(1, d), w_in.astype(BF16), gng.reshape(1, d), ws, bs.T,
      wpa.astype(BF16), pool_w.astype(BF16), pool_scale.reshape(1, d), conv_w, wpc.astype(BF16),
      w_out.astype(BF16), n2.reshape(1, d), router_w.T.astype(BF16), router_b.reshape(N_EXPERTS, 1))


def _route_tile(x, sh, sc, n2, rwt, rb, first, hpk_ref, idx_ref, gw_ref, rank_ref, cnt_ref, carry_ref):
    tm = x.shape[0]

    @pl.when(first)
    def _():
        carry_ref[...] = jnp.zeros(carry_ref.shape, F32)

    h = _rms(x, n2) * (1.0 + sc) + sh
    hb = h.astype(BF16)
    hpk_ref[...] = _pack_bf16_pairs(h)

    logits = lax.dot_general(rwt, hb, (((1,), (1,)), ((), ())), preferred_element_type=F32) + rb
    iota_e = lax.broadcasted_iota(I32, logits.shape, 0)
    vals, idxs, sels = [], [], []
    rest = logits
    for _ in range(TOP_K):
        m = jnp.max(rest, axis=0, keepdims=True)
        ik = jnp.min(jnp.where(rest == m, iota_e, N_EXPERTS), axis=0, keepdims=True)
        sel = iota_e == ik
        rest = jnp.where(sel, -jnp.inf, rest)
        vals.append(m)
        idxs.append(ik)
        sels.append(sel)
    exps = [jnp.exp(v - vals[0]) for v in vals]
    denom = exps[0] + exps[1] + exps[2] + exps[3]

    chosen = jnp.logical_or(jnp.logical_or(sels[0], sels[1]), jnp.logical_or(sels[2], sels[3]))
    a = jnp.where(chosen, 1.0, 0.0)
    before = lax.broadcasted_iota(I32, (tm, tm), 0) < lax.broadcasted_iota(I32, (tm, tm), 1)
    prior = _dot(a.astype(BF16), jnp.where(before, 1.0, 0.0).astype(BF16)) + carry_ref[:, 0:1]
    for k in range(TOP_K):
        idx_ref[k:k + 1, :] = idxs[k]
        gw_ref[k:k + 1, :] = exps[k] / denom
        rank_ref[k:k + 1, :] = jnp.sum(jnp.where(sels[k], prior, 0.0), axis=0, keepdims=True).astype(I32)
    total = carry_ref[...] + jnp.sum(a, axis=1, keepdims=True)
    carry_ref[...] = total
    cnt_ref[...] = total.astype(I32)


def _dest_kernel(start_ref, idx_ref, rank_ref, o_ref):
    idx = idx_ref[...]
    base = jnp.zeros(idx.shape, I32)
    for e in range(N_EXPERTS):
        base = jnp.where(idx == e, start_ref[e], base)
    o_ref[...] = base + rank_ref[...]


def _dest(pad_start, idx, rank):
    n_tok = idx.shape[1]
    tc = min(DEST_COLS, n_tok)
    blk = pl.BlockSpec((TOP_K, tc), lambda i, s: (0, i))
    return pl.pallas_call(
        _dest_kernel,
        grid_spec=pltpu.PrefetchScalarGridSpec(
            num_scalar_prefetch=1, grid=(n_tok // tc,), in_specs=[blk, blk], out_specs=blk),
        out_shape=jax.ShapeDtypeStruct(idx.shape, I32),
        compiler_params=_params(1),
        name="dest",
    )(pad_start, idx, rank)


def _dispatch(hpk, dest, n_rows):
    n_tok, half = hpk.shape
    win = SC_WINDOW
    assert n_tok % (win * SC_WORKERS) == 0
    per_worker = n_tok // win // SC_WORKERS
    mesh = plsc.VectorSubcoreMesh(core_axis_name="c", subcore_axis_name="s")

    @pl.kernel(out_type=jax.ShapeDtypeStruct((n_rows, half), U32), mesh=mesh,
               scratch_types=[pltpu.VMEM((win, half), U32), pltpu.VMEM((TOP_K, win), I32)])
    def scatter_rows(hpk_hbm, dest_hbm, xs_hbm, xbuf, ibuf):
        worker = lax.axis_index("c") * SC_SUBCORES + lax.axis_index("s")

        @pl.loop(0, per_worker)
        def _(j):
            t0 = pl.multiple_of((worker * per_worker + j) * win, win)
            pltpu.sync_copy(hpk_hbm.at[pl.ds(t0, win)], xbuf)
            pltpu.sync_copy(dest_hbm.at[:, pl.ds(t0, win)], ibuf)
            for k in range(TOP_K):
                pltpu.sync_copy(xbuf, xs_hbm.at[ibuf.at[k]])

    return scatter_rows(hpk, dest)


def _expert_kernel(be_ref, nv_ref, live_ref, xs_ref, wgu_ref, bgu_ref, wd_ref, bd_ref, ys_ref, wgu_bf, wd_bf):
    i = pl.program_id(0)

    @pl.when(i >= nv_ref[0])
    def _():
        ys_ref[...] = jnp.zeros(ys_ref.shape, U32)

    @pl.when(jnp.logical_or(i == 0, be_ref[i] != be_ref[jnp.maximum(i - 1, 0)]))
    def _():
        wgu_bf[...] = wgu_ref[0, 0].astype(BF16)
        wd_bf[...] = wd_ref[0, 0].astype(BF16)

    @pl.when(i < nv_ref[0])
    def _():
        live = lax.broadcasted_iota(I32, (xs_ref.shape[0], 1), 0) < live_ref[i]
        xb = jnp.where(live, _unpack_bf16_pairs(xs_ref[...]), 0.0).astype(BF16)
        gu = _dot(xb, wgu_bf[...]) + bgu_ref[0, 0]
        ff = gu.shape[1] // 2
        gate = jnp.minimum(gu[:, :ff], SWIGLU_LIMIT)
        up = jnp.clip(gu[:, ff:], -SWIGLU_LIMIT, SWIGLU_LIMIT)
        glu = gate * jax.nn.sigmoid(SWIGLU_ALPHA * gate)
        ys_ref[...] = _pack_bf16_pairs(_dot(((up + 1.0) * glu).astype(BF16), wd_bf[...]) + bd_ref[0, 0])


def _experts(layer, block_e, n_valid, live_rows, xs, w_gu, b_gu, w_down, b_down):
    n_rows, half = xs.shape
    depth, _, d, two_f = w_gu.shape
    tb = EXPERT_ROWS

    def rows(i, be, nv, lv):
        return (jnp.minimum(i, nv[0] - 1), 0)

    def by_expert(i, be, nv, lv):
        return (layer, be[i], 0, 0)

    return pl.pallas_call(
        _expert_kernel,
        grid_spec=pltpu.PrefetchScalarGridSpec(
            num_scalar_prefetch=3,
            grid=(n_rows // tb,),
            in_specs=[
                pl.BlockSpec((tb, half), rows),
                pl.BlockSpec((1, 1, d, two_f), by_expert),
                pl.BlockSpec((1, 1, 1, two_f), by_expert),
                pl.BlockSpec((1, 1, two_f // 2, d), by_expert),
                pl.BlockSpec((1, 1, 1, d), by_expert),
            ],
            out_specs=pl.BlockSpec((tb, half), lambda i, be, nv, lv: (i, 0)),
            scratch_shapes=[pltpu.VMEM((d, two_f), BF16), pltpu.VMEM((two_f // 2, d), BF16)],
        ),
        out_shape=jax.ShapeDtypeStruct((n_rows, half), U32),
        compiler_params=_params(1),
        name="experts",
    )(block_e, n_valid, live_rows, xs, w_gu, b_gu.reshape(depth, N_EXPERTS, 1, two_f), w_down,
      b_down.reshape(depth, N_EXPERTS, 1, d))


def _gather_rows(dest, ys):
    n_tok = dest.shape[1]
    width = ys.shape[1]
    win = SC_WINDOW
    assert n_tok % (win * SC_WORKERS) == 0
    per_worker = n_tok // win // SC_WORKERS
    mesh = plsc.VectorSubcoreMesh(core_axis_name="c", subcore_axis_name="s")

    @pl.kernel(out_type=jax.ShapeDtypeStruct((TOP_K, n_tok, width), ys.dtype), mesh=mesh,
               scratch_types=[pltpu.VMEM((win, width), ys.dtype), pltpu.VMEM((TOP_K, win), I32)])
    def gather(ys_hbm, idx_hbm, g_hbm, buf, ibuf):
        worker = lax.axis_index("c") * SC_SUBCORES + lax.axis_index("s")

        @pl.loop(0, per_worker)
        def _(j):
            t0 = pl.multiple_of((worker * per_worker + j) * win, win)
            pltpu.sync_copy(idx_hbm.at[:, pl.ds(t0, win)], ibuf)
            for k in range(TOP_K):
                pltpu.sync_copy(ys_hbm.at[ibuf.at[k]], buf)
                pltpu.sync_copy(buf, g_hbm.at[k, pl.ds(t0, win)])

    return gather(ys, dest)


def _final_kernel(g_ref, x_ref, gw_ref, gt_ref, fg_ref, o_ref):
    y = x_ref[...] + gt_ref[0] * _weighted_rows(g_ref, gw_ref[...])
    o_ref[...] = _rms(y, fg_ref[...])


def _final(g, x, gw, gt, final_g):
    bsz, seq, d = x.shape
    n_tok = bsz * seq
    tm = min(FINAL_ROWS, seq)
    per_seq = seq // tm
    out = pl.pallas_call(
        _final_kernel,
        grid=(n_tok // tm,),
        in_specs=[
            pl.BlockSpec((TOP_K, tm, d // 2), lambda i: (0, i, 0)),
            pl.BlockSpec((tm, d), lambda i: (i, 0)),
            pl.BlockSpec((tm, TOP_K), lambda i: (i, 0)),
            pl.BlockSpec((1, 1, d), lambda i: (i // per_seq, 0, 0)),
            pl.BlockSpec((1, d), lambda i: (0, 0)),
        ],
        out_specs=pl.BlockSpec((tm, d), lambda i: (i, 0)),
        out_shape=jax.ShapeDtypeStruct((n_tok, d), F32),
        compiler_params=_params(1),
        name="final",
    )(g, x.reshape(n_tok, d), gw.T, gt, final_g.reshape(1, d))
    return out.reshape(bsz, seq, d)


def _moe(layer, routed, w_gu, b_gu, w_down, b_down):
    n_tok = routed[0].shape[0]
    tb = EXPERT_ROWS
    n_rows = -(-(n_tok * TOP_K + N_EXPERTS * (tb - 1)) // tb) * tb
    hpk, idx, gw, rank, cnt = routed
    counts = cnt[:, 0]
    padded = (counts + tb - 1) // tb * tb
    pad_end = jnp.cumsum(padded)
    pad_start = pad_end - padded
    n_valid = pad_end[-1] // tb
    blocks = jnp.arange(n_rows // tb, dtype=I32)
    block_e = jnp.sum((pad_end[None, :] <= (blocks * tb)[:, None]).astype(I32), axis=1)
    block_e = jnp.minimum(block_e, N_EXPERTS - 1)
    block_e = jnp.where(blocks < n_valid, block_e, block_e[n_valid - 1])
    live_rows = jnp.clip(counts[block_e] - (blocks * tb - pad_start[block_e]), 0, tb)
    live_rows = jnp.where(blocks < n_valid, live_rows, 0).astype(I32)
    dest = _dest(pad_start, idx, rank)
    xs = _dispatch(hpk, dest, n_rows)
    ys = _experts(layer, block_e, n_valid.reshape(1), live_rows, xs, w_gu, b_gu, w_down, b_down)
    return _gather_rows(dest, ys)


def kernel(x, c, norm1_g, ada_w, ada_b, w_in, gmlp_norm_g, gmlp_ws, gmlp_bs, w_proj_a, pool_w, pool_scale,
           conv_w, w_proj_c, w_out, norm2_g, router_w, router_b, exp_w_gu, exp_b_gu, exp_w_down,
           exp_b_down, final_g):
    depth = ada_w.shape[0]
    bsz, _, d = x.shape
    mods = _ada(c, ada_w, ada_b)
    pending = None
    for l in range(depth):
        sh1, sc1, g1, sh2, sc2, g2 = [mods[l, :, i * d:(i + 1) * d].reshape(bsz, 1, d) for i in range(6)]
        x, *routed = _mixer(x, pending, sh1, sc1, g1, sh2, sc2, norm1_g[l], w_in[l], gmlp_norm_g[l],
                            gmlp_ws[l], gmlp_bs[l], w_proj_a[l], pool_w[l], pool_scale[l], conv_w[l],
                            w_proj_c[l], w_out[l], norm2_g[l], router_w[l], router_b[l])
        gate_w = routed[2]
        pending = (_moe(l, routed, exp_w_gu, exp_b_gu, exp_w_down, exp_b_down), gate_w, g2)
    g, gate_w, g2 = pending
    return _final(g, x, gate_w, g2, final_g)
```

```python
import functools

import jax
import jax.numpy as jnp
from jax import lax
from jax.experimental import pallas as pl
from jax.experimental.pallas import tpu as pltpu
from jax.experimental.pallas import tpu_sc as plsc

F32 = jnp.float32
BF16 = jnp.bfloat16
I32 = jnp.int32
U32 = jnp.uint32

RMS_EPS = 1e-5
GMLP_HEADS = 8
CHUNK = 128
POOL_WINDOWS = (2, 4, 8, 16)
POOL_CARRY = 16
CONV_K = 3
CONV_CARRY = 8
N_EXPERTS = 32
TOP_K = 4
SWIGLU_LIMIT = 7.0
SWIGLU_ALPHA = 1.702

MIX_ROWS = 512
MIX_ROWS_PENDING = 256
DEST_COLS = 4096
EXPERT_ROWS = 512
FINAL_ROWS = 512
SC_SUBCORES = 16
SC_WORKERS = 2 * SC_SUBCORES
SC_WINDOW = 128

VMEM_LIMIT = 56 * 1024 * 1024


def _rms(x, g):
    return x * lax.rsqrt(jnp.mean(x * x, axis=-1, keepdims=True) + RMS_EPS) * g


def _dot(a, b):
    return jnp.dot(a, b, preferred_element_type=F32)


def _params(n_axes, vmem=VMEM_LIMIT):
    return pltpu.CompilerParams(dimension_semantics=("arbitrary",) * n_axes, vmem_limit_bytes=vmem)


def _pack_bf16_pairs(x):
    half = x.shape[1] // 2
    bits = lax.bitcast_convert_type(x.astype(BF16).astype(F32), U32)
    return (bits[:, :half] >> 16) | bits[:, half:]


def _unpack_bf16_pairs(w):
    lo = lax.bitcast_convert_type(w << 16, F32)
    hi = lax.bitcast_convert_type(w & jnp.uint32(0xFFFF0000), F32)
    return jnp.concatenate([lo, hi], axis=1)


def _resident(shape):
    zeros = (0,) * len(shape)
    return pl.BlockSpec(shape, lambda *_: zeros, pipeline_mode=pl.Buffered(1))


def _ada_kernel(c_ref, w_ref, b_ref, o_ref):
    c = c_ref[...]
    cond = c * jax.nn.sigmoid(c)
    o_ref[0] = _dot(cond.astype(BF16), w_ref[0].astype(BF16)) + b_ref[0]


def _ada(c, ada_w, ada_b):
    depth, d, six_d = ada_w.shape
    bsz = c.shape[0]
    return pl.pallas_call(
        _ada_kernel,
        grid=(depth, six_d // d),
        in_specs=[
            pl.BlockSpec((bsz, d), lambda l, j: (0, 0)),
            pl.BlockSpec((1, d, d), lambda l, j: (l, 0, j)),
            pl.BlockSpec((1, 1, d), lambda l, j: (l, 0, j)),
        ],
        out_specs=pl.BlockSpec((1, bsz, d), lambda l, j: (l, 0, j)),
        out_shape=jax.ShapeDtypeStruct((depth, bsz, six_d), F32),
        compiler_params=_params(2),
        name="ada",
    )(c, ada_w, ada_b.reshape(depth, 1, six_d))


def _weighted_rows(g_ref, gw):
    acc = gw[:, 0:1] * _unpack_bf16_pairs(g_ref[0])
    for k in range(1, TOP_K):
        acc = acc + gw[:, k:k + 1] * _unpack_bf16_pairs(g_ref[k])
    return acc


def _mixer_kernel(*refs, pending):
    if pending:
        g_ref, gwp_ref, gtp_ref, *refs = refs
    (x_ref, sh_ref, sc_ref, gt_ref, sh2_ref, sc2_ref, n1_ref, win_ref, gng_ref, ws_ref, bst_ref,
     wpa_ref, pw_ref, ps_ref, cw_ref, wpc_ref, wout_ref, n2_ref, rwt_ref, rb_ref,
     o_ref, hpk_ref, idx_ref, gw_ref, rank_ref, cnt_ref,
     s_ref, pext_ref, zext_ref, carry_ref) = refs
    tm, d = x_ref.shape[1], x_ref.shape[2]
    j = pl.program_id(1)

    @pl.when(j == 0)
    def _():
        pext_ref[0:POOL_CARRY, :] = jnp.zeros((POOL_CARRY, d), F32)
        zext_ref[0:CONV_CARRY, :] = jnp.zeros((CONV_CARRY, d), F32)

    x = x_ref[0]
    if pending:
        x = x + gtp_ref[0] * _weighted_rows(g_ref, gwp_ref[...])
    hb = (_rms(x, n1_ref[...]) * (1.0 + sc_ref[0]) + sh_ref[0]).astype(BF16)

    def proj(col):
        return _dot(hb, win_ref[:, col * d:(col + 1) * d])

    u = jax.nn.gelu(proj(0))
    vb = _rms(jax.nn.gelu(proj(1)), gng_ref[...]).astype(BF16)
    hd = d // GMLP_HEADS
    row = lax.broadcasted_iota(I32, (CHUNK, CHUNK), 0)
    col = lax.broadcasted_iota(I32, (CHUNK, CHUNK), 1)
    for h in range(GMLP_HEADS):
        wm = jnp.where(row >= col, ws_ref[h], 0.0).astype(BF16)
        bias = bst_ref[:, h:h + 1]
        for ci in range(tm // CHUNK):
            blk = vb[ci * CHUNK:(ci + 1) * CHUNK, h * hd:(h + 1) * hd]
            s_ref[ci * CHUNK:(ci + 1) * CHUNK, h * hd:(h + 1) * hd] = _dot(wm, blk) + bias
    ya = _dot((u * s_ref[...]).astype(BF16), wpa_ref[...])
    mix = jax.nn.sigmoid(proj(6)) * ya

    p = proj(2)
    pext_ref[POOL_CARRY:POOL_CARRY + tm, :] = p
    pos1 = (j * tm + 1 + lax.broadcasted_iota(I32, (tm, 1), 0))
    gd = d // len(POOL_WINDOWS)
    yb_parts = []
    for gi, w in enumerate(POOL_WINDOWS):
        cs = slice(gi * gd, (gi + 1) * gd)
        acc = p[:, cs]
        for k in range(1, w):
            acc = acc + pext_ref[POOL_CARRY - k:POOL_CARRY - k + tm, cs]
        cnt = jnp.minimum(pos1, w).astype(F32)
        dg = (acc / cnt - p[:, cs]).astype(BF16)
        yb_parts.append(_dot(dg, pw_ref[gi]))
    yb = jnp.concatenate(yb_parts, axis=1) * ps_ref[...]
    pext_ref[0:POOL_CARRY, :] = pext_ref[tm:tm + POOL_CARRY, :]
    mix = mix + jax.nn.sigmoid(proj(7)) * yb

    z = proj(5) * proj(3)
    zext_ref[CONV_CARRY:CONV_CARRY + tm, :] = z
    conv = cw_ref[CONV_K - 1:CONV_K, :] * z
    for k in range(CONV_K - 1):
        lag = CONV_K - 1 - k
        conv = conv + cw_ref[k:k + 1, :] * zext_ref[CONV_CARRY - lag:CONV_CARRY - lag + tm, :]
    zext_ref[0:CONV_CARRY, :] = zext_ref[tm:tm + CONV_CARRY, :]
    yc = _dot((proj(4) * conv).astype(BF16), wpc_ref[...])
    mix = mix + jax.nn.sigmoid(proj(8)) * yc

    x_new = x + gt_ref[0] * _dot(mix.astype(BF16), wout_ref[...])
    o_ref[0] = x_new
    first = jnp.logical_and(pl.program_id(0) == 0, j == 0)
    _route_tile(x_new, sh2_ref[0], sc2_ref[0], n2_ref[...], rwt_ref[...], rb_ref[...], first,
                hpk_ref, idx_ref, gw_ref, rank_ref, cnt_ref, carry_ref)


def _mixer(x, pending, sh, sc, gt, sh2, sc2, n1, w_in, gng, ws, bs, wpa, pool_w, pool_scale, conv_w, wpc,
           w_out, n2, router_w, router_b):
    bsz, seq, d = x.shape
    n_tok = bsz * seq
    tm = min(MIX_ROWS if pending is None else MIX_ROWS_PENDING, seq)
    nj = seq // tm
    vec = pl.BlockSpec((1, 1, d), lambda b, j: (b, 0, 0))
    tile = pl.BlockSpec((1, tm, d), lambda b, j: (b, j, 0))
    per_tok = pl.BlockSpec((TOP_K, tm), lambda b, j: (0, b * nj + j))
    pending_specs, pending_args = [], []
    if pending is not None:
        g, gw, gt_prev = pending
        pending_specs = [pl.BlockSpec((TOP_K, tm, d // 2), lambda b, j: (0, b * nj + j, 0)),
                         pl.BlockSpec((tm, TOP_K), lambda b, j: (b * nj + j, 0)), vec]
        pending_args = [g, gw.T, gt_prev]
    return pl.pallas_call(
        functools.partial(_mixer_kernel, pending=pending is not None),
        grid=(bsz, nj),
        in_specs=pending_specs + [
            tile, vec, vec, vec, vec, vec,
            _resident((1, d)),
            _resident(w_in.shape),
            _resident((1, d)),
            _resident(ws.shape),
            _resident((CHUNK, GMLP_HEADS)),
            _resident(wpa.shape),
            _resident(pool_w.shape),
            _resident((1, d)),
            _resident(conv_w.shape),
            _resident(wpc.shape),
            _resident(w_out.shape),
            _resident((1, d)), _resident((N_EXPERTS, d)), _resident((N_EXPERTS, 1)),
        ],
        out_specs=[
            tile,
            pl.BlockSpec((tm, d // 2), lambda b, j: (b * nj + j, 0)),
            per_tok, per_tok, per_tok,
            pl.BlockSpec((N_EXPERTS, 128), lambda b, j: (0, 0)),
        ],
        out_shape=[
            jax.ShapeDtypeStruct(x.shape, F32),
            jax.ShapeDtypeStruct((n_tok, d // 2), U32),
            jax.ShapeDtypeStruct((TOP_K, n_tok), I32),
            jax.ShapeDtypeStruct((TOP_K, n_tok), F32),
            jax.ShapeDtypeStruct((TOP_K, n_tok), I32),
            jax.ShapeDtypeStruct((N_EXPERTS, 128), I32),
        ],
        scratch_shapes=[
            pltpu.VMEM((tm, d), F32),
            pltpu.VMEM((POOL_CARRY + tm, d), F32),
            pltpu.VMEM((CONV_CARRY + tm, d), F32),
            pltpu.VMEM((N_EXPERTS, 128), F32),
        ],
        compiler_params=_params(2),
        name="mixer",
    )(*pending_args, x, sh, sc, gt, sh2, sc2, n1.reshape(1, d), w_in.astype(BF16), gng.reshape(1, d), ws, bs.T,
      wpa.astype(BF16), pool_w.astype(BF16), pool_scale.reshape(1, d), conv_w, wpc.astype(BF16),
      w_out.astype(BF16), n2.reshape(1, d), router_w.T.astype(BF16), router_b.reshape(N_EXPERTS, 1))


def _route_tile(x, sh, sc, n2, rwt, rb, first, hpk_ref, idx_ref, gw_ref, rank_ref, cnt_ref, carry_ref):
    tm = x.shape[0]

    @pl.when(first)
    def _():
        carry_ref[...] = jnp.zeros(carry_ref.shape, F32)

    h = _rms(x, n2) * (1.0 + sc) + sh
    hb = h.astype(BF16)
    hpk_ref[...] = _pack_bf16_pairs(h)

    logits = lax.dot_general(rwt, hb, (((1,), (1,)), ((), ())), preferred_element_type=F32) + rb
    iota_e = lax.broadcasted_iota(I32, logits.shape, 0)
    vals, idxs, sels = [], [], []
    rest = logits
    for _ in range(TOP_K):
        m = jnp.max(rest, axis=0, keepdims=True)
        ik = jnp.min(jnp.where(rest == m, iota_e, N_EXPERTS), axis=0, keepdims=True)
        sel = iota_e == ik
        rest = jnp.where(sel, -jnp.inf, rest)
        vals.append(m)
        idxs.append(ik)
        sels.append(sel)
    exps = [jnp.exp(v - vals[0]) for v in vals]
    denom = exps[0] + exps[1] + exps[2] + exps[3]

    chosen = jnp.logical_or(jnp.logical_or(sels[0], sels[1]), jnp.logical_or(sels[2], sels[3]))
    a = jnp.where(chosen, 1.0, 0.0)
    before = lax.broadcasted_iota(I32, (tm, tm), 0) < lax.broadcasted_iota(I32, (tm, tm), 1)
    prior = _dot(a.astype(BF16), jnp.where(before, 1.0, 0.0).astype(BF16)) + carry_ref[:, 0:1]
    for k in range(TOP_K):
        idx_ref[k:k + 1, :] = idxs[k]
        gw_ref[k:k + 1, :] = exps[k] / denom
        rank_ref[k:k + 1, :] = jnp.sum(jnp.where(sels[k], prior, 0.0), axis=0, keepdims=True).astype(I32)
    total = carry_ref[...] + jnp.sum(a, axis=1, keepdims=True)
    carry_ref[...] = total
    cnt_ref[...] = total.astype(I32)


def _dest_kernel(start_ref, idx_ref, rank_ref, o_ref):
    idx = idx_ref[...]
    base = jnp.zeros(idx.shape, I32)
    for e in range(N_EXPERTS):
        base = jnp.where(idx == e, start_ref[e], base)
    o_ref[...] = base + rank_ref[...]


def _dest(pad_start, idx, rank):
    n_tok = idx.shape[1]
    tc = min(DEST_COLS, n_tok)
    blk = pl.BlockSpec((TOP_K, tc), lambda i, s: (0, i))
    return pl.pallas_call(
        _dest_kernel,
        grid_spec=pltpu.PrefetchScalarGridSpec(
            num_scalar_prefetch=1, grid=(n_tok // tc,), in_specs=[blk, blk], out_specs=blk),
        out_shape=jax.ShapeDtypeStruct(idx.shape, I32),
        compiler_params=_params(1),
        name="dest",
    )(pad_start, idx, rank)


def _dispatch(hpk, dest, n_rows):
    n_tok, half = hpk.shape
    win = SC_WINDOW
    assert n_tok % (win * SC_WORKERS) == 0
    per_worker = n_tok // win // SC_WORKERS
    mesh = plsc.VectorSubcoreMesh(core_axis_name="c", subcore_axis_name="s")

    @pl.kernel(out_type=jax.ShapeDtypeStruct((n_rows, half), U32), mesh=mesh,
               scratch_types=[pltpu.VMEM((win, half), U32), pltpu.VMEM((TOP_K, win), I32)])
    def scatter_rows(hpk_hbm, dest_hbm, xs_hbm, xbuf, ibuf):
        worker = lax.axis_index("c") * SC_SUBCORES + lax.axis_index("s")

        @pl.loop(0, per_worker)
        def _(j):
            t0 = pl.multiple_of((worker * per_worker + j) * win, win)
            pltpu.sync_copy(hpk_hbm.at[pl.ds(t0, win)], xbuf)
            pltpu.sync_copy(dest_hbm.at[:, pl.ds(t0, win)], ibuf)
            for k in range(TOP_K):
                pltpu.sync_copy(xbuf, xs_hbm.at[ibuf.at[k]])

    return scatter_rows(hpk, dest)


def _expert_kernel(be_ref, nv_ref, live_ref, first_ref, slot_ref, next_ref, xs_ref, wgu_hbm, bgu_ref, wd_hbm,
                   bd_ref, ys_ref, wgu_f32, wd_f32, wgu_bf, wd_bf, sem, *, layer):
    i = pl.program_id(0)

    def weight_copies(expert, slot):
        return (pltpu.make_async_copy(wgu_hbm.at[layer, expert], wgu_f32.at[slot], sem.at[0, slot]),
                pltpu.make_async_copy(wd_hbm.at[layer, expert], wd_f32.at[slot], sem.at[1, slot]))

    @pl.when(i >= nv_ref[0])
    def _():
        ys_ref[...] = jnp.zeros(ys_ref.shape, U32)

    @pl.when(i == 0)
    def _():
        for copy in weight_copies(be_ref[0], 0):
            copy.start()

    @pl.when(first_ref[i] == 1)
    def _():
        slot = slot_ref[i]

        @pl.when(next_ref[i] >= 0)
        def _():
            for copy in weight_copies(next_ref[i], 1 - slot):
                copy.start()

        for copy in weight_copies(be_ref[i], slot):
            copy.wait()
        wgu_bf[...] = wgu_f32[slot].astype(BF16)
        wd_bf[...] = wd_f32[slot].astype(BF16)

    @pl.when(i < nv_ref[0])
    def _():
        live = lax.broadcasted_iota(I32, (xs_ref.shape[0], 1), 0) < live_ref[i]
        xb = jnp.where(live, _unpack_bf16_pairs(xs_ref[...]), 0.0).astype(BF16)
        gu = _dot(xb, wgu_bf[...]) + bgu_ref[0, 0]
        ff = gu.shape[1] // 2
        gate = jnp.minimum(gu[:, :ff], SWIGLU_LIMIT)
        up = jnp.clip(gu[:, ff:], -SWIGLU_LIMIT, SWIGLU_LIMIT)
        glu = gate * jax.nn.sigmoid(SWIGLU_ALPHA * gate)
        ys_ref[...] = _pack_bf16_pairs(_dot(((up + 1.0) * glu).astype(BF16), wd_bf[...]) + bd_ref[0, 0])


def _experts(layer, plan, xs, w_gu, b_gu, w_down, b_down):
    n_rows, half = xs.shape
    depth, _, d, two_f = w_gu.shape
    tb = EXPERT_ROWS

    def rows(i, be, nv, *_):
        return (jnp.minimum(i, nv[0] - 1), 0)

    def by_expert(i, be, *_):
        return (layer, be[i], 0, 0)

    return pl.pallas_call(
        functools.partial(_expert_kernel, layer=layer),
        grid_spec=pltpu.PrefetchScalarGridSpec(
            num_scalar_prefetch=len(plan),
            grid=(n_rows // tb,),
            in_specs=[
                pl.BlockSpec((tb, half), rows),
                pl.BlockSpec(memory_space=pl.ANY),
                pl.BlockSpec((1, 1, 1, two_f), by_expert),
                pl.BlockSpec(memory_space=pl.ANY),
                pl.BlockSpec((1, 1, 1, d), by_expert),
            ],
            out_specs=pl.BlockSpec((tb, half), lambda i, *_: (i, 0)),
            scratch_shapes=[
                pltpu.VMEM((2, d, two_f), F32), pltpu.VMEM((2, two_f // 2, d), F32),
                pltpu.VMEM((d, two_f), BF16), pltpu.VMEM((two_f // 2, d), BF16),
                pltpu.SemaphoreType.DMA((2, 2)),
            ],
        ),
        out_shape=jax.ShapeDtypeStruct((n_rows, half), U32),
        compiler_params=_params(1),
        name="experts",
    )(*plan, xs, w_gu, b_gu.reshape(depth, N_EXPERTS, 1, two_f), w_down, b_down.reshape(depth, N_EXPERTS, 1, d))


def _gather_rows(dest, ys):
    n_tok = dest.shape[1]
    width = ys.shape[1]
    win = SC_WINDOW
    assert n_tok % (win * SC_WORKERS) == 0
    per_worker = n_tok // win // SC_WORKERS
    mesh = plsc.VectorSubcoreMesh(core_axis_name="c", subcore_axis_name="s")

    @pl.kernel(out_type=jax.ShapeDtypeStruct((TOP_K, n_tok, width), ys.dtype), mesh=mesh,
               scratch_types=[pltpu.VMEM((win, width), ys.dtype), pltpu.VMEM((TOP_K, win), I32)])
    def gather(ys_hbm, idx_hbm, g_hbm, buf, ibuf):
        worker = lax.axis_index("c") * SC_SUBCORES + lax.axis_index("s")

        @pl.loop(0, per_worker)
        def _(j):
            t0 = pl.multiple_of((worker * per_worker + j) * win, win)
            pltpu.sync_copy(idx_hbm.at[:, pl.ds(t0, win)], ibuf)
            for k in range(TOP_K):
                pltpu.sync_copy(ys_hbm.at[ibuf.at[k]], buf)
                pltpu.sync_copy(buf, g_hbm.at[k, pl.ds(t0, win)])

    return gather(ys, dest)


def _final_kernel(g_ref, x_ref, gw_ref, gt_ref, fg_ref, o_ref):
    y = x_ref[...] + gt_ref[0] * _weighted_rows(g_ref, gw_ref[...])
    o_ref[...] = _rms(y, fg_ref[...])


def _final(g, x, gw, gt, final_g):
    bsz, seq, d = x.shape
    n_tok = bsz * seq
    tm = min(FINAL_ROWS, seq)
    per_seq = seq // tm
    out = pl.pallas_call(
        _final_kernel,
        grid=(n_tok // tm,),
        in_specs=[
            pl.BlockSpec((TOP_K, tm, d // 2), lambda i: (0, i, 0)),
            pl.BlockSpec((tm, d), lambda i: (i, 0)),
            pl.BlockSpec((tm, TOP_K), lambda i: (i, 0)),
            pl.BlockSpec((1, 1, d), lambda i: (i // per_seq, 0, 0)),
            pl.BlockSpec((1, d), lambda i: (0, 0)),
        ],
        out_specs=pl.BlockSpec((tm, d), lambda i: (i, 0)),
        out_shape=jax.ShapeDtypeStruct((n_tok, d), F32),
        compiler_params=_params(1),
        name="final",
    )(g, x.reshape(n_tok, d), gw.T, gt, final_g.reshape(1, d))
    return out.reshape(bsz, seq, d)


def _moe(layer, routed, w_gu, b_gu, w_down, b_down):
    n_tok = routed[0].shape[0]
    tb = EXPERT_ROWS
    n_rows = -(-(n_tok * TOP_K + N_EXPERTS * (tb - 1)) // tb) * tb
    hpk, idx, gw, rank, cnt = routed
    counts = cnt[:, 0]
    padded = (counts + tb - 1) // tb * tb
    pad_end = jnp.cumsum(padded)
    pad_start = pad_end - padded
    n_valid = pad_end[-1] // tb
    blocks = jnp.arange(n_rows // tb, dtype=I32)
    block_e = jnp.sum((pad_end[None, :] <= (blocks * tb)[:, None]).astype(I32), axis=1)
    block_e = jnp.minimum(block_e, N_EXPERTS - 1)
    block_e = jnp.where(blocks < n_valid, block_e, block_e[n_valid - 1])
    live_rows = jnp.clip(counts[block_e] - (blocks * tb - pad_start[block_e]), 0, tb)
    live_rows = jnp.where(blocks < n_valid, live_rows, 0).astype(I32)
    first = jnp.logical_and(blocks < n_valid, blocks * tb == pad_start[block_e])
    slot = (jnp.cumsum(first.astype(I32)) - 1) % 2
    experts = jnp.arange(N_EXPERTS, dtype=I32)
    used_from = lax.cummin(jnp.where(counts > 0, experts, N_EXPERTS), reverse=True)
    next_used = jnp.concatenate([used_from[1:], jnp.full((1,), N_EXPERTS, I32)])
    next_e = jnp.where(next_used < N_EXPERTS, next_used, -1)[block_e]
    plan = (block_e, n_valid.reshape(1), live_rows, first.astype(I32), slot.astype(I32), next_e.astype(I32))
    dest = _dest(pad_start, idx, rank)
    xs = _dispatch(hpk, dest, n_rows)
    ys = _experts(layer, plan, xs, w_gu, b_gu, w_down, b_down)
    return _gather_rows(dest, ys)


def kernel(x, c, norm1_g, ada_w, ada_b, w_in, gmlp_norm_g, gmlp_ws, gmlp_bs, w_proj_a, pool_w, pool_scale,
           conv_w, w_proj_c, w_out, norm2_g, router_w, router_b, exp_w_gu, exp_b_gu, exp_w_down,
           exp_b_down, final_g):
    depth = ada_w.shape[0]
    bsz, _, d = x.shape
    mods = _ada(c, ada_w, ada_b)
    pending = None
    for l in range(depth):
        sh1, sc1, g1, sh2, sc2, g2 = [mods[l, :, i * d:(i + 1) * d].reshape(bsz, 1, d) for i in range(6)]
        x, *routed = _mixer(x, pending, sh1, sc1, g1, sh2, sc2, norm1_g[l], w_in[l], gmlp_norm_g[l],
                            gmlp_ws[l], gmlp_bs[l], w_proj_a[l], pool_w[l], pool_scale[l], conv_w[l],
                            w_proj_c[l], w_out[l], norm2_g[l], router_w[l], router_b[l])
        gate_w = routed[2]
        pending = (_moe(l, routed, exp_w_gu, exp_b_gu, exp_w_down, exp_b_down), gate_w, g2)
    g, gate_w, g2 = pending
    return _final(g, x, gate_w, g2, final_g)
```

```python
import functools

import jax
import jax.numpy as jnp
from jax import lax
from jax.experimental import pallas as pl
from jax.experimental.pallas import tpu as pltpu
from jax.experimental.pallas import tpu_sc as plsc

F32 = jnp.float32
BF16 = jnp.bfloat16
I32 = jnp.int32
U32 = jnp.uint32

RMS_EPS = 1e-5
GMLP_HEADS = 8
CHUNK = 128
POOL_WINDOWS = (2, 4, 8, 16)
POOL_CARRY = 16
CONV_K = 3
CONV_CARRY = 8
N_EXPERTS = 32
TOP_K = 4
SWIGLU_LIMIT = 7.0
SWIGLU_ALPHA = 1.702

MIX_ROWS = 512
MIX_ROWS_PENDING = 512
DEST_COLS = 4096
EXPERT_ROWS = 512
EXPERT_BLOCKS_PER_STEP = 2
FINAL_ROWS = 512
SC_SUBCORES = 16
SC_WORKERS = 2 * SC_SUBCORES
SC_WINDOW = 128

VMEM_LIMIT = 58 * 1024 * 1024


def _rms(x, g):
    return x * lax.rsqrt(jnp.mean(x * x, axis=-1, keepdims=True) + RMS_EPS) * g


def _dot(a, b):
    return jnp.dot(a, b, preferred_element_type=F32)


def _params(n_axes, vmem=VMEM_LIMIT):
    return pltpu.CompilerParams(dimension_semantics=("arbitrary",) * n_axes, vmem_limit_bytes=vmem)


def _pack_bf16_pairs(x):
    half = x.shape[1] // 2
    bits = lax.bitcast_convert_type(x.astype(BF16).astype(F32), U32)
    return (bits[:, :half] >> 16) | bits[:, half:]


def _unpack_bf16_pairs(w):
    lo = lax.bitcast_convert_type(w << 16, F32)
    hi = lax.bitcast_convert_type(w & jnp.uint32(0xFFFF0000), F32)
    return jnp.concatenate([lo, hi], axis=1)


def _resident(shape):
    zeros = (0,) * len(shape)
    return pl.BlockSpec(shape, lambda *_: zeros, pipeline_mode=pl.Buffered(1))


def _ada_kernel(c_ref, w_ref, b_ref, o_ref):
    c = c_ref[...]
    cond = c * jax.nn.sigmoid(c)
    o_ref[0] = _dot(cond.astype(BF16), w_ref[0].astype(BF16)) + b_ref[0]


def _ada(c, ada_w, ada_b):
    depth, d, six_d = ada_w.shape
    bsz = c.shape[0]
    return pl.pallas_call(
        _ada_kernel,
        grid=(depth, six_d // d),
        in_specs=[
            pl.BlockSpec((bsz, d), lambda l, j: (0, 0)),
            pl.BlockSpec((1, d, d), lambda l, j: (l, 0, j)),
            pl.BlockSpec((1, 1, d), lambda l, j: (l, 0, j)),
        ],
        out_specs=pl.BlockSpec((1, bsz, d), lambda l, j: (l, 0, j)),
        out_shape=jax.ShapeDtypeStruct((depth, bsz, six_d), F32),
        compiler_params=_params(2),
        name="ada",
    )(c, ada_w, ada_b.reshape(depth, 1, six_d))


def _weighted_rows(g_ref, gw):
    acc = gw[:, 0:1] * _unpack_bf16_pairs(g_ref[0])
    for k in range(1, TOP_K):
        acc = acc + gw[:, k:k + 1] * _unpack_bf16_pairs(g_ref[k])
    return acc


def _mixer_kernel(*refs, pending):
    if pending:
        g_ref, gwp_ref, gtp_ref, *refs = refs
    (x_ref, sh_ref, sc_ref, gt_ref, sh2_ref, sc2_ref, n1_ref, win_ref, gng_ref, ws_ref, bst_ref,
     wpa_ref, pw_ref, ps_ref, cw_ref, wpc_ref, wout_ref, n2_ref, rwt_ref, rb_ref,
     o_ref, hpk_ref, idx_ref, gw_ref, rank_ref, cnt_ref,
     s_ref, pext_ref, zext_ref, carry_ref) = refs
    tm, d = x_ref.shape[1], x_ref.shape[2]
    j = pl.program_id(1)

    @pl.when(j == 0)
    def _():
        pext_ref[0:POOL_CARRY, :] = jnp.zeros((POOL_CARRY, d), F32)
        zext_ref[0:CONV_CARRY, :] = jnp.zeros((CONV_CARRY, d), F32)

    x = x_ref[0]
    if pending:
        x = x + gtp_ref[0] * _weighted_rows(g_ref, gwp_ref[...])
    hb = (_rms(x, n1_ref[...]) * (1.0 + sc_ref[0]) + sh_ref[0]).astype(BF16)

    def proj(col):
        return _dot(hb, win_ref[:, col * d:(col + 1) * d])

    u = jax.nn.gelu(proj(0))
    vb = _rms(jax.nn.gelu(proj(1)), gng_ref[...]).astype(BF16)
    hd = d // GMLP_HEADS
    row = lax.broadcasted_iota(I32, (CHUNK, CHUNK), 0)
    col = lax.broadcasted_iota(I32, (CHUNK, CHUNK), 1)
    for h in range(GMLP_HEADS):
        wm = jnp.where(row >= col, ws_ref[h], 0.0).astype(BF16)
        bias = bst_ref[:, h:h + 1]
        for ci in range(tm // CHUNK):
            blk = vb[ci * CHUNK:(ci + 1) * CHUNK, h * hd:(h + 1) * hd]
            s_ref[ci * CHUNK:(ci + 1) * CHUNK, h * hd:(h + 1) * hd] = _dot(wm, blk) + bias
    ya = _dot((u * s_ref[...]).astype(BF16), wpa_ref[...])
    mix = jax.nn.sigmoid(proj(6)) * ya

    p = proj(2)
    pext_ref[POOL_CARRY:POOL_CARRY + tm, :] = p
    pos1 = (j * tm + 1 + lax.broadcasted_iota(I32, (tm, 1), 0))
    gd = d // len(POOL_WINDOWS)
    yb_parts = []
    for gi, w in enumerate(POOL_WINDOWS):
        cs = slice(gi * gd, (gi + 1) * gd)
        acc = p[:, cs]
        for k in range(1, w):
            acc = acc + pext_ref[POOL_CARRY - k:POOL_CARRY - k + tm, cs]
        cnt = jnp.minimum(pos1, w).astype(F32)
        dg = (acc / cnt - p[:, cs]).astype(BF16)
        yb_parts.append(_dot(dg, pw_ref[gi]))
    yb = jnp.concatenate(yb_parts, axis=1) * ps_ref[...]
    pext_ref[0:POOL_CARRY, :] = pext_ref[tm:tm + POOL_CARRY, :]
    mix = mix + jax.nn.sigmoid(proj(7)) * yb

    z = proj(5) * proj(3)
    zext_ref[CONV_CARRY:CONV_CARRY + tm, :] = z
    conv = cw_ref[CONV_K - 1:CONV_K, :] * z
    for k in range(CONV_K - 1):
        lag = CONV_K - 1 - k
        conv = conv + cw_ref[k:k + 1, :] * zext_ref[CONV_CARRY - lag:CONV_CARRY - lag + tm, :]
    zext_ref[0:CONV_CARRY, :] = zext_ref[tm:tm + CONV_CARRY, :]
    yc = _dot((proj(4) * conv).astype(BF16), wpc_ref[...])
    mix = mix + jax.nn.sigmoid(proj(8)) * yc

    x_new = x + gt_ref[0] * _dot(mix.astype(BF16), wout_ref[...])
    o_ref[0] = x_new
    first = jnp.logical_and(pl.program_id(0) == 0, j == 0)
    _route_tile(x_new, sh2_ref[0], sc2_ref[0], n2_ref[...], rwt_ref[...], rb_ref[...], first,
                hpk_ref, idx_ref, gw_ref, rank_ref, cnt_ref, carry_ref)


def _mixer(x, pending, sh, sc, gt, sh2, sc2, n1, w_in, gng, ws, bs, wpa, pool_w, pool_scale, conv_w, wpc,
           w_out, n2, router_w, router_b):
    bsz, seq, d = x.shape
    n_tok = bsz * seq
    tm = min(MIX_ROWS if pending is None else MIX_ROWS_PENDING, seq)
    nj = seq // tm
    vec = pl.BlockSpec((1, 1, d), lambda b, j: (b, 0, 0))
    tile = pl.BlockSpec((1, tm, d), lambda b, j: (b, j, 0))
    per_tok = pl.BlockSpec((TOP_K, tm), lambda b, j: (0, b * nj + j))
    pending_specs, pending_args = [], []
    if pending is not None:
        g, gw, gt_prev = pending
        pending_specs = [pl.BlockSpec((TOP_K, tm, d // 2), lambda b, j: (0, b * nj + j, 0)),
                         pl.BlockSpec((tm, TOP_K), lambda b, j: (b * nj + j, 0)), vec]
        pending_args = [g, gw.T, gt_prev]
    return pl.pallas_call(
        functools.partial(_mixer_kernel, pending=pending is not None),
        grid=(bsz, nj),
        in_specs=pending_specs + [
            tile, vec, vec, vec, vec, vec,
            _resident((1, d)),
            _resident(w_in.shape),
            _resident((1, d)),
            _resident(ws.shape),
            _resident((CHUNK, GMLP_HEADS)),
            _resident(wpa.shape),
            _resident(pool_w.shape),
            _resident((1, d)),
            _resident(conv_w.shape),
            _resident(wpc.shape),
            _resident(w_out.shape),
            _resident((1, d)), _resident((N_EXPERTS, d)), _resident((N_EXPERTS, 1)),
        ],
        out_specs=[
            tile,
            pl.BlockSpec((tm, d // 2), lambda b, j: (b * nj + j, 0)),
            per_tok, per_tok, per_tok,
            pl.BlockSpec((N_EXPERTS, 128), lambda b, j: (0, 0)),
        ],
        out_shape=[
            jax.ShapeDtypeStruct(x.shape, F32),
            jax.ShapeDtypeStruct((n_tok, d // 2), U32),
            jax.ShapeDtypeStruct((TOP_K, n_tok), I32),
            jax.ShapeDtypeStruct((TOP_K, n_tok), F32),
            jax.ShapeDtypeStruct((TOP_K, n_tok), I32),
            jax.ShapeDtypeStruct((N_EXPERTS, 128), I32),
        ],
        scratch_shapes=[
            pltpu.VMEM((tm, d), F32),
            pltpu.VMEM((POOL_CARRY + tm, d), F32),
            pltpu.VMEM((CONV_CARRY + tm, d), F32),
            pltpu.VMEM((N_EXPERTS, 128), F32),
        ],
        compiler_params=_params(2),
        name="mixer",
    )(*pending_args, x, sh, sc, gt, sh2, sc2, n1.reshape(1, d), w_in.astype(BF16), gng.reshape(1, d), ws, bs.T,
      wpa.astype(BF16), pool_w.astype(BF16), pool_scale.reshape(1, d), conv_w, wpc.astype(BF16),
      w_out.astype(BF16), n2.reshape(1, d), router_w.T.astype(BF16), router_b.reshape(N_EXPERTS, 1))


def _route_tile(x, sh, sc, n2, rwt, rb, first, hpk_ref, idx_ref, gw_ref, rank_ref, cnt_ref, carry_ref):
    tm = x.shape[0]

    @pl.when(first)
    def _():
        carry_ref[...] = jnp.zeros(carry_ref.shape, F32)

    h = _rms(x, n2) * (1.0 + sc) + sh
    hb = h.astype(BF16)
    hpk_ref[...] = _pack_bf16_pairs(h)

    logits = lax.dot_general(rwt, hb, (((1,), (1,)), ((), ())), preferred_element_type=F32) + rb
    iota_e = lax.broadcasted_iota(I32, logits.shape, 0)
    vals, idxs, sels = [], [], []
    rest = logits
    for _ in range(TOP_K):
        m = jnp.max(rest, axis=0, keepdims=True)
        ik = jnp.min(jnp.where(rest == m, iota_e, N_EXPERTS), axis=0, keepdims=True)
        sel = iota_e == ik
        rest = jnp.where(sel, -jnp.inf, rest)
        vals.append(m)
        idxs.append(ik)
        sels.append(sel)
    exps = [jnp.exp(v - vals[0]) for v in vals]
    denom = exps[0] + exps[1] + exps[2] + exps[3]

    chosen = jnp.logical_or(jnp.logical_or(sels[0], sels[1]), jnp.logical_or(sels[2], sels[3]))
    a = jnp.where(chosen, 1.0, 0.0)
    before = lax.broadcasted_iota(I32, (tm, tm), 0) < lax.broadcasted_iota(I32, (tm, tm), 1)
    prior = _dot(a.astype(BF16), jnp.where(before, 1.0, 0.0).astype(BF16)) + carry_ref[:, 0:1]
    for k in range(TOP_K):
        idx_ref[k:k + 1, :] = idxs[k]
        gw_ref[k:k + 1, :] = exps[k] / denom
        rank_ref[k:k + 1, :] = jnp.sum(jnp.where(sels[k], prior, 0.0), axis=0, keepdims=True).astype(I32)
    total = carry_ref[...] + jnp.sum(a, axis=1, keepdims=True)
    carry_ref[...] = total
    cnt_ref[...] = total.astype(I32)


def _dest_kernel(start_ref, idx_ref, rank_ref, o_ref):
    idx = idx_ref[...]
    base = jnp.zeros(idx.shape, I32)
    for e in range(N_EXPERTS):
        base = jnp.where(idx == e, start_ref[e], base)
    o_ref[...] = base + rank_ref[...]


def _dest(pad_start, idx, rank):
    n_tok = idx.shape[1]
    tc = min(DEST_COLS, n_tok)
    blk = pl.BlockSpec((TOP_K, tc), lambda i, s: (0, i))
    return pl.pallas_call(
        _dest_kernel,
        grid_spec=pltpu.PrefetchScalarGridSpec(
            num_scalar_prefetch=1, grid=(n_tok // tc,), in_specs=[blk, blk], out_specs=blk),
        out_shape=jax.ShapeDtypeStruct(idx.shape, I32),
        compiler_params=_params(1),
        name="dest",
    )(pad_start, idx, rank)


def _dispatch(hpk, dest, n_rows):
    n_tok, half = hpk.shape
    win = SC_WINDOW
    assert n_tok % (win * SC_WORKERS) == 0
    per_worker = n_tok // win // SC_WORKERS
    mesh = plsc.VectorSubcoreMesh(core_axis_name="c", subcore_axis_name="s")

    @pl.kernel(out_type=jax.ShapeDtypeStruct((n_rows, half), U32), mesh=mesh,
               scratch_types=[pltpu.VMEM((win, half), U32), pltpu.VMEM((TOP_K, win), I32)])
    def scatter_rows(hpk_hbm, dest_hbm, xs_hbm, xbuf, ibuf):
        worker = lax.axis_index("c") * SC_SUBCORES + lax.axis_index("s")

        @pl.loop(0, per_worker)
        def _(j):
            t0 = pl.multiple_of((worker * per_worker + j) * win, win)
            pltpu.sync_copy(hpk_hbm.at[pl.ds(t0, win)], xbuf)
            pltpu.sync_copy(dest_hbm.at[:, pl.ds(t0, win)], ibuf)
            for k in range(TOP_K):
                pltpu.sync_copy(xbuf, xs_hbm.at[ibuf.at[k]])

    return scatter_rows(hpk, dest)


def _expert_kernel(be_ref, nv_ref, live_ref, first_ref, slot_ref, next_ref, xs_ref, wgu_hbm, bgu_ref, wd_hbm,
                   bd_ref, ys_ref, wgu_f32, wd_f32, wgu_bf, wd_bf, sem, *, layer):
    step = pl.program_id(0)
    tb = EXPERT_ROWS

    def weight_copies(expert, slot):
        return (pltpu.make_async_copy(wgu_hbm.at[layer, expert], wgu_f32.at[slot], sem.at[0, slot]),
                pltpu.make_async_copy(wd_hbm.at[layer, expert], wd_f32.at[slot], sem.at[1, slot]))

    @pl.when(step == 0)
    def _():
        for copy in weight_copies(be_ref[0], 0):
            copy.start()

    for sub in range(xs_ref.shape[0] // tb):
        i = step * (xs_ref.shape[0] // tb) + sub
        rows = slice(sub * tb, (sub + 1) * tb)

        @pl.when(i >= nv_ref[0])
        def _():
            ys_ref[rows, :] = jnp.zeros((tb, ys_ref.shape[1]), U32)

        @pl.when(first_ref[i] == 1)
        def _():
            slot = slot_ref[i]

            @pl.when(next_ref[i] >= 0)
            def _():
                for copy in weight_copies(next_ref[i], 1 - slot):
                    copy.start()

            for copy in weight_copies(be_ref[i], slot):
                copy.wait()
            wgu_bf[...] = wgu_f32[slot].astype(BF16)
            wd_bf[...] = wd_f32[slot].astype(BF16)

        @pl.when(i < nv_ref[0])
        def _():
            live = lax.broadcasted_iota(I32, (tb, 1), 0) < live_ref[i]
            xb = jnp.where(live, _unpack_bf16_pairs(xs_ref[rows, :]), 0.0).astype(BF16)
            gu = _dot(xb, wgu_bf[...]) + bgu_ref[layer, be_ref[i]]
            ff = gu.shape[1] // 2
            gate = jnp.minimum(gu[:, :ff], SWIGLU_LIMIT)
            up = jnp.clip(gu[:, ff:], -SWIGLU_LIMIT, SWIGLU_LIMIT)
            glu = gate * jax.nn.sigmoid(SWIGLU_ALPHA * gate)
            y = _dot(((up + 1.0) * glu).astype(BF16), wd_bf[...]) + bd_ref[layer, be_ref[i]]
            ys_ref[rows, :] = _pack_bf16_pairs(y)


def _experts(layer, plan, xs, w_gu, b_gu, w_down, b_down):
    n_rows, half = xs.shape
    depth, _, d, two_f = w_gu.shape
    step_rows = EXPERT_ROWS * EXPERT_BLOCKS_PER_STEP

    def rows(i, be, nv, *_):
        return (jnp.minimum(i, (nv[0] - 1) // EXPERT_BLOCKS_PER_STEP), 0)

    return pl.pallas_call(
        functools.partial(_expert_kernel, layer=layer),
        grid_spec=pltpu.PrefetchScalarGridSpec(
            num_scalar_prefetch=len(plan),
            grid=(n_rows // step_rows,),
            in_specs=[
                pl.BlockSpec((step_rows, half), rows),
                pl.BlockSpec(memory_space=pl.ANY),
                _resident((depth, N_EXPERTS, 1, two_f)),
                pl.BlockSpec(memory_space=pl.ANY),
                _resident((depth, N_EXPERTS, 1, d)),
            ],
            out_specs=pl.BlockSpec((step_rows, half), lambda i, *_: (i, 0)),
            scratch_shapes=[
                pltpu.VMEM((2, d, two_f), F32), pltpu.VMEM((2, two_f // 2, d), F32),
                pltpu.VMEM((d, two_f), BF16), pltpu.VMEM((two_f // 2, d), BF16),
                pltpu.SemaphoreType.DMA((2, 2)),
            ],
        ),
        out_shape=jax.ShapeDtypeStruct((n_rows, half), U32),
        compiler_params=_params(1),
        name="experts",
    )(*plan, xs, w_gu, b_gu.reshape(depth, N_EXPERTS, 1, two_f), w_down, b_down.reshape(depth, N_EXPERTS, 1, d))


def _gather_rows(dest, ys):
    n_tok = dest.shape[1]
    width = ys.shape[1]
    win = SC_WINDOW
    assert n_tok % (win * SC_WORKERS) == 0
    per_worker = n_tok // win // SC_WORKERS
    mesh = plsc.VectorSubcoreMesh(core_axis_name="c", subcore_axis_name="s")

    @pl.kernel(out_type=jax.ShapeDtypeStruct((TOP_K, n_tok, width), ys.dtype), mesh=mesh,
               scratch_types=[pltpu.VMEM((win, width), ys.dtype), pltpu.VMEM((TOP_K, win), I32)])
    def gather(ys_hbm, idx_hbm, g_hbm, buf, ibuf):
        worker = lax.axis_index("c") * SC_SUBCORES + lax.axis_index("s")

        @pl.loop(0, per_worker)
        def _(j):
            t0 = pl.multiple_of((worker * per_worker + j) * win, win)
            pltpu.sync_copy(idx_hbm.at[:, pl.ds(t0, win)], ibuf)
            for k in range(TOP_K):
                pltpu.sync_copy(ys_hbm.at[ibuf.at[k]], buf)
                pltpu.sync_copy(buf, g_hbm.at[k, pl.ds(t0, win)])

    return gather(ys, dest)


def _final_kernel(g_ref, x_ref, gw_ref, gt_ref, fg_ref, o_ref):
    y = x_ref[...] + gt_ref[0] * _weighted_rows(g_ref, gw_ref[...])
    o_ref[...] = _rms(y, fg_ref[...])


def _final(g, x, gw, gt, final_g):
    bsz, seq, d = x.shape
    n_tok = bsz * seq
    tm = min(FINAL_ROWS, seq)
    per_seq = seq // tm
    out = pl.pallas_call(
        _final_kernel,
        grid=(n_tok // tm,),
        in_specs=[
            pl.BlockSpec((TOP_K, tm, d // 2), lambda i: (0, i, 0)),
            pl.BlockSpec((tm, d), lambda i: (i, 0)),
            pl.BlockSpec((tm, TOP_K), lambda i: (i, 0)),
            pl.BlockSpec((1, 1, d), lambda i: (i // per_seq, 0, 0)),
            pl.BlockSpec((1, d), lambda i: (0, 0)),
        ],
        out_specs=pl.BlockSpec((tm, d), lambda i: (i, 0)),
        out_shape=jax.ShapeDtypeStruct((n_tok, d), F32),
        compiler_params=_params(1),
        name="final",
    )(g, x.reshape(n_tok, d), gw.T, gt, final_g.reshape(1, d))
    return out.reshape(bsz, seq, d)


def _moe(layer, routed, w_gu, b_gu, w_down, b_down):
    n_tok = routed[0].shape[0]
    tb = EXPERT_ROWS
    step_rows = tb * EXPERT_BLOCKS_PER_STEP
    n_rows = -(-(n_tok * TOP_K + N_EXPERTS * (tb - 1)) // step_rows) * step_rows
    hpk, idx, gw, rank, cnt = routed
    counts = cnt[:, 0]
    padded = (counts + tb - 1) // tb * tb
    pad_end = jnp.cumsum(padded)
    pad_start = pad_end - padded
    n_valid = pad_end[-1] // tb
    blocks = jnp.arange(n_rows // tb, dtype=I32)
    block_e = jnp.sum((pad_end[None, :] <= (blocks * tb)[:, None]).astype(I32), axis=1)
    block_e = jnp.minimum(block_e, N_EXPERTS - 1)
    block_e = jnp.where(blocks < n_valid, block_e, block_e[n_valid - 1])
    live_rows = jnp.clip(counts[block_e] - (blocks * tb - pad_start[block_e]), 0, tb)
    live_rows = jnp.where(blocks < n_valid, live_rows, 0).astype(I32)
    first = jnp.logical_and(blocks < n_valid, blocks * tb == pad_start[block_e])
    slot = (jnp.cumsum(first.astype(I32)) - 1) % 2
    experts = jnp.arange(N_EXPERTS, dtype=I32)
    used_from = lax.cummin(jnp.where(counts > 0, experts, N_EXPERTS), reverse=True)
    next_used = jnp.concatenate([used_from[1:], jnp.full((1,), N_EXPERTS, I32)])
    next_e = jnp.where(next_used < N_EXPERTS, next_used, -1)[block_e]
    plan = (block_e, n_valid.reshape(1), live_rows, first.astype(I32), slot.astype(I32), next_e.astype(I32))
    dest = _dest(pad_start, idx, rank)
    xs = _dispatch(hpk, dest, n_rows)
    ys = _experts(layer, plan, xs, w_gu, b_gu, w_down, b_down)
    return _gather_rows(dest, ys)


def kernel(x, c, norm1_g, ada_w, ada_b, w_in, gmlp_norm_g, gmlp_ws, gmlp_bs, w_proj_a, pool_w, pool_scale,
           conv_w, w_proj_c, w_out, norm2_g, router_w, router_b, exp_w_gu, exp_b_gu, exp_w_down,
           exp_b_down, final_g):
    depth = ada_w.shape[0]
    bsz, _, d = x.shape
    mods = _ada(c, ada_w, ada_b)
    pending = None
    for l in range(depth):
        sh1, sc1, g1, sh2, sc2, g2 = [mods[l, :, i * d:(i + 1) * d].reshape(bsz, 1, d) for i in range(6)]
        x, *routed = _mixer(x, pending, sh1, sc1, g1, sh2, sc2, norm1_g[l], w_in[l], gmlp_norm_g[l],
                            gmlp_ws[l], gmlp_bs[l], w_proj_a[l], pool_w[l], pool_scale[l], conv_w[l],
                            w_proj_c[l], w_out[l], norm2_g[l], router_w[l], router_b[l])
        gate_w = routed[2]
        pending = (_moe(l, routed, exp_w_gu, exp_b_gu, exp_w_down, exp_b_down), gate_w, g2)
    g, gate_w, g2 = pending
    return _final(g, x, gate_w, g2, final_g)
```

```python
import functools

import jax
import jax.numpy as jnp
from jax import lax
from jax.experimental import pallas as pl
from jax.experimental.pallas import tpu as pltpu
from jax.experimental.pallas import tpu_sc as plsc

F32 = jnp.float32
BF16 = jnp.bfloat16
I32 = jnp.int32
U32 = jnp.uint32

RMS_EPS = 1e-5
GMLP_HEADS = 8
CHUNK = 128
POOL_WINDOWS = (2, 4, 8, 16)
POOL_CARRY = 16
CONV_K = 3
CONV_CARRY = 8
N_EXPERTS = 32
TOP_K = 4
SWIGLU_LIMIT = 7.0
SWIGLU_ALPHA = 1.702

MIX_ROWS = 512
MIX_ROWS_PENDING = 512
MIX_SUB_ROWS = 256
DEST_COLS = 4096
EXPERT_ROWS = 512
EXPERT_BLOCKS_PER_STEP = 2
FINAL_ROWS = 512
SC_SUBCORES = 16
SC_WORKERS = 2 * SC_SUBCORES
SC_WINDOW = 128

VMEM_LIMIT = 58 * 1024 * 1024


def _rms(x, g):
    return x * lax.rsqrt(jnp.mean(x * x, axis=-1, keepdims=True) + RMS_EPS) * g


def _dot(a, b):
    return jnp.dot(a, b, preferred_element_type=F32)


def _params(n_axes, vmem=VMEM_LIMIT):
    return pltpu.CompilerParams(dimension_semantics=("arbitrary",) * n_axes, vmem_limit_bytes=vmem)


def _pack_bf16_pairs(x):
    half = x.shape[1] // 2
    bits = lax.bitcast_convert_type(x.astype(BF16).astype(F32), U32)
    return (bits[:, :half] >> 16) | bits[:, half:]


def _unpack_bf16_pairs(w):
    lo = lax.bitcast_convert_type(w << 16, F32)
    hi = lax.bitcast_convert_type(w & jnp.uint32(0xFFFF0000), F32)
    return jnp.concatenate([lo, hi], axis=1)


def _resident(shape):
    zeros = (0,) * len(shape)
    return pl.BlockSpec(shape, lambda *_: zeros, pipeline_mode=pl.Buffered(1))


def _ada_kernel(c_ref, w_ref, b_ref, o_ref):
    c = c_ref[...]
    cond = c * jax.nn.sigmoid(c)
    o_ref[0] = _dot(cond.astype(BF16), w_ref[0].astype(BF16)) + b_ref[0]


def _ada(c, ada_w, ada_b):
    depth, d, six_d = ada_w.shape
    bsz = c.shape[0]
    return pl.pallas_call(
        _ada_kernel,
        grid=(depth, six_d // d),
        in_specs=[
            pl.BlockSpec((bsz, d), lambda l, j: (0, 0)),
            pl.BlockSpec((1, d, d), lambda l, j: (l, 0, j)),
            pl.BlockSpec((1, 1, d), lambda l, j: (l, 0, j)),
        ],
        out_specs=pl.BlockSpec((1, bsz, d), lambda l, j: (l, 0, j)),
        out_shape=jax.ShapeDtypeStruct((depth, bsz, six_d), F32),
        compiler_params=_params(2),
        name="ada",
    )(c, ada_w, ada_b.reshape(depth, 1, six_d))


def _weighted_rows(g_ref, gw):
    acc = gw[:, 0:1] * _unpack_bf16_pairs(g_ref[0])
    for k in range(1, TOP_K):
        acc = acc + gw[:, k:k + 1] * _unpack_bf16_pairs(g_ref[k])
    return acc


def _mixer_kernel(*refs, pending):
    if pending:
        g_ref, gwp_ref, gtp_ref, *refs = refs
    (x_ref, sh_ref, sc_ref, gt_ref, sh2_ref, sc2_ref, n1_ref, win_ref, gng_ref, ws_ref, bst_ref,
     wpa_ref, pw_ref, ps_ref, cw_ref, wpc_ref, wout_ref, n2_ref, rwt_ref, rb_ref,
     o_ref, hpk_ref, idx_ref, gw_ref, rank_ref, cnt_ref,
     s_ref, pext_ref, zext_ref, carry_ref) = refs
    tm, d = x_ref.shape[1], x_ref.shape[2]
    j = pl.program_id(1)

    @pl.when(j == 0)
    def _():
        pext_ref[0:POOL_CARRY, :] = jnp.zeros((POOL_CARRY, d), F32)
        zext_ref[0:CONV_CARRY, :] = jnp.zeros((CONV_CARRY, d), F32)

    @pl.when(jnp.logical_and(pl.program_id(0) == 0, j == 0))
    def _():
        carry_ref[...] = jnp.zeros(carry_ref.shape, F32)

    hd = d // GMLP_HEADS
    gd = d // len(POOL_WINDOWS)
    row = lax.broadcasted_iota(I32, (CHUNK, CHUNK), 0)
    col = lax.broadcasted_iota(I32, (CHUNK, CHUNK), 1)
    spatial_w = [jnp.where(row >= col, ws_ref[h], 0.0).astype(BF16) for h in range(GMLP_HEADS)]

    fine = not pending

    def rows_block(r0, n):
        rs = slice(r0, r0 + n)
        x = x_ref[0, rs, :]
        if pending:
            x = x + gtp_ref[0] * _weighted_rows(g_ref.at[:, rs, :], gwp_ref[rs, :])
        hb = (_rms(x, n1_ref[...]) * (1.0 + sc_ref[0]) + sh_ref[0]).astype(BF16)

        def proj(c):
            return _dot(hb, win_ref[:, c * d:(c + 1) * d])

        u_raw = proj(0)
        if fine:
            yield
        v_raw = proj(1)
        yield
        u = jax.nn.gelu(u_raw)
        if fine:
            yield
        vb = _rms(jax.nn.gelu(v_raw), gng_ref[...]).astype(BF16)
        yield
        for h in range(GMLP_HEADS):
            bias = bst_ref[:, h:h + 1]
            for ci in range(n // CHUNK):
                blk = vb[ci * CHUNK:(ci + 1) * CHUNK, h * hd:(h + 1) * hd]
                s_ref[r0 + ci * CHUNK:r0 + (ci + 1) * CHUNK, h * hd:(h + 1) * hd] = _dot(spatial_w[h], blk) + bias
        if fine:
            yield
        gate_a = proj(6)
        yield
        ya = _dot((u * s_ref[rs, :]).astype(BF16), wpa_ref[...])
        if fine:
            yield
        mix = jax.nn.sigmoid(gate_a) * ya

        p = proj(2)
        if fine:
            yield
        gate_b = proj(7)
        yield
        p0 = POOL_CARRY + r0
        pext_ref[p0:p0 + n, :] = p
        pos1 = (j * tm + r0 + 1 + lax.broadcasted_iota(I32, (n, 1), 0))
        diffs = []
        for gi, w in enumerate(POOL_WINDOWS):
            cs = slice(gi * gd, (gi + 1) * gd)
            acc = p[:, cs]
            for k in range(1, w):
                acc = acc + pext_ref[p0 - k:p0 - k + n, cs]
            cnt = jnp.minimum(pos1, w).astype(F32)
            diffs.append((acc / cnt - p[:, cs]).astype(BF16))
            if fine:
                yield
        if not fine:
            yield
        yb = jnp.concatenate([_dot(dg, pw_ref[gi]) for gi, dg in enumerate(diffs)], axis=1) * ps_ref[...]
        mix = mix + jax.nn.sigmoid(gate_b) * yb

        conv_c = proj(5)
        if fine:
            yield
        conv_x = proj(3)
        if fine:
            yield
        conv_b = proj(4)
        if fine:
            yield
        gate_c = proj(8)
        yield
        z = conv_c * conv_x
        z0 = CONV_CARRY + r0
        zext_ref[z0:z0 + n, :] = z
        conv = cw_ref[CONV_K - 1:CONV_K, :] * z
        for k in range(CONV_K - 1):
            lag = CONV_K - 1 - k
            conv = conv + cw_ref[k:k + 1, :] * zext_ref[z0 - lag:z0 - lag + n, :]
        gated = (conv_b * conv).astype(BF16)
        yield
        yc = _dot(gated, wpc_ref[...])
        if fine:
            yield
        mix = mix + jax.nn.sigmoid(gate_c) * yc

        x_new = x + gt_ref[0] * _dot(mix.astype(BF16), wout_ref[...])
        o_ref[0, rs, :] = x_new
        _route_tile(x_new, sh2_ref[0], sc2_ref[0], n2_ref[...], rwt_ref[...], rb_ref[...],
                    hpk_ref.at[rs, :], idx_ref.at[:, rs], gw_ref.at[:, rs], rank_ref.at[:, rs], cnt_ref, carry_ref)

    sub = min(MIX_SUB_ROWS, tm)
    waiting = [rows_block(r0, sub) for r0 in range(0, tm, sub)]
    running = []
    while waiting or running:
        if waiting:
            running.append(waiting.pop(0))
        for block in list(running):
            if next(block, "done") == "done":
                running.remove(block)
    pext_ref[0:POOL_CARRY, :] = pext_ref[tm:tm + POOL_CARRY, :]
    zext_ref[0:CONV_CARRY, :] = zext_ref[tm:tm + CONV_CARRY, :]


def _mixer(x, pending, sh, sc, gt, sh2, sc2, n1, w_in, gng, ws, bs, wpa, pool_w, pool_scale, conv_w, wpc,
           w_out, n2, router_w, router_b):
    bsz, seq, d = x.shape
    n_tok = bsz * seq
    tm = min(MIX_ROWS if pending is None else MIX_ROWS_PENDING, seq)
    nj = seq // tm
    vec = pl.BlockSpec((1, 1, d), lambda b, j: (b, 0, 0))
    tile = pl.BlockSpec((1, tm, d), lambda b, j: (b, j, 0))
    per_tok = pl.BlockSpec((TOP_K, tm), lambda b, j: (0, b * nj + j))
    pending_specs, pending_args = [], []
    if pending is not None:
        g, gw, gt_prev = pending
        pending_specs = [pl.BlockSpec((TOP_K, tm, d // 2), lambda b, j: (0, b * nj + j, 0)),
                         pl.BlockSpec((tm, TOP_K), lambda b, j: (b * nj + j, 0)), vec]
        pending_args = [g, gw.T, gt_prev]
    return pl.pallas_call(
        functools.partial(_mixer_kernel, pending=pending is not None),
        grid=(bsz, nj),
        in_specs=pending_specs + [
            tile, vec, vec, vec, vec, vec,
            _resident((1, d)),
            _resident(w_in.shape),
            _resident((1, d)),
            _resident(ws.shape),
            _resident((CHUNK, GMLP_HEADS)),
            _resident(wpa.shape),
            _resident(pool_w.shape),
            _resident((1, d)),
            _resident(conv_w.shape),
            _resident(wpc.shape),
            _resident(w_out.shape),
            _resident((1, d)), _resident((N_EXPERTS, d)), _resident((N_EXPERTS, 1)),
        ],
        out_specs=[
            tile,
            pl.BlockSpec((tm, d // 2), lambda b, j: (b * nj + j, 0)),
            per_tok, per_tok, per_tok,
            pl.BlockSpec((N_EXPERTS, 128), lambda b, j: (0, 0)),
        ],
        out_shape=[
            jax.ShapeDtypeStruct(x.shape, F32),
            jax.ShapeDtypeStruct((n_tok, d // 2), U32),
            jax.ShapeDtypeStruct((TOP_K, n_tok), I32),
            jax.ShapeDtypeStruct((TOP_K, n_tok), F32),
            jax.ShapeDtypeStruct((TOP_K, n_tok), I32),
            jax.ShapeDtypeStruct((N_EXPERTS, 128), I32),
        ],
        scratch_shapes=[
            pltpu.VMEM((tm, d), F32),
            pltpu.VMEM((POOL_CARRY + tm, d), F32),
            pltpu.VMEM((CONV_CARRY + tm, d), F32),
            pltpu.VMEM((N_EXPERTS, 128), F32),
        ],
        compiler_params=_params(2),
        name="mixer",
    )(*pending_args, x, sh, sc, gt, sh2, sc2, n1.reshape(1, d), w_in.astype(BF16), gng.reshape(1, d), ws, bs.T,
      wpa.astype(BF16), pool_w.astype(BF16), pool_scale.reshape(1, d), conv_w, wpc.astype(BF16),
      w_out.astype(BF16), n2.reshape(1, d), router_w.T.astype(BF16), router_b.reshape(N_EXPERTS, 1))


def _route_tile(x, sh, sc, n2, rwt, rb, hpk_ref, idx_ref, gw_ref, rank_ref, cnt_ref, carry_ref):
    tm = x.shape[0]

    h = _rms(x, n2) * (1.0 + sc) + sh
    hb = h.astype(BF16)
    hpk_ref[...] = _pack_bf16_pairs(h)

    logits = lax.dot_general(rwt, hb, (((1,), (1,)), ((), ())), preferred_element_type=F32) + rb
    iota_e = lax.broadcasted_iota(I32, logits.shape, 0)
    vals, idxs, sels = [], [], []
    rest = logits
    for _ in range(TOP_K):
        m = jnp.max(rest, axis=0, keepdims=True)
        ik = jnp.min(jnp.where(rest == m, iota_e, N_EXPERTS), axis=0, keepdims=True)
        sel = iota_e == ik
        rest = jnp.where(sel, -jnp.inf, rest)
        vals.append(m)
        idxs.append(ik)
        sels.append(sel)
    exps = [jnp.exp(v - vals[0]) for v in vals]
    denom = exps[0] + exps[1] + exps[2] + exps[3]

    chosen = jnp.logical_or(jnp.logical_or(sels[0], sels[1]), jnp.logical_or(sels[2], sels[3]))
    a = jnp.where(chosen, 1.0, 0.0)
    before = lax.broadcasted_iota(I32, (tm, tm), 0) < lax.broadcasted_iota(I32, (tm, tm), 1)
    prior = _dot(a.astype(BF16), jnp.where(before, 1.0, 0.0).astype(BF16)) + carry_ref[:, 0:1]
    for k in range(TOP_K):
        idx_ref[k:k + 1, :] = idxs[k]
        gw_ref[k:k + 1, :] = exps[k] / denom
        rank_ref[k:k + 1, :] = jnp.sum(jnp.where(sels[k], prior, 0.0), axis=0, keepdims=True).astype(I32)
    total = carry_ref[...] + jnp.sum(a, axis=1, keepdims=True)
    carry_ref[...] = total
    cnt_ref[...] = total.astype(I32)


def _dest_kernel(start_ref, idx_ref, rank_ref, o_ref):
    idx = idx_ref[...]
    base = jnp.zeros(idx.shape, I32)
    for e in range(N_EXPERTS):
        base = jnp.where(idx == e, start_ref[e], base)
    o_ref[...] = base + rank_ref[...]


def _dest(pad_start, idx, rank):
    n_tok = idx.shape[1]
    tc = min(DEST_COLS, n_tok)
    blk = pl.BlockSpec((TOP_K, tc), lambda i, s: (0, i))
    return pl.pallas_call(
        _dest_kernel,
        grid_spec=pltpu.PrefetchScalarGridSpec(
            num_scalar_prefetch=1, grid=(n_tok // tc,), in_specs=[blk, blk], out_specs=blk),
        out_shape=jax.ShapeDtypeStruct(idx.shape, I32),
        compiler_params=_params(1),
        name="dest",
    )(pad_start, idx, rank)


def _dispatch(hpk, dest, n_rows):
    n_tok, half = hpk.shape
    win = SC_WINDOW
    assert n_tok % (win * SC_WORKERS) == 0
    per_worker = n_tok // win // SC_WORKERS
    mesh = plsc.VectorSubcoreMesh(core_axis_name="c", subcore_axis_name="s")

    @pl.kernel(out_type=jax.ShapeDtypeStruct((n_rows, half), U32), mesh=mesh,
               scratch_types=[pltpu.VMEM((win, half), U32), pltpu.VMEM((TOP_K, win), I32)])
    def scatter_rows(hpk_hbm, dest_hbm, xs_hbm, xbuf, ibuf):
        worker = lax.axis_index("c") * SC_SUBCORES + lax.axis_index("s")

        @pl.loop(0, per_worker)
        def _(j):
            t0 = pl.multiple_of((worker * per_worker + j) * win, win)
            pltpu.sync_copy(hpk_hbm.at[pl.ds(t0, win)], xbuf)
            pltpu.sync_copy(dest_hbm.at[:, pl.ds(t0, win)], ibuf)
            for k in range(TOP_K):
                pltpu.sync_copy(xbuf, xs_hbm.at[ibuf.at[k]])

    return scatter_rows(hpk, dest)


def _expert_kernel(be_ref, nv_ref, live_ref, first_ref, slot_ref, next_ref, xs_ref, wgu_hbm, bgu_ref, wd_hbm,
                   bd_ref, ys_ref, wgu_f32, wd_f32, wgu_bf, wd_bf, sem, *, layer):
    step = pl.program_id(0)
    tb = EXPERT_ROWS

    def weight_copies(expert, slot):
        return (pltpu.make_async_copy(wgu_hbm.at[layer, expert], wgu_f32.at[slot], sem.at[0, slot]),
                pltpu.make_async_copy(wd_hbm.at[layer, expert], wd_f32.at[slot], sem.at[1, slot]))

    @pl.when(step == 0)
    def _():
        for copy in weight_copies(be_ref[0], 0):
            copy.start()

    for sub in range(xs_ref.shape[0] // tb):
        i = step * (xs_ref.shape[0] // tb) + sub
        rows = slice(sub * tb, (sub + 1) * tb)

        @pl.when(i >= nv_ref[0])
        def _():
            ys_ref[rows, :] = jnp.zeros((tb, ys_ref.shape[1]), U32)

        @pl.when(first_ref[i] == 1)
        def _():
            slot = slot_ref[i]

            @pl.when(next_ref[i] >= 0)
            def _():
                for copy in weight_copies(next_ref[i], 1 - slot):
                    copy.start()

            for copy in weight_copies(be_ref[i], slot):
                copy.wait()
            wgu_bf[...] = wgu_f32[slot].astype(BF16)
            wd_bf[...] = wd_f32[slot].astype(BF16)

        @pl.when(i < nv_ref[0])
        def _():
            live = lax.broadcasted_iota(I32, (tb, 1), 0) < live_ref[i]
            xb = jnp.where(live, _unpack_bf16_pairs(xs_ref[rows, :]), 0.0).astype(BF16)
            gu = _dot(xb, wgu_bf[...]) + bgu_ref[layer, be_ref[i]]
            ff = gu.shape[1] // 2
            gate = jnp.minimum(gu[:, :ff], SWIGLU_LIMIT)
            up = jnp.clip(gu[:, ff:], -SWIGLU_LIMIT, SWIGLU_LIMIT)
            glu = gate * jax.nn.sigmoid(SWIGLU_ALPHA * gate)
            y = _dot(((up + 1.0) * glu).astype(BF16), wd_bf[...]) + bd_ref[layer, be_ref[i]]
            ys_ref[rows, :] = _pack_bf16_pairs(y)


def _experts(layer, plan, xs, w_gu, b_gu, w_down, b_down):
    n_rows, half = xs.shape
    depth, _, d, two_f = w_gu.shape
    step_rows = EXPERT_ROWS * EXPERT_BLOCKS_PER_STEP

    def rows(i, be, nv, *_):
        return (jnp.minimum(i, (nv[0] - 1) // EXPERT_BLOCKS_PER_STEP), 0)

    return pl.pallas_call(
        functools.partial(_expert_kernel, layer=layer),
        grid_spec=pltpu.PrefetchScalarGridSpec(
            num_scalar_prefetch=len(plan),
            grid=(n_rows // step_rows,),
            in_specs=[
                pl.BlockSpec((step_rows, half), rows),
                pl.BlockSpec(memory_space=pl.ANY),
                _resident((depth, N_EXPERTS, 1, two_f)),
                pl.BlockSpec(memory_space=pl.ANY),
                _resident((depth, N_EXPERTS, 1, d)),
            ],
            out_specs=pl.BlockSpec((step_rows, half), lambda i, *_: (i, 0)),
            scratch_shapes=[
                pltpu.VMEM((2, d, two_f), F32), pltpu.VMEM((2, two_f // 2, d), F32),
                pltpu.VMEM((d, two_f), BF16), pltpu.VMEM((two_f // 2, d), BF16),
                pltpu.SemaphoreType.DMA((2, 2)),
            ],
        ),
        out_shape=jax.ShapeDtypeStruct((n_rows, half), U32),
        compiler_params=_params(1),
        name="experts",
    )(*plan, xs, w_gu, b_gu.reshape(depth, N_EXPERTS, 1, two_f), w_down, b_down.reshape(depth, N_EXPERTS, 1, d))


def _gather_rows(dest, ys):
    n_tok = dest.shape[1]
    width = ys.shape[1]
    win = SC_WINDOW
    assert n_tok % (win * SC_WORKERS) == 0
    per_worker = n_tok // win // SC_WORKERS
    mesh = plsc.VectorSubcoreMesh(core_axis_name="c", subcore_axis_name="s")

    @pl.kernel(out_type=jax.ShapeDtypeStruct((TOP_K, n_tok, width), ys.dtype), mesh=mesh,
               scratch_types=[pltpu.VMEM((win, width), ys.dtype), pltpu.VMEM((TOP_K, win), I32)])
    def gather(ys_hbm, idx_hbm, g_hbm, buf, ibuf):
        worker = lax.axis_index("c") * SC_SUBCORES + lax.axis_index("s")

        @pl.loop(0, per_worker)
        def _(j):
            t0 = pl.multiple_of((worker * per_worker + j) * win, win)
            pltpu.sync_copy(idx_hbm.at[:, pl.ds(t0, win)], ibuf)
            for k in range(TOP_K):
                pltpu.sync_copy(ys_hbm.at[ibuf.at[k]], buf)
                pltpu.sync_copy(buf, g_hbm.at[k, pl.ds(t0, win)])

    return gather(ys, dest)


def _final_kernel(g_ref, x_ref, gw_ref, gt_ref, fg_ref, o_ref):
    y = x_ref[...] + gt_ref[0] * _weighted_rows(g_ref, gw_ref[...])
    o_ref[...] = _rms(y, fg_ref[...])


def _final(g, x, gw, gt, final_g):
    bsz, seq, d = x.shape
    n_tok = bsz * seq
    tm = min(FINAL_ROWS, seq)
    per_seq = seq // tm
    out = pl.pallas_call(
        _final_kernel,
        grid=(n_tok // tm,),
        in_specs=[
            pl.BlockSpec((TOP_K, tm, d // 2), lambda i: (0, i, 0)),
            pl.BlockSpec((tm, d), lambda i: (i, 0)),
            pl.BlockSpec((tm, TOP_K), lambda i: (i, 0)),
            pl.BlockSpec((1, 1, d), lambda i: (i // per_seq, 0, 0)),
            pl.BlockSpec((1, d), lambda i: (0, 0)),
        ],
        out_specs=pl.BlockSpec((tm, d), lambda i: (i, 0)),
        out_shape=jax.ShapeDtypeStruct((n_tok, d), F32),
        compiler_params=_params(1),
        name="final",
    )(g, x.reshape(n_tok, d), gw.T, gt, final_g.reshape(1, d))
    return out.reshape(bsz, seq, d)


def _moe(layer, routed, w_gu, b_gu, w_down, b_down):
    n_tok = routed[0].shape[0]
    tb = EXPERT_ROWS
    step_rows = tb * EXPERT_BLOCKS_PER_STEP
    n_rows = -(-(n_tok * TOP_K + N_EXPERTS * (tb - 1)) // step_rows) * step_rows
    hpk, idx, gw, rank, cnt = routed
    counts = cnt[:, 0]
    padded = (counts + tb - 1) // tb * tb
    pad_end = jnp.cumsum(padded)
    pad_start = pad_end - padded
    n_valid = pad_end[-1] // tb
    blocks = jnp.arange(n_rows // tb, dtype=I32)
    block_e = jnp.sum((pad_end[None, :] <= (blocks * tb)[:, None]).astype(I32), axis=1)
    block_e = jnp.minimum(block_e, N_EXPERTS - 1)
    block_e = jnp.where(blocks < n_valid, block_e, block_e[n_valid - 1])
    live_rows = jnp.clip(counts[block_e] - (blocks * tb - pad_start[block_e]), 0, tb)
    live_rows = jnp.where(blocks < n_valid, live_rows, 0).astype(I32)
    first = jnp.logical_and(blocks < n_valid, blocks * tb == pad_start[block_e])
    slot = (jnp.cumsum(first.astype(I32)) - 1) % 2
    experts = jnp.arange(N_EXPERTS, dtype=I32)
    used_from = lax.cummin(jnp.where(counts > 0, experts, N_EXPERTS), reverse=True)
    next_used = jnp.concatenate([used_from[1:], jnp.full((1,), N_EXPERTS, I32)])
    next_e = jnp.where(next_used < N_EXPERTS, next_used, -1)[block_e]
    plan = (block_e, n_valid.reshape(1), live_rows, first.astype(I32), slot.astype(I32), next_e.astype(I32))
    dest = _dest(pad_start, idx, rank)
    xs = _dispatch(hpk, dest, n_rows)
    ys = _experts(layer, plan, xs, w_gu, b_gu, w_down, b_down)
    return _gather_rows(dest, ys)


def kernel(x, c, norm1_g, ada_w, ada_b, w_in, gmlp_norm_g, gmlp_ws, gmlp_bs, w_proj_a, pool_w, pool_scale,
           conv_w, w_proj_c, w_out, norm2_g, router_w, router_b, exp_w_gu, exp_b_gu, exp_w_down,
           exp_b_down, final_g):
    depth = ada_w.shape[0]
    bsz, _, d = x.shape
    mods = _ada(c, ada_w, ada_b)
    pending = None
    for l in range(depth):
        sh1, sc1, g1, sh2, sc2, g2 = [mods[l, :, i * d:(i + 1) * d].reshape(bsz, 1, d) for i in range(6)]
        x, *routed = _mixer(x, pending, sh1, sc1, g1, sh2, sc2, norm1_g[l], w_in[l], gmlp_norm_g[l],
                            gmlp_ws[l], gmlp_bs[l], w_proj_a[l], pool_w[l], pool_scale[l], conv_w[l],
                            w_proj_c[l], w_out[l], norm2_g[l], router_w[l], router_b[l])
        gate_w = routed[2]
        pending = (_moe(l, routed, exp_w_gu, exp_b_gu, exp_w_down, exp_b_down), gate_w, g2)
    g, gate_w, g2 = pending
    return _final(g, x, gate_w, g2, final_g)
```

```python
import functools

import jax
import jax.numpy as jnp
from jax import lax
from jax.experimental import pallas as pl
from jax.experimental.pallas import tpu as pltpu
from jax.experimental.pallas import tpu_sc as plsc

F32 = jnp.float32
BF16 = jnp.bfloat16
I32 = jnp.int32
U32 = jnp.uint32

RMS_EPS = 1e-5
GMLP_HEADS = 8
CHUNK = 128
POOL_WINDOWS = (2, 4, 8, 16)
POOL_CARRY = 16
CONV_K = 3
CONV_CARRY = 8
N_EXPERTS = 32
TOP_K = 4
SWIGLU_LIMIT = 7.0
SWIGLU_ALPHA = 1.702

MIX_ROWS = 512
MIX_ROWS_PENDING = 512
MIX_SUB_ROWS = 256
DEST_COLS = 4096
EXPERT_ROWS = 512
EXPERT_BLOCKS_PER_STEP = 2
FINAL_ROWS = 512
SC_SUBCORES = 16
SC_WORKERS = 2 * SC_SUBCORES
SC_WINDOW = 128

VMEM_LIMIT = 58 * 1024 * 1024


def _rms(x, g):
    return x * lax.rsqrt(jnp.mean(x * x, axis=-1, keepdims=True) + RMS_EPS) * g


def _dot(a, b):
    return jnp.dot(a, b, preferred_element_type=F32)


def _params(n_axes, vmem=VMEM_LIMIT):
    return pltpu.CompilerParams(dimension_semantics=("arbitrary",) * n_axes, vmem_limit_bytes=vmem)


def _pack_bf16_pairs(x):
    half = x.shape[1] // 2
    bits = lax.bitcast_convert_type(x.astype(BF16).astype(F32), U32)
    return (bits[:, :half] >> 16) | bits[:, half:]


def _unpack_bf16_pairs(w):
    lo = lax.bitcast_convert_type(w << 16, F32)
    hi = lax.bitcast_convert_type(w & jnp.uint32(0xFFFF0000), F32)
    return jnp.concatenate([lo, hi], axis=1)


def _resident(shape):
    zeros = (0,) * len(shape)
    return pl.BlockSpec(shape, lambda *_: zeros, pipeline_mode=pl.Buffered(1))


def _ada_kernel(c_ref, w_ref, b_ref, o_ref):
    c = c_ref[...]
    cond = c * jax.nn.sigmoid(c)
    o_ref[0] = _dot(cond.astype(BF16), w_ref[0].astype(BF16)) + b_ref[0]


def _ada(c, ada_w, ada_b):
    depth, d, six_d = ada_w.shape
    bsz = c.shape[0]
    return pl.pallas_call(
        _ada_kernel,
        grid=(depth, six_d // d),
        in_specs=[
            pl.BlockSpec((bsz, d), lambda l, j: (0, 0)),
            pl.BlockSpec((1, d, d), lambda l, j: (l, 0, j)),
            pl.BlockSpec((1, 1, d), lambda l, j: (l, 0, j)),
        ],
        out_specs=pl.BlockSpec((1, bsz, d), lambda l, j: (l, 0, j)),
        out_shape=jax.ShapeDtypeStruct((depth, bsz, six_d), F32),
        compiler_params=_params(2),
        name="ada",
    )(c, ada_w, ada_b.reshape(depth, 1, six_d))


def _weighted_rows(g_ref, gw):
    acc = gw[:, 0:1] * _unpack_bf16_pairs(g_ref[0])
    for k in range(1, TOP_K):
        acc = acc + gw[:, k:k + 1] * _unpack_bf16_pairs(g_ref[k])
    return acc


def _mixer_kernel(*refs, pending):
    if pending:
        g_ref, gwp_ref, gtp_ref, *refs = refs
    (x_ref, sh_ref, sc_ref, gt_ref, sh2_ref, sc2_ref, n1_ref, win_ref, gng_ref, ws_ref, bst_ref,
     wpa_ref, pw_ref, ps_ref, cw_ref, wpc_ref, wout_ref, n2_ref, rwt_ref, rb_ref,
     o_ref, hpk_ref, idx_ref, gw_ref, rank_ref, cnt_ref,
     s_ref, pext_ref, zext_ref, carry_ref) = refs
    tm, d = x_ref.shape[1], x_ref.shape[2]
    j = pl.program_id(1)

    @pl.when(j == 0)
    def _():
        pext_ref[0:POOL_CARRY, :] = jnp.zeros((POOL_CARRY, d), F32)
        zext_ref[0:CONV_CARRY, :] = jnp.zeros((CONV_CARRY, d), F32)

    @pl.when(jnp.logical_and(pl.program_id(0) == 0, j == 0))
    def _():
        carry_ref[...] = jnp.zeros(carry_ref.shape, F32)

    hd = d // GMLP_HEADS
    gd = d // len(POOL_WINDOWS)
    row = lax.broadcasted_iota(I32, (CHUNK, CHUNK), 0)
    col = lax.broadcasted_iota(I32, (CHUNK, CHUNK), 1)
    spatial_w = [jnp.where(row >= col, ws_ref[h], 0.0).astype(BF16) for h in range(GMLP_HEADS)]

    fine = not pending

    def rows_block(r0, n):
        rs = slice(r0, r0 + n)
        x = x_ref[0, rs, :]
        if pending:
            x = x + gtp_ref[0] * _weighted_rows(g_ref.at[:, rs, :], gwp_ref[:, rs].T)
        hb = (_rms(x, n1_ref[...]) * (1.0 + sc_ref[0]) + sh_ref[0]).astype(BF16)

        def proj(c):
            return _dot(hb, win_ref[:, c * d:(c + 1) * d])

        u_raw = proj(0)
        if fine:
            yield
        v_raw = proj(1)
        yield
        u = jax.nn.gelu(u_raw)
        if fine:
            yield
        vb = _rms(jax.nn.gelu(v_raw), gng_ref[...]).astype(BF16)
        yield
        for h in range(GMLP_HEADS):
            bias = bst_ref[:, h:h + 1]
            for ci in range(n // CHUNK):
                blk = vb[ci * CHUNK:(ci + 1) * CHUNK, h * hd:(h + 1) * hd]
                s_ref[r0 + ci * CHUNK:r0 + (ci + 1) * CHUNK, h * hd:(h + 1) * hd] = _dot(spatial_w[h], blk) + bias
        if fine:
            yield
        gate_a = proj(6)
        yield
        ya = _dot((u * s_ref[rs, :]).astype(BF16), wpa_ref[...])
        if fine:
            yield
        mix = jax.nn.sigmoid(gate_a) * ya

        p = proj(2)
        if fine:
            yield
        gate_b = proj(7)
        yield
        p0 = POOL_CARRY + r0
        pext_ref[p0:p0 + n, :] = p
        pos1 = (j * tm + r0 + 1 + lax.broadcasted_iota(I32, (n, 1), 0))
        diffs = []
        for gi, w in enumerate(POOL_WINDOWS):
            cs = slice(gi * gd, (gi + 1) * gd)
            acc = p[:, cs]
            for k in range(1, w):
                acc = acc + pext_ref[p0 - k:p0 - k + n, cs]
            cnt = jnp.minimum(pos1, w).astype(F32)
            diffs.append((acc / cnt - p[:, cs]).astype(BF16))
            if fine:
                yield
        if not fine:
            yield
        yb = jnp.concatenate([_dot(dg, pw_ref[gi]) for gi, dg in enumerate(diffs)], axis=1) * ps_ref[...]
        mix = mix + jax.nn.sigmoid(gate_b) * yb

        conv_c = proj(5)
        if fine:
            yield
        conv_x = proj(3)
        if fine:
            yield
        conv_b = proj(4)
        if fine:
            yield
        gate_c = proj(8)
        yield
        z = conv_c * conv_x
        z0 = CONV_CARRY + r0
        zext_ref[z0:z0 + n, :] = z
        conv = cw_ref[CONV_K - 1:CONV_K, :] * z
        for k in range(CONV_K - 1):
            lag = CONV_K - 1 - k
            conv = conv + cw_ref[k:k + 1, :] * zext_ref[z0 - lag:z0 - lag + n, :]
        gated = (conv_b * conv).astype(BF16)
        yield
        yc = _dot(gated, wpc_ref[...])
        if fine:
            yield
        mix = mix + jax.nn.sigmoid(gate_c) * yc

        x_new = x + gt_ref[0] * _dot(mix.astype(BF16), wout_ref[...])
        o_ref[0, rs, :] = x_new
        _route_tile(x_new, sh2_ref[0], sc2_ref[0], n2_ref[...], rwt_ref[...], rb_ref[...],
                    hpk_ref.at[rs, :], idx_ref.at[:, rs], gw_ref.at[:, rs], rank_ref.at[:, rs], cnt_ref, carry_ref)

    sub = min(MIX_SUB_ROWS, tm)
    waiting = [rows_block(r0, sub) for r0 in range(0, tm, sub)]
    running = []
    while waiting or running:
        if waiting:
            running.append(waiting.pop(0))
        for block in list(running):
            if next(block, "done") == "done":
                running.remove(block)
    pext_ref[0:POOL_CARRY, :] = pext_ref[tm:tm + POOL_CARRY, :]
    zext_ref[0:CONV_CARRY, :] = zext_ref[tm:tm + CONV_CARRY, :]


def _mixer(x, pending, sh, sc, gt, sh2, sc2, n1, w_in, gng, ws, bs, wpa, pool_w, pool_scale, conv_w, wpc,
           w_out, n2, router_w, router_b):
    bsz, seq, d = x.shape
    n_tok = bsz * seq
    tm = min(MIX_ROWS if pending is None else MIX_ROWS_PENDING, seq)
    nj = seq // tm
    vec = pl.BlockSpec((1, 1, d), lambda b, j: (b, 0, 0))
    tile = pl.BlockSpec((1, tm, d), lambda b, j: (b, j, 0))
    per_tok = pl.BlockSpec((TOP_K, tm), lambda b, j: (0, b * nj + j))
    pending_specs, pending_args = [], []
    if pending is not None:
        g, gw, gt_prev = pending
        pending_specs = [pl.BlockSpec((TOP_K, tm, d // 2), lambda b, j: (0, b * nj + j, 0)),
                         pl.BlockSpec((TOP_K, tm), lambda b, j: (0, b * nj + j)), vec]
        pending_args = [g, gw, gt_prev]
    return pl.pallas_call(
        functools.partial(_mixer_kernel, pending=pending is not None),
        grid=(bsz, nj),
        in_specs=pending_specs + [
            tile, vec, vec, vec, vec, vec,
            _resident((1, d)),
            _resident(w_in.shape),
            _resident((1, d)),
            _resident(ws.shape),
            _resident((CHUNK, GMLP_HEADS)),
            _resident(wpa.shape),
            _resident(pool_w.shape),
            _resident((1, d)),
            _resident(conv_w.shape),
            _resident(wpc.shape),
            _resident(w_out.shape),
            _resident((1, d)), _resident((N_EXPERTS, d)), _resident((N_EXPERTS, 1)),
        ],
        out_specs=[
            tile,
            pl.BlockSpec((tm, d // 2), lambda b, j: (b * nj + j, 0)),
            per_tok, per_tok, per_tok,
            pl.BlockSpec((N_EXPERTS, 128), lambda b, j: (0, 0)),
        ],
        out_shape=[
            jax.ShapeDtypeStruct(x.shape, F32),
            jax.ShapeDtypeStruct((n_tok, d // 2), U32),
            jax.ShapeDtypeStruct((TOP_K, n_tok), I32),
            jax.ShapeDtypeStruct((TOP_K, n_tok), F32),
            jax.ShapeDtypeStruct((TOP_K, n_tok), I32),
            jax.ShapeDtypeStruct((N_EXPERTS, 128), I32),
        ],
        scratch_shapes=[
            pltpu.VMEM((tm, d), F32),
            pltpu.VMEM((POOL_CARRY + tm, d), F32),
            pltpu.VMEM((CONV_CARRY + tm, d), F32),
            pltpu.VMEM((N_EXPERTS, 128), F32),
        ],
        compiler_params=_params(2),
        name="mixer",
    )(*pending_args, x, sh, sc, gt, sh2, sc2, n1.reshape(1, d), w_in.astype(BF16), gng.reshape(1, d), ws, bs.T,
      wpa.astype(BF16), pool_w.astype(BF16), pool_scale.reshape(1, d), conv_w, wpc.astype(BF16),
      w_out.astype(BF16), n2.reshape(1, d), router_w.T.astype(BF16), router_b.reshape(N_EXPERTS, 1))


def _route_tile(x, sh, sc, n2, rwt, rb, hpk_ref, idx_ref, gw_ref, rank_ref, cnt_ref, carry_ref):
    tm = x.shape[0]

    h = _rms(x, n2) * (1.0 + sc) + sh
    hb = h.astype(BF16)
    hpk_ref[...] = _pack_bf16_pairs(h)

    logits = lax.dot_general(rwt, hb, (((1,), (1,)), ((), ())), preferred_element_type=F32) + rb
    iota_e = lax.broadcasted_iota(I32, logits.shape, 0)
    vals, idxs, sels = [], [], []
    rest = logits
    for _ in range(TOP_K):
        m = jnp.max(rest, axis=0, keepdims=True)
        ik = jnp.min(jnp.where(rest == m, iota_e, N_EXPERTS), axis=0, keepdims=True)
        sel = iota_e == ik
        rest = jnp.where(sel, -jnp.inf, rest)
        vals.append(m)
        idxs.append(ik)
        sels.append(sel)
    exps = [jnp.exp(v - vals[0]) for v in vals]
    denom = exps[0] + exps[1] + exps[2] + exps[3]

    chosen = jnp.logical_or(jnp.logical_or(sels[0], sels[1]), jnp.logical_or(sels[2], sels[3]))
    a = jnp.where(chosen, 1.0, 0.0)
    before = lax.broadcasted_iota(I32, (tm, tm), 0) < lax.broadcasted_iota(I32, (tm, tm), 1)
    prior = _dot(a.astype(BF16), jnp.where(before, 1.0, 0.0).astype(BF16)) + carry_ref[:, 0:1]
    for k in range(TOP_K):
        idx_ref[k:k + 1, :] = idxs[k]
        gw_ref[k:k + 1, :] = exps[k] / denom
        rank_ref[k:k + 1, :] = jnp.sum(jnp.where(sels[k], prior, 0.0), axis=0, keepdims=True).astype(I32)
    total = carry_ref[...] + jnp.sum(a, axis=1, keepdims=True)
    carry_ref[...] = total
    cnt_ref[...] = total.astype(I32)


def _dest_kernel(start_ref, idx_ref, rank_ref, o_ref):
    idx = idx_ref[...]
    base = jnp.zeros(idx.shape, I32)
    for e in range(N_EXPERTS):
        base = jnp.where(idx == e, start_ref[e], base)
    o_ref[...] = base + rank_ref[...]


def _dest(pad_start, idx, rank):
    n_tok = idx.shape[1]
    tc = min(DEST_COLS, n_tok)
    blk = pl.BlockSpec((TOP_K, tc), lambda i, s: (0, i))
    return pl.pallas_call(
        _dest_kernel,
        grid_spec=pltpu.PrefetchScalarGridSpec(
            num_scalar_prefetch=1, grid=(n_tok // tc,), in_specs=[blk, blk], out_specs=blk),
        out_shape=jax.ShapeDtypeStruct(idx.shape, I32),
        compiler_params=_params(1),
        name="dest",
    )(pad_start, idx, rank)


def _dispatch(hpk, dest, n_rows):
    n_tok, half = hpk.shape
    win = SC_WINDOW
    assert n_tok % (win * SC_WORKERS) == 0
    per_worker = n_tok // win // SC_WORKERS
    mesh = plsc.VectorSubcoreMesh(core_axis_name="c", subcore_axis_name="s")

    @pl.kernel(out_type=jax.ShapeDtypeStruct((n_rows, half), U32), mesh=mesh,
               scratch_types=[pltpu.VMEM((win, half), U32), pltpu.VMEM((TOP_K, win), I32)])
    def scatter_rows(hpk_hbm, dest_hbm, xs_hbm, xbuf, ibuf):
        worker = lax.axis_index("c") * SC_SUBCORES + lax.axis_index("s")

        @pl.loop(0, per_worker)
        def _(j):
            t0 = pl.multiple_of((worker * per_worker + j) * win, win)
            pltpu.sync_copy(hpk_hbm.at[pl.ds(t0, win)], xbuf)
            pltpu.sync_copy(dest_hbm.at[:, pl.ds(t0, win)], ibuf)
            for k in range(TOP_K):
                pltpu.sync_copy(xbuf, xs_hbm.at[ibuf.at[k]])

    return scatter_rows(hpk, dest)


def _expert_kernel(be_ref, nv_ref, live_ref, first_ref, slot_ref, next_ref, xs_ref, wgu_hbm, bgu_ref, wd_hbm,
                   bd_ref, ys_ref, wgu_f32, wd_f32, wgu_bf, wd_bf, sem, *, layer):
    step = pl.program_id(0)
    tb = EXPERT_ROWS

    def weight_copies(expert, slot):
        return (pltpu.make_async_copy(wgu_hbm.at[layer, expert], wgu_f32.at[slot], sem.at[0, slot]),
                pltpu.make_async_copy(wd_hbm.at[layer, expert], wd_f32.at[slot], sem.at[1, slot]))

    @pl.when(step == 0)
    def _():
        for copy in weight_copies(be_ref[0], 0):
            copy.start()

    for sub in range(xs_ref.shape[0] // tb):
        i = step * (xs_ref.shape[0] // tb) + sub
        rows = slice(sub * tb, (sub + 1) * tb)

        @pl.when(i >= nv_ref[0])
        def _():
            ys_ref[rows, :] = jnp.zeros((tb, ys_ref.shape[1]), U32)

        @pl.when(first_ref[i] == 1)
        def _():
            slot = slot_ref[i]

            @pl.when(next_ref[i] >= 0)
            def _():
                for copy in weight_copies(next_ref[i], 1 - slot):
                    copy.start()

            for copy in weight_copies(be_ref[i], slot):
                copy.wait()
            wgu_bf[...] = wgu_f32[slot].astype(BF16)
            wd_bf[...] = wd_f32[slot].astype(BF16)

        @pl.when(i < nv_ref[0])
        def _():
            live = lax.broadcasted_iota(I32, (tb, 1), 0) < live_ref[i]
            xb = jnp.where(live, _unpack_bf16_pairs(xs_ref[rows, :]), 0.0).astype(BF16)
            gu = _dot(xb, wgu_bf[...]) + bgu_ref[layer, be_ref[i]]
            ff = gu.shape[1] // 2
            gate = jnp.minimum(gu[:, :ff], SWIGLU_LIMIT)
            up = jnp.clip(gu[:, ff:], -SWIGLU_LIMIT, SWIGLU_LIMIT)
            glu = gate * jax.nn.sigmoid(SWIGLU_ALPHA * gate)
            y = _dot(((up + 1.0) * glu).astype(BF16), wd_bf[...]) + bd_ref[layer, be_ref[i]]
            ys_ref[rows, :] = _pack_bf16_pairs(y)


def _experts(layer, plan, xs, w_gu, b_gu, w_down, b_down):
    n_rows, half = xs.shape
    depth, _, d, two_f = w_gu.shape
    step_rows = EXPERT_ROWS * EXPERT_BLOCKS_PER_STEP

    def rows(i, be, nv, *_):
        return (jnp.minimum(i, (nv[0] - 1) // EXPERT_BLOCKS_PER_STEP), 0)

    return pl.pallas_call(
        functools.partial(_expert_kernel, layer=layer),
        grid_spec=pltpu.PrefetchScalarGridSpec(
            num_scalar_prefetch=len(plan),
            grid=(n_rows // step_rows,),
            in_specs=[
                pl.BlockSpec((step_rows, half), rows),
                pl.BlockSpec(memory_space=pl.ANY),
                _resident((depth, N_EXPERTS, 1, two_f)),
                pl.BlockSpec(memory_space=pl.ANY),
                _resident((depth, N_EXPERTS, 1, d)),
            ],
            out_specs=pl.BlockSpec((step_rows, half), lambda i, *_: (i, 0)),
            scratch_shapes=[
                pltpu.VMEM((2, d, two_f), F32), pltpu.VMEM((2, two_f // 2, d), F32),
                pltpu.VMEM((d, two_f), BF16), pltpu.VMEM((two_f // 2, d), BF16),
                pltpu.SemaphoreType.DMA((2, 2)),
            ],
        ),
        out_shape=jax.ShapeDtypeStruct((n_rows, half), U32),
        compiler_params=_params(1),
        name="experts",
    )(*plan, xs, w_gu, b_gu.reshape(depth, N_EXPERTS, 1, two_f), w_down, b_down.reshape(depth, N_EXPERTS, 1, d))


def _gather_rows(dest, ys):
    n_tok = dest.shape[1]
    width = ys.shape[1]
    win = SC_WINDOW
    assert n_tok % (win * SC_WORKERS) == 0
    per_worker = n_tok // win // SC_WORKERS
    mesh = plsc.VectorSubcoreMesh(core_axis_name="c", subcore_axis_name="s")

    @pl.kernel(out_type=jax.ShapeDtypeStruct((TOP_K, n_tok, width), ys.dtype), mesh=mesh,
               scratch_types=[pltpu.VMEM((win, width), ys.dtype), pltpu.VMEM((TOP_K, win), I32)])
    def gather(ys_hbm, idx_hbm, g_hbm, buf, ibuf):
        worker = lax.axis_index("c") * SC_SUBCORES + lax.axis_index("s")

        @pl.loop(0, per_worker)
        def _(j):
            t0 = pl.multiple_of((worker * per_worker + j) * win, win)
            pltpu.sync_copy(idx_hbm.at[:, pl.ds(t0, win)], ibuf)
            for k in range(TOP_K):
                pltpu.sync_copy(ys_hbm.at[ibuf.at[k]], buf)
                pltpu.sync_copy(buf, g_hbm.at[k, pl.ds(t0, win)])

    return gather(ys, dest)


def _final_kernel(g_ref, x_ref, gw_ref, gt_ref, fg_ref, o_ref):
    y = x_ref[...] + gt_ref[0] * _weighted_rows(g_ref, gw_ref[...].T)
    o_ref[...] = _rms(y, fg_ref[...])


def _final(g, x, gw, gt, final_g):
    bsz, seq, d = x.shape
    n_tok = bsz * seq
    tm = min(FINAL_ROWS, seq)
    per_seq = seq // tm
    out = pl.pallas_call(
        _final_kernel,
        grid=(n_tok // tm,),
        in_specs=[
            pl.BlockSpec((TOP_K, tm, d // 2), lambda i: (0, i, 0)),
            pl.BlockSpec((tm, d), lambda i: (i, 0)),
            pl.BlockSpec((TOP_K, tm), lambda i: (0, i)),
            pl.BlockSpec((1, 1, d), lambda i: (i // per_seq, 0, 0)),
            pl.BlockSpec((1, d), lambda i: (0, 0)),
        ],
        out_specs=pl.BlockSpec((tm, d), lambda i: (i, 0)),
        out_shape=jax.ShapeDtypeStruct((n_tok, d), F32),
        compiler_params=_params(1),
        name="final",
    )(g, x.reshape(n_tok, d), gw, gt, final_g.reshape(1, d))
    return out.reshape(bsz, seq, d)


def _plan_kernel(cnt_ref, start_ref, be_ref, nv_ref, live_ref, first_ref, slot_ref, next_ref):
    tb = EXPERT_ROWS
    n_blocks = be_ref.shape[0]

    def per_expert(e, carry):
        blk0, group = carry
        count = cnt_ref[e]
        n_blk = (count + (tb - 1)) // tb
        start_ref[e] = blk0 * tb

        def per_block(b, c):
            i = blk0 + b
            be_ref[i] = e
            live_ref[i] = jnp.minimum(count - b * tb, tb)
            first_ref[i] = jnp.where(b == 0, 1, 0)
            slot_ref[i] = group % 2
            return c

        lax.fori_loop(0, n_blk, per_block, 0)
        return blk0 + n_blk, group + jnp.where(n_blk > 0, 1, 0)

    used, _ = lax.fori_loop(0, N_EXPERTS, per_expert, (jnp.int32(0), jnp.int32(0)))
    nv_ref[0] = used

    def backwards(t, nxt):
        i = used - 1 - t
        after = be_ref[jnp.minimum(i + 1, used - 1)]
        nxt = jnp.where(jnp.logical_and(i < used - 1, after != be_ref[i]), after, nxt)
        next_ref[i] = nxt
        return nxt

    lax.fori_loop(0, used, backwards, jnp.int32(-1))
    last = be_ref[used - 1]

    def tail(i, c):
        be_ref[i] = last
        live_ref[i] = 0
        first_ref[i] = 0
        slot_ref[i] = 0
        next_ref[i] = -1
        return c

    lax.fori_loop(used, n_blocks, tail, 0)


def _plan(counts, n_blocks):
    smem = pl.BlockSpec(memory_space=pltpu.SMEM)
    per_block = jax.ShapeDtypeStruct((n_blocks,), I32)
    return pl.pallas_call(
        _plan_kernel,
        in_specs=[smem],
        out_specs=[smem] * 7,
        out_shape=[jax.ShapeDtypeStruct((N_EXPERTS,), I32), per_block, jax.ShapeDtypeStruct((1,), I32),
                   per_block, per_block, per_block, per_block],
        name="plan",
    )(counts)


def _moe(layer, routed, w_gu, b_gu, w_down, b_down):
    n_tok = routed[0].shape[0]
    tb = EXPERT_ROWS
    step_rows = tb * EXPERT_BLOCKS_PER_STEP
    n_rows = -(-(n_tok * TOP_K + N_EXPERTS * (tb - 1)) // step_rows) * step_rows
    hpk, idx, gw, rank, cnt = routed
    pad_start, *plan = _plan(cnt[:, 0], n_rows // tb)
    dest = _dest(pad_start, idx, rank)
    xs = _dispatch(hpk, dest, n_rows)
    ys = _experts(layer, plan, xs, w_gu, b_gu, w_down, b_down)
    return _gather_rows(dest, ys)


def kernel(x, c, norm1_g, ada_w, ada_b, w_in, gmlp_norm_g, gmlp_ws, gmlp_bs, w_proj_a, pool_w, pool_scale,
           conv_w, w_proj_c, w_out, norm2_g, router_w, router_b, exp_w_gu, exp_b_gu, exp_w_down,
           exp_b_down, final_g):
    depth = ada_w.shape[0]
    bsz, _, d = x.shape
    mods = _ada(c, ada_w, ada_b)
    pending = None
    for l in range(depth):
        sh1, sc1, g1, sh2, sc2, g2 = [mods[l, :, i * d:(i + 1) * d].reshape(bsz, 1, d) for i in range(6)]
        x, *routed = _mixer(x, pending, sh1, sc1, g1, sh2, sc2, norm1_g[l], w_in[l], gmlp_norm_g[l],
                            gmlp_ws[l], gmlp_bs[l], w_proj_a[l], pool_w[l], pool_scale[l], conv_w[l],
                            w_proj_c[l], w_out[l], norm2_g[l], router_w[l], router_b[l])
        gate_w = routed[2]
        pending = (_moe(l, routed, exp_w_gu, exp_b_gu, exp_w_down, exp_b_down), gate_w, g2)
    g, gate_w, g2 = pending
    return _final(g, x, gate_w, g2, final_g)
```

```python
import functools

import jax
import jax.numpy as jnp
from jax import lax
from jax.experimental import pallas as pl
from jax.experimental.pallas import tpu as pltpu
from jax.experimental.pallas import tpu_sc as plsc

F32 = jnp.float32
BF16 = jnp.bfloat16
I32 = jnp.int32
U32 = jnp.uint32

RMS_EPS = 1e-5
GMLP_HEADS = 8
CHUNK = 128
POOL_WINDOWS = (2, 4, 8, 16)
POOL_CARRY = 16
CONV_K = 3
CONV_CARRY = 8
N_EXPERTS = 32
TOP_K = 4
SWIGLU_LIMIT = 7.0
SWIGLU_ALPHA = 1.702

MIX_ROWS = 512
MIX_ROWS_PENDING = 512
MIX_SUB_ROWS = 256
DEST_COLS = 4096
EXPERT_ROWS = 512
EXPERT_BLOCKS_PER_STEP = 2
FINAL_ROWS = 512
SC_SUBCORES = 16
SC_WORKERS = 2 * SC_SUBCORES
SC_WINDOW = 128
SC_GATHER_PARTS = 2
SC_GATHER_BUFFERS = 3

VMEM_LIMIT = 58 * 1024 * 1024


def _rms(x, g):
    return x * lax.rsqrt(jnp.mean(x * x, axis=-1, keepdims=True) + RMS_EPS) * g


def _dot(a, b):
    return jnp.dot(a, b, preferred_element_type=F32)


def _params(n_axes, vmem=VMEM_LIMIT):
    return pltpu.CompilerParams(dimension_semantics=("arbitrary",) * n_axes, vmem_limit_bytes=vmem)


def _pack_bf16_pairs(x):
    half = x.shape[1] // 2
    bits = lax.bitcast_convert_type(x.astype(BF16).astype(F32), U32)
    return (bits[:, :half] >> 16) | bits[:, half:]


def _unpack_bf16_pairs(w):
    lo = lax.bitcast_convert_type(w << 16, F32)
    hi = lax.bitcast_convert_type(w & jnp.uint32(0xFFFF0000), F32)
    return jnp.concatenate([lo, hi], axis=1)


def _resident(shape):
    zeros = (0,) * len(shape)
    return pl.BlockSpec(shape, lambda *_: zeros, pipeline_mode=pl.Buffered(1))


def _ada_kernel(c_ref, w_ref, b_ref, o_ref):
    c = c_ref[...]
    cond = c * jax.nn.sigmoid(c)
    o_ref[0] = _dot(cond.astype(BF16), w_ref[0].astype(BF16)) + b_ref[0]


def _ada(c, ada_w, ada_b):
    depth, d, six_d = ada_w.shape
    bsz = c.shape[0]
    return pl.pallas_call(
        _ada_kernel,
        grid=(depth, six_d // d),
        in_specs=[
            pl.BlockSpec((bsz, d), lambda l, j: (0, 0)),
            pl.BlockSpec((1, d, d), lambda l, j: (l, 0, j)),
            pl.BlockSpec((1, 1, d), lambda l, j: (l, 0, j)),
        ],
        out_specs=pl.BlockSpec((1, bsz, d), lambda l, j: (l, 0, j)),
        out_shape=jax.ShapeDtypeStruct((depth, bsz, six_d), F32),
        compiler_params=_params(2),
        name="ada",
    )(c, ada_w, ada_b.reshape(depth, 1, six_d))


def _weighted_rows(g_ref, gw):
    acc = gw[:, 0:1] * _unpack_bf16_pairs(g_ref[0])
    for k in range(1, TOP_K):
        acc = acc + gw[:, k:k + 1] * _unpack_bf16_pairs(g_ref[k])
    return acc


def _mixer_kernel(*refs, pending):
    if pending:
        g_ref, gwp_ref, gtp_ref, *refs = refs
    (x_ref, sh_ref, sc_ref, gt_ref, sh2_ref, sc2_ref, n1_ref, win_ref, gng_ref, ws_ref, bst_ref,
     wpa_ref, pw_ref, ps_ref, cw_ref, wpc_ref, wout_ref, n2_ref, rwt_ref, rb_ref,
     o_ref, hpk_ref, idx_ref, gw_ref, rank_ref, cnt_ref,
     s_ref, pext_ref, zext_ref, carry_ref) = refs
    tm, d = x_ref.shape[1], x_ref.shape[2]
    j = pl.program_id(1)

    @pl.when(j == 0)
    def _():
        pext_ref[0:POOL_CARRY, :] = jnp.zeros((POOL_CARRY, d), F32)
        zext_ref[0:CONV_CARRY, :] = jnp.zeros((CONV_CARRY, d), F32)

    @pl.when(jnp.logical_and(pl.program_id(0) == 0, j == 0))
    def _():
        carry_ref[...] = jnp.zeros(carry_ref.shape, F32)

    hd = d // GMLP_HEADS
    gd = d // len(POOL_WINDOWS)
    row = lax.broadcasted_iota(I32, (CHUNK, CHUNK), 0)
    col = lax.broadcasted_iota(I32, (CHUNK, CHUNK), 1)
    spatial_w = [jnp.where(row >= col, ws_ref[h], 0.0).astype(BF16) for h in range(GMLP_HEADS)]

    fine = not pending

    def rows_block(r0, n):
        rs = slice(r0, r0 + n)
        x = x_ref[0, rs, :]
        if pending:
            x = x + gtp_ref[0] * _weighted_rows(g_ref.at[:, rs, :], gwp_ref[:, rs].T)
        hb = (_rms(x, n1_ref[...]) * (1.0 + sc_ref[0]) + sh_ref[0]).astype(BF16)

        def proj(c):
            return _dot(hb, win_ref[:, c * d:(c + 1) * d])

        u_raw = proj(0)
        if fine:
            yield
        v_raw = proj(1)
        yield
        u = jax.nn.gelu(u_raw)
        if fine:
            yield
        vb = _rms(jax.nn.gelu(v_raw), gng_ref[...]).astype(BF16)
        yield
        for h in range(GMLP_HEADS):
            bias = bst_ref[:, h:h + 1]
            for ci in range(n // CHUNK):
                blk = vb[ci * CHUNK:(ci + 1) * CHUNK, h * hd:(h + 1) * hd]
                s_ref[r0 + ci * CHUNK:r0 + (ci + 1) * CHUNK, h * hd:(h + 1) * hd] = _dot(spatial_w[h], blk) + bias
        if fine:
            yield
        gate_a = proj(6)
        yield
        ya = _dot((u * s_ref[rs, :]).astype(BF16), wpa_ref[...])
        if fine:
            yield
        mix = jax.nn.sigmoid(gate_a) * ya

        p = proj(2)
        if fine:
            yield
        gate_b = proj(7)
        yield
        p0 = POOL_CARRY + r0
        pext_ref[p0:p0 + n, :] = p
        pos1 = (j * tm + r0 + 1 + lax.broadcasted_iota(I32, (n, 1), 0))
        diffs = []
        for gi, w in enumerate(POOL_WINDOWS):
            cs = slice(gi * gd, (gi + 1) * gd)
            acc = p[:, cs]
            for k in range(1, w):
                acc = acc + pext_ref[p0 - k:p0 - k + n, cs]
            cnt = jnp.minimum(pos1, w).astype(F32)
            diffs.append((acc / cnt - p[:, cs]).astype(BF16))
            if fine:
                yield
        if not fine:
            yield
        yb = jnp.concatenate([_dot(dg, pw_ref[gi]) for gi, dg in enumerate(diffs)], axis=1) * ps_ref[...]
        mix = mix + jax.nn.sigmoid(gate_b) * yb

        conv_c = proj(5)
        if fine:
            yield
        conv_x = proj(3)
        if fine:
            yield
        conv_b = proj(4)
        if fine:
            yield
        gate_c = proj(8)
        yield
        z = conv_c * conv_x
        z0 = CONV_CARRY + r0
        zext_ref[z0:z0 + n, :] = z
        conv = cw_ref[CONV_K - 1:CONV_K, :] * z
        for k in range(CONV_K - 1):
            lag = CONV_K - 1 - k
            conv = conv + cw_ref[k:k + 1, :] * zext_ref[z0 - lag:z0 - lag + n, :]
        gated = (conv_b * conv).astype(BF16)
        yield
        yc = _dot(gated, wpc_ref[...])
        if fine:
            yield
        mix = mix + jax.nn.sigmoid(gate_c) * yc

        x_new = x + gt_ref[0] * _dot(mix.astype(BF16), wout_ref[...])
        o_ref[0, rs, :] = x_new
        _route_tile(x_new, sh2_ref[0], sc2_ref[0], n2_ref[...], rwt_ref[...], rb_ref[...],
                    hpk_ref.at[rs, :], idx_ref.at[:, rs], gw_ref.at[:, rs], rank_ref.at[:, rs], cnt_ref, carry_ref)

    sub = min(MIX_SUB_ROWS, tm)
    waiting = [rows_block(r0, sub) for r0 in range(0, tm, sub)]
    running = []
    while waiting or running:
        if waiting:
            running.append(waiting.pop(0))
        for block in list(running):
            if next(block, "done") == "done":
                running.remove(block)
    pext_ref[0:POOL_CARRY, :] = pext_ref[tm:tm + POOL_CARRY, :]
    zext_ref[0:CONV_CARRY, :] = zext_ref[tm:tm + CONV_CARRY, :]


def _mixer(x, pending, sh, sc, gt, sh2, sc2, n1, w_in, gng, ws, bs, wpa, pool_w, pool_scale, conv_w, wpc,
           w_out, n2, router_w, router_b):
    bsz, seq, d = x.shape
    n_tok = bsz * seq
    tm = min(MIX_ROWS if pending is None else MIX_ROWS_PENDING, seq)
    nj = seq // tm
    vec = pl.BlockSpec((1, 1, d), lambda b, j: (b, 0, 0))
    tile = pl.BlockSpec((1, tm, d), lambda b, j: (b, j, 0))
    per_tok = pl.BlockSpec((TOP_K, tm), lambda b, j: (0, b * nj + j))
    pending_specs, pending_args = [], []
    if pending is not None:
        g, gw, gt_prev = pending
        pending_specs = [pl.BlockSpec((TOP_K, tm, d // 2), lambda b, j: (0, b * nj + j, 0)),
                         pl.BlockSpec((TOP_K, tm), lambda b, j: (0, b * nj + j)), vec]
        pending_args = [g, gw, gt_prev]
    return pl.pallas_call(
        functools.partial(_mixer_kernel, pending=pending is not None),
        grid=(bsz, nj),
        in_specs=pending_specs + [
            tile, vec, vec, vec, vec, vec,
            _resident((1, d)),
            _resident(w_in.shape),
            _resident((1, d)),
            _resident(ws.shape),
            _resident((CHUNK, GMLP_HEADS)),
            _resident(wpa.shape),
            _resident(pool_w.shape),
            _resident((1, d)),
            _resident(conv_w.shape),
            _resident(wpc.shape),
            _resident(w_out.shape),
            _resident((1, d)), _resident((N_EXPERTS, d)), _resident((N_EXPERTS, 1)),
        ],
        out_specs=[
            tile,
            pl.BlockSpec((tm, d // 2), lambda b, j: (b * nj + j, 0)),
            per_tok, per_tok, per_tok,
            pl.BlockSpec((N_EXPERTS, 128), lambda b, j: (0, 0)),
        ],
        out_shape=[
            jax.ShapeDtypeStruct(x.shape, F32),
            jax.ShapeDtypeStruct((n_tok, d // 2), U32),
            jax.ShapeDtypeStruct((TOP_K, n_tok), I32),
            jax.ShapeDtypeStruct((TOP_K, n_tok), F32),
            jax.ShapeDtypeStruct((TOP_K, n_tok), I32),
            jax.ShapeDtypeStruct((N_EXPERTS, 128), I32),
        ],
        scratch_shapes=[
            pltpu.VMEM((tm, d), F32),
            pltpu.VMEM((POOL_CARRY + tm, d), F32),
            pltpu.VMEM((CONV_CARRY + tm, d), F32),
            pltpu.VMEM((N_EXPERTS, 128), F32),
        ],
        compiler_params=_params(2),
        name="mixer",
    )(*pending_args, x, sh, sc, gt, sh2, sc2, n1.reshape(1, d), w_in.astype(BF16), gng.reshape(1, d), ws, bs.T,
      wpa.astype(BF16), pool_w.astype(BF16), pool_scale.reshape(1, d), conv_w, wpc.astype(BF16),
      w_out.astype(BF16), n2.reshape(1, d), router_w.T.astype(BF16), router_b.reshape(N_EXPERTS, 1))


def _route_tile(x, sh, sc, n2, rwt, rb, hpk_ref, idx_ref, gw_ref, rank_ref, cnt_ref, carry_ref):
    tm = x.shape[0]

    h = _rms(x, n2) * (1.0 + sc) + sh
    hb = h.astype(BF16)
    hpk_ref[...] = _pack_bf16_pairs(h)

    logits = lax.dot_general(rwt, hb, (((1,), (1,)), ((), ())), preferred_element_type=F32) + rb
    iota_e = lax.broadcasted_iota(I32, logits.shape, 0)
    vals, idxs, sels = [], [], []
    rest = logits
    for _ in range(TOP_K):
        m = jnp.max(rest, axis=0, keepdims=True)
        ik = jnp.min(jnp.where(rest == m, iota_e, N_EXPERTS), axis=0, keepdims=True)
        sel = iota_e == ik
        rest = jnp.where(sel, -jnp.inf, rest)
        vals.append(m)
        idxs.append(ik)
        sels.append(sel)
    exps = [jnp.exp(v - vals[0]) for v in vals]
    denom = exps[0] + exps[1] + exps[2] + exps[3]

    chosen = jnp.logical_or(jnp.logical_or(sels[0], sels[1]), jnp.logical_or(sels[2], sels[3]))
    a = jnp.where(chosen, 1.0, 0.0)
    before = lax.broadcasted_iota(I32, (tm, tm), 0) < lax.broadcasted_iota(I32, (tm, tm), 1)
    prior = _dot(a.astype(BF16), jnp.where(before, 1.0, 0.0).astype(BF16)) + carry_ref[:, 0:1]
    for k in range(TOP_K):
        idx_ref[k:k + 1, :] = idxs[k]
        gw_ref[k:k + 1, :] = exps[k] / denom
        rank_ref[k:k + 1, :] = jnp.sum(jnp.where(sels[k], prior, 0.0), axis=0, keepdims=True).astype(I32)
    total = carry_ref[...] + jnp.sum(a, axis=1, keepdims=True)
    carry_ref[...] = total
    cnt_ref[...] = total.astype(I32)


def _dest_kernel(start_ref, idx_ref, rank_ref, o_ref):
    idx = idx_ref[...]
    base = jnp.zeros(idx.shape, I32)
    for e in range(N_EXPERTS):
        base = jnp.where(idx == e, start_ref[e], base)
    o_ref[...] = base + rank_ref[...]


def _dest(pad_start, idx, rank):
    n_tok = idx.shape[1]
    tc = min(DEST_COLS, n_tok)
    blk = pl.BlockSpec((TOP_K, tc), lambda i, s: (0, i))
    return pl.pallas_call(
        _dest_kernel,
        grid_spec=pltpu.PrefetchScalarGridSpec(
            num_scalar_prefetch=1, grid=(n_tok // tc,), in_specs=[blk, blk], out_specs=blk),
        out_shape=jax.ShapeDtypeStruct(idx.shape, I32),
        compiler_params=_params(1),
        name="dest",
    )(pad_start, idx, rank)


def _dispatch(hpk, dest, n_rows):
    n_tok, half = hpk.shape
    win = SC_WINDOW
    assert n_tok % (win * SC_WORKERS) == 0
    per_worker = n_tok // win // SC_WORKERS
    mesh = plsc.VectorSubcoreMesh(core_axis_name="c", subcore_axis_name="s")

    @pl.kernel(out_type=jax.ShapeDtypeStruct((n_rows, half), U32), mesh=mesh,
               scratch_types=[pltpu.VMEM((win, half), U32), pltpu.VMEM((TOP_K, win), I32),
                              pltpu.SemaphoreType.DMA((TOP_K + 1,))])
    def scatter_rows(hpk_hbm, dest_hbm, xs_hbm, xbuf, ibuf, sem):
        worker = lax.axis_index("c") * SC_SUBCORES + lax.axis_index("s")

        @pl.loop(0, per_worker)
        def _(j):
            t0 = pl.multiple_of((worker * per_worker + j) * win, win)
            rows_in = pltpu.async_copy(hpk_hbm.at[pl.ds(t0, win)], xbuf, sem.at[TOP_K])
            pltpu.sync_copy(dest_hbm.at[:, pl.ds(t0, win)], ibuf)
            rows_in.wait()
            scatters = [pltpu.async_copy(xbuf, xs_hbm.at[ibuf.at[k]], sem.at[k]) for k in range(TOP_K)]
            for scatter in scatters:
                scatter.wait()

    return scatter_rows(hpk, dest)


def _expert_kernel(be_ref, nv_ref, live_ref, first_ref, slot_ref, next_ref, xs_ref, wgu_hbm, bgu_ref, wd_hbm,
                   bd_ref, ys_ref, wgu_f32, wd_f32, wgu_bf, wd_bf, sem, *, layer):
    step = pl.program_id(0)
    tb = EXPERT_ROWS

    def weight_copies(expert, slot):
        return (pltpu.make_async_copy(wgu_hbm.at[layer, expert], wgu_f32.at[slot], sem.at[0, slot]),
                pltpu.make_async_copy(wd_hbm.at[layer, expert], wd_f32.at[slot], sem.at[1, slot]))

    @pl.when(step == 0)
    def _():
        for copy in weight_copies(be_ref[0], 0):
            copy.start()

    for sub in range(xs_ref.shape[0] // tb):
        i = step * (xs_ref.shape[0] // tb) + sub
        rows = slice(sub * tb, (sub + 1) * tb)

        @pl.when(i >= nv_ref[0])
        def _():
            ys_ref[rows, :] = jnp.zeros((tb, ys_ref.shape[1]), U32)

        @pl.when(first_ref[i] == 1)
        def _():
            slot = slot_ref[i]

            @pl.when(next_ref[i] >= 0)
            def _():
                for copy in weight_copies(next_ref[i], 1 - slot):
                    copy.start()

            for copy in weight_copies(be_ref[i], slot):
                copy.wait()
            wgu_bf[...] = wgu_f32[slot].astype(BF16)
            wd_bf[...] = wd_f32[slot].astype(BF16)

        @pl.when(i < nv_ref[0])
        def _():
            live = lax.broadcasted_iota(I32, (tb, 1), 0) < live_ref[i]
            xb = jnp.where(live, _unpack_bf16_pairs(xs_ref[rows, :]), 0.0).astype(BF16)
            gu = _dot(xb, wgu_bf[...]) + bgu_ref[layer, be_ref[i]]
            ff = gu.shape[1] // 2
            gate = jnp.minimum(gu[:, :ff], SWIGLU_LIMIT)
            up = jnp.clip(gu[:, ff:], -SWIGLU_LIMIT, SWIGLU_LIMIT)
            glu = gate * jax.nn.sigmoid(SWIGLU_ALPHA * gate)
            y = _dot(((up + 1.0) * glu).astype(BF16), wd_bf[...]) + bd_ref[layer, be_ref[i]]
            ys_ref[rows, :] = _pack_bf16_pairs(y)


def _experts(layer, plan, xs, w_gu, b_gu, w_down, b_down):
    n_rows, half = xs.shape
    depth, _, d, two_f = w_gu.shape
    step_rows = EXPERT_ROWS * EXPERT_BLOCKS_PER_STEP

    def rows(i, be, nv, *_):
        return (jnp.minimum(i, (nv[0] - 1) // EXPERT_BLOCKS_PER_STEP), 0)

    return pl.pallas_call(
        functools.partial(_expert_kernel, layer=layer),
        grid_spec=pltpu.PrefetchScalarGridSpec(
            num_scalar_prefetch=len(plan),
            grid=(n_rows // step_rows,),
            in_specs=[
                pl.BlockSpec((step_rows, half), rows),
                pl.BlockSpec(memory_space=pl.ANY),
                _resident((depth, N_EXPERTS, 1, two_f)),
                pl.BlockSpec(memory_space=pl.ANY),
                _resident((depth, N_EXPERTS, 1, d)),
            ],
            out_specs=pl.BlockSpec((step_rows, half), lambda i, *_: (i, 0)),
            scratch_shapes=[
                pltpu.VMEM((2, d, two_f), F32), pltpu.VMEM((2, two_f // 2, d), F32),
                pltpu.VMEM((d, two_f), BF16), pltpu.VMEM((two_f // 2, d), BF16),
                pltpu.SemaphoreType.DMA((2, 2)),
            ],
        ),
        out_shape=jax.ShapeDtypeStruct((n_rows, half), U32),
        compiler_params=_params(1),
        name="experts",
    )(*plan, xs, w_gu, b_gu.reshape(depth, N_EXPERTS, 1, two_f), w_down, b_down.reshape(depth, N_EXPERTS, 1, d))


def _gather_rows(dest, ys):
    n_tok = dest.shape[1]
    width = ys.shape[1]
    win = SC_WINDOW
    assert n_tok % (win * SC_WORKERS) == 0
    per_worker = n_tok // win // SC_WORKERS
    mesh = plsc.VectorSubcoreMesh(core_axis_name="c", subcore_axis_name="s")

    part = win // SC_GATHER_PARTS
    n_items = TOP_K * SC_GATHER_PARTS
    n_buf = SC_GATHER_BUFFERS

    @pl.kernel(out_type=jax.ShapeDtypeStruct((TOP_K, n_tok, width), ys.dtype), mesh=mesh,
               scratch_types=[pltpu.VMEM((n_buf, part, width), ys.dtype), pltpu.VMEM((TOP_K, win), I32),
                              pltpu.SemaphoreType.DMA((2, n_buf))])
    def gather(ys_hbm, idx_hbm, g_hbm, buf, ibuf, sem):
        worker = lax.axis_index("c") * SC_SUBCORES + lax.axis_index("s")

        @pl.loop(0, per_worker)
        def _(j):
            t0 = pl.multiple_of((worker * per_worker + j) * win, win)
            pltpu.sync_copy(idx_hbm.at[:, pl.ds(t0, win)], ibuf)

            def fetch(n):
                k, h = divmod(n, SC_GATHER_PARTS)
                return pltpu.async_copy(ys_hbm.at[ibuf.at[k, pl.ds(h * part, part)]], buf.at[n % n_buf],
                                        sem.at[0, n % n_buf])

            def store(n):
                k, h = divmod(n, SC_GATHER_PARTS)
                return pltpu.async_copy(buf.at[n % n_buf], g_hbm.at[k, pl.ds(t0 + h * part, part)],
                                        sem.at[1, n % n_buf])

            fetches = {n: fetch(n) for n in range(n_buf - 1)}
            stores = {}
            for n in range(n_items):
                fetches[n].wait()
                stores[n] = store(n)
                nxt = n + n_buf - 1
                if nxt < n_items:
                    if n >= 1:
                        stores.pop(n - 1).wait()
                    fetches[nxt] = fetch(nxt)
            for pending_store in stores.values():
                pending_store.wait()

    return gather(ys, dest)


def _final_kernel(g_ref, x_ref, gw_ref, gt_ref, fg_ref, o_ref):
    y = x_ref[...] + gt_ref[0] * _weighted_rows(g_ref, gw_ref[...].T)
    o_ref[...] = _rms(y, fg_ref[...])


def _final(g, x, gw, gt, final_g):
    bsz, seq, d = x.shape
    n_tok = bsz * seq
    tm = min(FINAL_ROWS, seq)
    per_seq = seq // tm
    out = pl.pallas_call(
        _final_kernel,
        grid=(n_tok // tm,),
        in_specs=[
            pl.BlockSpec((TOP_K, tm, d // 2), lambda i: (0, i, 0)),
            pl.BlockSpec((tm, d), lambda i: (i, 0)),
            pl.BlockSpec((TOP_K, tm), lambda i: (0, i)),
            pl.BlockSpec((1, 1, d), lambda i: (i // per_seq, 0, 0)),
            pl.BlockSpec((1, d), lambda i: (0, 0)),
        ],
        out_specs=pl.BlockSpec((tm, d), lambda i: (i, 0)),
        out_shape=jax.ShapeDtypeStruct((n_tok, d), F32),
        compiler_params=_params(1),
        name="final",
    )(g, x.reshape(n_tok, d), gw, gt, final_g.reshape(1, d))
    return out.reshape(bsz, seq, d)


def _plan_kernel(cnt_ref, start_ref, be_ref, nv_ref, live_ref, first_ref, slot_ref, next_ref):
    tb = EXPERT_ROWS
    n_blocks = be_ref.shape[0]

    def per_expert(e, carry):
        blk0, group = carry
        count = cnt_ref[e]
        n_blk = (count + (tb - 1)) // tb
        start_ref[e] = blk0 * tb

        def per_block(b, c):
            i = blk0 + b
            be_ref[i] = e
            live_ref[i] = jnp.minimum(count - b * tb, tb)
            first_ref[i] = jnp.where(b == 0, 1, 0)
            slot_ref[i] = group % 2
            return c

        lax.fori_loop(0, n_blk, per_block, 0)
        return blk0 + n_blk, group + jnp.where(n_blk > 0, 1, 0)

    used, _ = lax.fori_loop(0, N_EXPERTS, per_expert, (jnp.int32(0), jnp.int32(0)))
    nv_ref[0] = used

    def backwards(t, nxt):
        i = used - 1 - t
        after = be_ref[jnp.minimum(i + 1, used - 1)]
        nxt = jnp.where(jnp.logical_and(i < used - 1, after != be_ref[i]), after, nxt)
        next_ref[i] = nxt
        return nxt

    lax.fori_loop(0, used, backwards, jnp.int32(-1))
    last = be_ref[used - 1]

    def tail(i, c):
        be_ref[i] = last
        live_ref[i] = 0
        first_ref[i] = 0
        slot_ref[i] = 0
        next_ref[i] = -1
        return c

    lax.fori_loop(used, n_blocks, tail, 0)


def _plan(counts, n_blocks):
    smem = pl.BlockSpec(memory_space=pltpu.SMEM)
    per_block = jax.ShapeDtypeStruct((n_blocks,), I32)
    return pl.pallas_call(
        _plan_kernel,
        in_specs=[smem],
        out_specs=[smem] * 7,
        out_shape=[jax.ShapeDtypeStruct((N_EXPERTS,), I32), per_block, jax.ShapeDtypeStruct((1,), I32),
                   per_block, per_block, per_block, per_block],
        name="plan",
    )(counts)


def _moe(layer, routed, w_gu, b_gu, w_down, b_down):
    n_tok = routed[0].shape[0]
    tb = EXPERT_ROWS
    step_rows = tb * EXPERT_BLOCKS_PER_STEP
    n_rows = -(-(n_tok * TOP_K + N_EXPERTS * (tb - 1)) // step_rows) * step_rows
    hpk, idx, gw, rank, cnt = routed
    pad_start, *plan = _plan(cnt[:, 0], n_rows // tb)
    dest = _dest(pad_start, idx, rank)
    xs = _dispatch(hpk, dest, n_rows)
    ys = _experts(layer, plan, xs, w_gu, b_gu, w_down, b_down)
    return _gather_rows(dest, ys)


def kernel(x, c, norm1_g, ada_w, ada_b, w_in, gmlp_norm_g, gmlp_ws, gmlp_bs, w_proj_a, pool_w, pool_scale,
           conv_w, w_proj_c, w_out, norm2_g, router_w, router_b, exp_w_gu, exp_b_gu, exp_w_down,
           exp_b_down, final_g):
    depth = ada_w.shape[0]
    bsz, _, d = x.shape
    mods = _ada(c, ada_w, ada_b)
    pending = None
    for l in range(depth):
        sh1, sc1, g1, sh2, sc2, g2 = [mods[l, :, i * d:(i + 1) * d].reshape(bsz, 1, d) for i in range(6)]
        x, *routed = _mixer(x, pending, sh1, sc1, g1, sh2, sc2, norm1_g[l], w_in[l], gmlp_norm_g[l],
                            gmlp_ws[l], gmlp_bs[l], w_proj_a[l], pool_w[l], pool_scale[l], conv_w[l],
                            w_proj_c[l], w_out[l], norm2_g[l], router_w[l], router_b[l])
        gate_w = routed[2]
        pending = (_moe(l, routed, exp_w_gu, exp_b_gu, exp_w_down, exp_b_down), gate_w, g2)
    g, gate_w, g2 = pending
    return _final(g, x, gate_w, g2, final_g)
```

```python
import functools

import jax
import jax.numpy as jnp
from jax import lax
from jax.experimental import pallas as pl
from jax.experimental.pallas import tpu as pltpu
from jax.experimental.pallas import tpu_sc as plsc

F32 = jnp.float32
BF16 = jnp.bfloat16
I32 = jnp.int32
U32 = jnp.uint32

RMS_EPS = 1e-5
GMLP_HEADS = 8
CHUNK = 128
POOL_WINDOWS = (2, 4, 8, 16)
POOL_CARRY = 16
CONV_K = 3
CONV_CARRY = 8
N_EXPERTS = 32
TOP_K = 4
SWIGLU_LIMIT = 7.0
SWIGLU_ALPHA = 1.702

MIX_ROWS = 512
MIX_ROWS_PENDING = 512
MIX_SUB_ROWS = 256
DEST_COLS = 4096
EXPERT_ROWS = 512
EXPERT_BLOCKS_PER_STEP = 2
FINAL_ROWS = 512
FINAL_PARTS = 2
SC_SUBCORES = 16
SC_WORKERS = 2 * SC_SUBCORES
SC_WINDOW = 128
SC_GATHER_PARTS = 2
SC_GATHER_BUFFERS = 3

VMEM_LIMIT = 58 * 1024 * 1024


def _rms(x, g):
    return x * lax.rsqrt(jnp.mean(x * x, axis=-1, keepdims=True) + RMS_EPS) * g


def _dot(a, b):
    return jnp.dot(a, b, preferred_element_type=F32)


def _params(n_axes, vmem=VMEM_LIMIT):
    return pltpu.CompilerParams(dimension_semantics=("arbitrary",) * n_axes, vmem_limit_bytes=vmem)


def _pack_bf16_pairs(x):
    half = x.shape[1] // 2
    bits = lax.bitcast_convert_type(x.astype(BF16).astype(F32), U32)
    return (bits[:, :half] >> 16) | bits[:, half:]


def _unpack_bf16_pairs(w):
    lo = lax.bitcast_convert_type(w << 16, F32)
    hi = lax.bitcast_convert_type(w & jnp.uint32(0xFFFF0000), F32)
    return jnp.concatenate([lo, hi], axis=1)


def _resident(shape):
    zeros = (0,) * len(shape)
    return pl.BlockSpec(shape, lambda *_: zeros, pipeline_mode=pl.Buffered(1))


def _ada_kernel(c_ref, w_ref, b_ref, o_ref):
    c = c_ref[...]
    cond = c * jax.nn.sigmoid(c)
    o_ref[0] = _dot(cond.astype(BF16), w_ref[0].astype(BF16)) + b_ref[0]


def _ada(c, ada_w, ada_b):
    depth, d, six_d = ada_w.shape
    bsz = c.shape[0]
    return pl.pallas_call(
        _ada_kernel,
        grid=(depth, six_d // d),
        in_specs=[
            pl.BlockSpec((bsz, d), lambda l, j: (0, 0)),
            pl.BlockSpec((1, d, d), lambda l, j: (l, 0, j)),
            pl.BlockSpec((1, 1, d), lambda l, j: (l, 0, j)),
        ],
        out_specs=pl.BlockSpec((1, bsz, d), lambda l, j: (l, 0, j)),
        out_shape=jax.ShapeDtypeStruct((depth, bsz, six_d), F32),
        compiler_params=_params(2),
        name="ada",
    )(c, ada_w, ada_b.reshape(depth, 1, six_d))


def _weighted_rows(g_ref, gw):
    acc = gw[:, 0:1] * _unpack_bf16_pairs(g_ref[0])
    for k in range(1, TOP_K):
        acc = acc + gw[:, k:k + 1] * _unpack_bf16_pairs(g_ref[k])
    return acc


def _mixer_kernel(*refs, pending):
    if pending:
        g_ref, gwp_ref, gtp_ref, *refs = refs
    (x_ref, sh_ref, sc_ref, gt_ref, sh2_ref, sc2_ref, n1_ref, win_ref, gng_ref, ws_ref, bst_ref,
     wpa_ref, pw_ref, ps_ref, cw_ref, wpc_ref, wout_ref, n2_ref, rwt_ref, rb_ref,
     o_ref, hpk_ref, idx_ref, gw_ref, rank_ref, cnt_ref,
     s_ref, pext_ref, zext_ref, carry_ref) = refs
    tm, d = x_ref.shape[1], x_ref.shape[2]
    j = pl.program_id(1)

    @pl.when(j == 0)
    def _():
        pext_ref[0:POOL_CARRY, :] = jnp.zeros((POOL_CARRY, d), F32)
        zext_ref[0:CONV_CARRY, :] = jnp.zeros((CONV_CARRY, d), F32)

    @pl.when(jnp.logical_and(pl.program_id(0) == 0, j == 0))
    def _():
        carry_ref[...] = jnp.zeros(carry_ref.shape, F32)

    hd = d // GMLP_HEADS
    gd = d // len(POOL_WINDOWS)
    row = lax.broadcasted_iota(I32, (CHUNK, CHUNK), 0)
    col = lax.broadcasted_iota(I32, (CHUNK, CHUNK), 1)
    spatial_w = [jnp.where(row >= col, ws_ref[h], 0.0).astype(BF16) for h in range(GMLP_HEADS)]

    fine = not pending

    def rows_block(r0, n):
        rs = slice(r0, r0 + n)
        x = x_ref[0, rs, :]
        if pending:
            x = x + gtp_ref[0] * _weighted_rows(g_ref.at[:, rs, :], gwp_ref[:, rs].T)
        hb = (_rms(x, n1_ref[...]) * (1.0 + sc_ref[0]) + sh_ref[0]).astype(BF16)

        def proj(c):
            return _dot(hb, win_ref[:, c * d:(c + 1) * d])

        u_raw = proj(0)
        if fine:
            yield
        v_raw = proj(1)
        yield
        u = jax.nn.gelu(u_raw)
        if fine:
            yield
        vb = _rms(jax.nn.gelu(v_raw), gng_ref[...]).astype(BF16)
        yield
        for h in range(GMLP_HEADS):
            bias = bst_ref[:, h:h + 1]
            for ci in range(n // CHUNK):
                blk = vb[ci * CHUNK:(ci + 1) * CHUNK, h * hd:(h + 1) * hd]
                s_ref[r0 + ci * CHUNK:r0 + (ci + 1) * CHUNK, h * hd:(h + 1) * hd] = _dot(spatial_w[h], blk) + bias
        if fine:
            yield
        gate_a = proj(6)
        yield
        ya = _dot((u * s_ref[rs, :]).astype(BF16), wpa_ref[...])
        if fine:
            yield
        mix = jax.nn.sigmoid(gate_a) * ya

        p = proj(2)
        if fine:
            yield
        gate_b = proj(7)
        yield
        p0 = POOL_CARRY + r0
        pext_ref[p0:p0 + n, :] = p
        pos1 = (j * tm + r0 + 1 + lax.broadcasted_iota(I32, (n, 1), 0))
        diffs = []
        for gi, w in enumerate(POOL_WINDOWS):
            cs = slice(gi * gd, (gi + 1) * gd)
            acc = p[:, cs]
            for k in range(1, w):
                acc = acc + pext_ref[p0 - k:p0 - k + n, cs]
            cnt = jnp.minimum(pos1, w).astype(F32)
            diffs.append((acc / cnt - p[:, cs]).astype(BF16))
            if fine:
                yield
        if not fine:
            yield
        yb = jnp.concatenate([_dot(dg, pw_ref[gi]) for gi, dg in enumerate(diffs)], axis=1) * ps_ref[...]
        mix = mix + jax.nn.sigmoid(gate_b) * yb

        conv_c = proj(5)
        if fine:
            yield
        conv_x = proj(3)
        if fine:
            yield
        conv_b = proj(4)
        if fine:
            yield
        gate_c = proj(8)
        yield
        z = conv_c * conv_x
        z0 = CONV_CARRY + r0
        zext_ref[z0:z0 + n, :] = z
        conv = cw_ref[CONV_K - 1:CONV_K, :] * z
        for k in range(CONV_K - 1):
            lag = CONV_K - 1 - k
            conv = conv + cw_ref[k:k + 1, :] * zext_ref[z0 - lag:z0 - lag + n, :]
        gated = (conv_b * conv).astype(BF16)
        yield
        yc = _dot(gated, wpc_ref[...])
        if fine:
            yield
        mix = mix + jax.nn.sigmoid(gate_c) * yc

        x_new = x + gt_ref[0] * _dot(mix.astype(BF16), wout_ref[...])
        o_ref[0, rs, :] = x_new
        _route_tile(x_new, sh2_ref[0], sc2_ref[0], n2_ref[...], rwt_ref[...], rb_ref[...],
                    hpk_ref.at[rs, :], idx_ref.at[:, rs], gw_ref.at[:, rs], rank_ref.at[:, rs], cnt_ref, carry_ref)

    sub = min(MIX_SUB_ROWS, tm)
    waiting = [rows_block(r0, sub) for r0 in range(0, tm, sub)]
    running = []
    while waiting or running:
        if waiting:
            running.append(waiting.pop(0))
        for block in list(running):
            if next(block, "done") == "done":
                running.remove(block)
    pext_ref[0:POOL_CARRY, :] = pext_ref[tm:tm + POOL_CARRY, :]
    zext_ref[0:CONV_CARRY, :] = zext_ref[tm:tm + CONV_CARRY, :]


def _mixer(x, pending, sh, sc, gt, sh2, sc2, n1, w_in, gng, ws, bs, wpa, pool_w, pool_scale, conv_w, wpc,
           w_out, n2, router_w, router_b):
    bsz, seq, d = x.shape
    n_tok = bsz * seq
    tm = min(MIX_ROWS if pending is None else MIX_ROWS_PENDING, seq)
    nj = seq // tm
    vec = pl.BlockSpec((1, 1, d), lambda b, j: (b, 0, 0))
    tile = pl.BlockSpec((1, tm, d), lambda b, j: (b, j, 0))
    per_tok = pl.BlockSpec((TOP_K, tm), lambda b, j: (0, b * nj + j))
    pending_specs, pending_args = [], []
    if pending is not None:
        g, gw, gt_prev = pending
        pending_specs = [pl.BlockSpec((TOP_K, tm, d // 2), lambda b, j: (0, b * nj + j, 0)),
                         pl.BlockSpec((TOP_K, tm), lambda b, j: (0, b * nj + j)), vec]
        pending_args = [g, gw, gt_prev]
    return pl.pallas_call(
        functools.partial(_mixer_kernel, pending=pending is not None),
        grid=(bsz, nj),
        in_specs=pending_specs + [
            tile, vec, vec, vec, vec, vec,
            _resident((1, d)),
            _resident(w_in.shape),
            _resident((1, d)),
            _resident(ws.shape),
            _resident((CHUNK, GMLP_HEADS)),
            _resident(wpa.shape),
            _resident(pool_w.shape),
            _resident((1, d)),
            _resident(conv_w.shape),
            _resident(wpc.shape),
            _resident(w_out.shape),
            _resident((1, d)), _resident((N_EXPERTS, d)), _resident((N_EXPERTS, 1)),
        ],
        out_specs=[
            tile,
            pl.BlockSpec((tm, d // 2), lambda b, j: (b * nj + j, 0)),
            per_tok, per_tok, per_tok,
            pl.BlockSpec((N_EXPERTS, 128), lambda b, j: (0, 0)),
        ],
        out_shape=[
            jax.ShapeDtypeStruct(x.shape, F32),
            jax.ShapeDtypeStruct((n_tok, d // 2), U32),
            jax.ShapeDtypeStruct((TOP_K, n_tok), I32),
            jax.ShapeDtypeStruct((TOP_K, n_tok), F32),
            jax.ShapeDtypeStruct((TOP_K, n_tok), I32),
            jax.ShapeDtypeStruct((N_EXPERTS, 128), I32),
        ],
        scratch_shapes=[
            pltpu.VMEM((tm, d), F32),
            pltpu.VMEM((POOL_CARRY + tm, d), F32),
            pltpu.VMEM((CONV_CARRY + tm, d), F32),
            pltpu.VMEM((N_EXPERTS, 128), F32),
        ],
        compiler_params=_params(2),
        name="mixer",
    )(*pending_args, x, sh, sc, gt, sh2, sc2, n1.reshape(1, d), w_in.astype(BF16), gng.reshape(1, d), ws, bs.T,
      wpa.astype(BF16), pool_w.astype(BF16), pool_scale.reshape(1, d), conv_w, wpc.astype(BF16),
      w_out.astype(BF16), n2.reshape(1, d), router_w.T.astype(BF16), router_b.reshape(N_EXPERTS, 1))


def _route_tile(x, sh, sc, n2, rwt, rb, hpk_ref, idx_ref, gw_ref, rank_ref, cnt_ref, carry_ref):
    tm = x.shape[0]

    h = _rms(x, n2) * (1.0 + sc) + sh
    hb = h.astype(BF16)
    hpk_ref[...] = _pack_bf16_pairs(h)

    logits = lax.dot_general(rwt, hb, (((1,), (1,)), ((), ())), preferred_element_type=F32) + rb
    iota_e = lax.broadcasted_iota(I32, logits.shape, 0)
    vals, idxs, sels = [], [], []
    rest = logits
    for _ in range(TOP_K):
        m = jnp.max(rest, axis=0, keepdims=True)
        ik = jnp.min(jnp.where(rest == m, iota_e, N_EXPERTS), axis=0, keepdims=True)
        sel = iota_e == ik
        rest = jnp.where(sel, -jnp.inf, rest)
        vals.append(m)
        idxs.append(ik)
        sels.append(sel)
    exps = [jnp.exp(v - vals[0]) for v in vals]
    denom = exps[0] + exps[1] + exps[2] + exps[3]

    chosen = jnp.logical_or(jnp.logical_or(sels[0], sels[1]), jnp.logical_or(sels[2], sels[3]))
    a = jnp.where(chosen, 1.0, 0.0)
    before = lax.broadcasted_iota(I32, (tm, tm), 0) < lax.broadcasted_iota(I32, (tm, tm), 1)
    prior = _dot(a.astype(BF16), jnp.where(before, 1.0, 0.0).astype(BF16)) + carry_ref[:, 0:1]
    for k in range(TOP_K):
        idx_ref[k:k + 1, :] = idxs[k]
        gw_ref[k:k + 1, :] = exps[k] / denom
        rank_ref[k:k + 1, :] = jnp.sum(jnp.where(sels[k], prior, 0.0), axis=0, keepdims=True).astype(I32)
    total = carry_ref[...] + jnp.sum(a, axis=1, keepdims=True)
    carry_ref[...] = total
    cnt_ref[...] = total.astype(I32)


def _dest_kernel(start_ref, idx_ref, rank_ref, o_ref):
    idx = idx_ref[...]
    base = jnp.zeros(idx.shape, I32)
    for e in range(N_EXPERTS):
        base = jnp.where(idx == e, start_ref[e], base)
    o_ref[...] = base + rank_ref[...]


def _dest(pad_start, idx, rank):
    n_tok = idx.shape[1]
    tc = min(DEST_COLS, n_tok)
    blk = pl.BlockSpec((TOP_K, tc), lambda i, s: (0, i))
    return pl.pallas_call(
        _dest_kernel,
        grid_spec=pltpu.PrefetchScalarGridSpec(
            num_scalar_prefetch=1, grid=(n_tok // tc,), in_specs=[blk, blk], out_specs=blk),
        out_shape=jax.ShapeDtypeStruct(idx.shape, I32),
        compiler_params=_params(1),
        name="dest",
    )(pad_start, idx, rank)


def _dispatch(hpk, dest, n_rows):
    n_tok, half = hpk.shape
    win = SC_WINDOW
    assert n_tok % (win * SC_WORKERS) == 0
    per_worker = n_tok // win // SC_WORKERS
    mesh = plsc.VectorSubcoreMesh(core_axis_name="c", subcore_axis_name="s")

    @pl.kernel(out_type=jax.ShapeDtypeStruct((n_rows, half), U32), mesh=mesh,
               scratch_types=[pltpu.VMEM((win, half), U32), pltpu.VMEM((TOP_K, win), I32),
                              pltpu.SemaphoreType.DMA((TOP_K + 1,))])
    def scatter_rows(hpk_hbm, dest_hbm, xs_hbm, xbuf, ibuf, sem):
        worker = lax.axis_index("c") * SC_SUBCORES + lax.axis_index("s")

        @pl.loop(0, per_worker)
        def _(j):
            t0 = pl.multiple_of((worker * per_worker + j) * win, win)
            rows_in = pltpu.async_copy(hpk_hbm.at[pl.ds(t0, win)], xbuf, sem.at[TOP_K])
            pltpu.sync_copy(dest_hbm.at[:, pl.ds(t0, win)], ibuf)
            rows_in.wait()
            scatters = [pltpu.async_copy(xbuf, xs_hbm.at[ibuf.at[k]], sem.at[k]) for k in range(TOP_K)]
            for scatter in scatters:
                scatter.wait()

    return scatter_rows(hpk, dest)


def _expert_kernel(be_ref, nv_ref, live_ref, first_ref, slot_ref, next_ref, xs_ref, wgu_hbm, bgu_ref, wd_hbm,
                   bd_ref, ys_ref, wgu_f32, wd_f32, wgu_bf, wd_bf, sem, *, layer):
    step = pl.program_id(0)
    tb = EXPERT_ROWS

    def weight_copies(expert, slot):
        return (pltpu.make_async_copy(wgu_hbm.at[layer, expert], wgu_f32.at[slot], sem.at[0, slot]),
                pltpu.make_async_copy(wd_hbm.at[layer, expert], wd_f32.at[slot], sem.at[1, slot]))

    @pl.when(step == 0)
    def _():
        for copy in weight_copies(be_ref[0], 0):
            copy.start()

    for sub in range(xs_ref.shape[0] // tb):
        i = step * (xs_ref.shape[0] // tb) + sub
        rows = slice(sub * tb, (sub + 1) * tb)

        @pl.when(i >= nv_ref[0])
        def _():
            ys_ref[rows, :] = jnp.zeros((tb, ys_ref.shape[1]), U32)

        @pl.when(first_ref[i] == 1)
        def _():
            slot = slot_ref[i]

            @pl.when(next_ref[i] >= 0)
            def _():
                for copy in weight_copies(next_ref[i], 1 - slot):
                    copy.start()

            for copy in weight_copies(be_ref[i], slot):
                copy.wait()
            wgu_bf[...] = wgu_f32[slot].astype(BF16)
            wd_bf[...] = wd_f32[slot].astype(BF16)

        @pl.when(i < nv_ref[0])
        def _():
            live = lax.broadcasted_iota(I32, (tb, 1), 0) < live_ref[i]
            xb = jnp.where(live, _unpack_bf16_pairs(xs_ref[rows, :]), 0.0).astype(BF16)
            gu = _dot(xb, wgu_bf[...]) + bgu_ref[layer, be_ref[i]]
            ff = gu.shape[1] // 2
            gate = jnp.minimum(gu[:, :ff], SWIGLU_LIMIT)
            up = jnp.clip(gu[:, ff:], -SWIGLU_LIMIT, SWIGLU_LIMIT)
            glu = gate * jax.nn.sigmoid(SWIGLU_ALPHA * gate)
            y = _dot(((up + 1.0) * glu).astype(BF16), wd_bf[...]) + bd_ref[layer, be_ref[i]]
            ys_ref[rows, :] = _pack_bf16_pairs(y)


def _experts(layer, plan, xs, w_gu, b_gu, w_down, b_down):
    n_rows, half = xs.shape
    depth, _, d, two_f = w_gu.shape
    step_rows = EXPERT_ROWS * EXPERT_BLOCKS_PER_STEP

    def rows(i, be, nv, *_):
        return (jnp.minimum(i, (nv[0] - 1) // EXPERT_BLOCKS_PER_STEP), 0)

    return pl.pallas_call(
        functools.partial(_expert_kernel, layer=layer),
        grid_spec=pltpu.PrefetchScalarGridSpec(
            num_scalar_prefetch=len(plan),
            grid=(n_rows // step_rows,),
            in_specs=[
                pl.BlockSpec((step_rows, half), rows),
                pl.BlockSpec(memory_space=pl.ANY),
                _resident((depth, N_EXPERTS, 1, two_f)),
                pl.BlockSpec(memory_space=pl.ANY),
                _resident((depth, N_EXPERTS, 1, d)),
            ],
            out_specs=pl.BlockSpec((step_rows, half), lambda i, *_: (i, 0)),
            scratch_shapes=[
                pltpu.VMEM((2, d, two_f), F32), pltpu.VMEM((2, two_f // 2, d), F32),
                pltpu.VMEM((d, two_f), BF16), pltpu.VMEM((two_f // 2, d), BF16),
                pltpu.SemaphoreType.DMA((2, 2)),
            ],
        ),
        out_shape=jax.ShapeDtypeStruct((n_rows, half), U32),
        compiler_params=_params(1),
        name="experts",
    )(*plan, xs, w_gu, b_gu.reshape(depth, N_EXPERTS, 1, two_f), w_down, b_down.reshape(depth, N_EXPERTS, 1, d))


def _gather_rows(dest, ys):
    n_tok = dest.shape[1]
    width = ys.shape[1]
    win = SC_WINDOW
    assert n_tok % (win * SC_WORKERS) == 0
    per_worker = n_tok // win // SC_WORKERS
    mesh = plsc.VectorSubcoreMesh(core_axis_name="c", subcore_axis_name="s")

    part = win // SC_GATHER_PARTS
    n_items = TOP_K * SC_GATHER_PARTS
    n_buf = SC_GATHER_BUFFERS

    @pl.kernel(out_type=jax.ShapeDtypeStruct((TOP_K, n_tok, width), ys.dtype), mesh=mesh,
               scratch_types=[pltpu.VMEM((n_buf, part, width), ys.dtype), pltpu.VMEM((TOP_K, win), I32),
                              pltpu.SemaphoreType.DMA((2, n_buf))])
    def gather(ys_hbm, idx_hbm, g_hbm, buf, ibuf, sem):
        worker = lax.axis_index("c") * SC_SUBCORES + lax.axis_index("s")

        @pl.loop(0, per_worker)
        def _(j):
            t0 = pl.multiple_of((worker * per_worker + j) * win, win)
            pltpu.sync_copy(idx_hbm.at[:, pl.ds(t0, win)], ibuf)

            def fetch(n):
                k, h = divmod(n, SC_GATHER_PARTS)
                return pltpu.async_copy(ys_hbm.at[ibuf.at[k, pl.ds(h * part, part)]], buf.at[n % n_buf],
                                        sem.at[0, n % n_buf])

            def store(n):
                k, h = divmod(n, SC_GATHER_PARTS)
                return pltpu.async_copy(buf.at[n % n_buf], g_hbm.at[k, pl.ds(t0 + h * part, part)],
                                        sem.at[1, n % n_buf])

            fetches = {n: fetch(n) for n in range(n_buf - 1)}
            stores = {}
            for n in range(n_items):
                fetches[n].wait()
                stores[n] = store(n)
                nxt = n + n_buf - 1
                if nxt < n_items:
                    if n >= 1:
                        stores.pop(n - 1).wait()
                    fetches[nxt] = fetch(nxt)
            for pending_store in stores.values():
                pending_store.wait()

    return gather(ys, dest)


def _final_kernel(g_ref, x_ref, gw_ref, gt_ref, fg_ref, *rest):
    o_ref = rest[-1]
    y = x_ref[...] + gt_ref[0] * _weighted_rows(g_ref, gw_ref[...].T)
    o_ref[...] = _rms(y, fg_ref[...])


def _final(g_parts, x, gw, gt, final_g):
    bsz, seq, d = x.shape
    n_tok = bsz * seq
    tm = min(FINAL_ROWS, seq)
    per_seq = seq // tm
    steps = n_tok // len(g_parts) // tm
    out = None
    for part, g in enumerate(g_parts):
        base = part * steps
        in_specs = [
            pl.BlockSpec((TOP_K, tm, d // 2), lambda i: (0, i, 0)),
            pl.BlockSpec((tm, d), lambda i, base=base: (base + i, 0)),
            pl.BlockSpec((TOP_K, tm), lambda i, base=base: (0, base + i)),
            pl.BlockSpec((1, 1, d), lambda i, base=base: ((base + i) // per_seq, 0, 0)),
            pl.BlockSpec((1, d), lambda i: (0, 0)),
        ]
        args = [g, x.reshape(n_tok, d), gw, gt, final_g.reshape(1, d)]
        aliases = {}
        if out is not None:
            in_specs.append(pl.BlockSpec(memory_space=pl.ANY))
            args.append(out)
            aliases = {len(args) - 1: 0}
        out = pl.pallas_call(
            _final_kernel,
            grid=(steps,),
            in_specs=in_specs,
            out_specs=pl.BlockSpec((tm, d), lambda i, base=base: (base + i, 0)),
            out_shape=jax.ShapeDtypeStruct((n_tok, d), F32),
            input_output_aliases=aliases,
            compiler_params=_params(1),
            name="final",
        )(*args)
    return out.reshape(bsz, seq, d)


def _plan_kernel(cnt_ref, start_ref, be_ref, nv_ref, live_ref, first_ref, slot_ref, next_ref):
    tb = EXPERT_ROWS
    n_blocks = be_ref.shape[0]

    def per_expert(e, carry):
        blk0, group = carry
        count = cnt_ref[e]
        n_blk = (count + (tb - 1)) // tb
        start_ref[e] = blk0 * tb

        def per_block(b, c):
            i = blk0 + b
            be_ref[i] = e
            live_ref[i] = jnp.minimum(count - b * tb, tb)
            first_ref[i] = jnp.where(b == 0, 1, 0)
            slot_ref[i] = group % 2
            return c

        lax.fori_loop(0, n_blk, per_block, 0)
        return blk0 + n_blk, group + jnp.where(n_blk > 0, 1, 0)

    used, _ = lax.fori_loop(0, N_EXPERTS, per_expert, (jnp.int32(0), jnp.int32(0)))
    nv_ref[0] = used

    def backwards(t, nxt):
        i = used - 1 - t
        after = be_ref[jnp.minimum(i + 1, used - 1)]
        nxt = jnp.where(jnp.logical_and(i < used - 1, after != be_ref[i]), after, nxt)
        next_ref[i] = nxt
        return nxt

    lax.fori_loop(0, used, backwards, jnp.int32(-1))
    last = be_ref[used - 1]

    def tail(i, c):
        be_ref[i] = last
        live_ref[i] = 0
        first_ref[i] = 0
        slot_ref[i] = 0
        next_ref[i] = -1
        return c

    lax.fori_loop(used, n_blocks, tail, 0)


def _plan(counts, n_blocks):
    smem = pl.BlockSpec(memory_space=pltpu.SMEM)
    per_block = jax.ShapeDtypeStruct((n_blocks,), I32)
    return pl.pallas_call(
        _plan_kernel,
        in_specs=[smem],
        out_specs=[smem] * 7,
        out_shape=[jax.ShapeDtypeStruct((N_EXPERTS,), I32), per_block, jax.ShapeDtypeStruct((1,), I32),
                   per_block, per_block, per_block, per_block],
        name="plan",
    )(counts)


def _moe(layer, routed, w_gu, b_gu, w_down, b_down, gather_parts):
    n_tok = routed[0].shape[0]
    tb = EXPERT_ROWS
    step_rows = tb * EXPERT_BLOCKS_PER_STEP
    n_rows = -(-(n_tok * TOP_K + N_EXPERTS * (tb - 1)) // step_rows) * step_rows
    hpk, idx, gw, rank, cnt = routed
    pad_start, *plan = _plan(cnt[:, 0], n_rows // tb)
    dest = _dest(pad_start, idx, rank)
    xs = _dispatch(hpk, dest, n_rows)
    ys = _experts(layer, plan, xs, w_gu, b_gu, w_down, b_down)
    per_part = n_tok // gather_parts
    return [_gather_rows(dest[:, p * per_part:(p + 1) * per_part], ys) for p in range(gather_parts)]


def kernel(x, c, norm1_g, ada_w, ada_b, w_in, gmlp_norm_g, gmlp_ws, gmlp_bs, w_proj_a, pool_w, pool_scale,
           conv_w, w_proj_c, w_out, norm2_g, router_w, router_b, exp_w_gu, exp_b_gu, exp_w_down,
           exp_b_down, final_g):
    depth = ada_w.shape[0]
    bsz, _, d = x.shape
    mods = _ada(c, ada_w, ada_b)
    pending = None
    for l in range(depth):
        sh1, sc1, g1, sh2, sc2, g2 = [mods[l, :, i * d:(i + 1) * d].reshape(bsz, 1, d) for i in range(6)]
        x, *routed = _mixer(x, pending, sh1, sc1, g1, sh2, sc2, norm1_g[l], w_in[l], gmlp_norm_g[l],
                            gmlp_ws[l], gmlp_bs[l], w_proj_a[l], pool_w[l], pool_scale[l], conv_w[l],
                            w_proj_c[l], w_out[l], norm2_g[l], router_w[l], router_b[l])
        last = l == depth - 1
        g_parts = _moe(l, routed, exp_w_gu, exp_b_gu, exp_w_down, exp_b_down, FINAL_PARTS if last else 1)
        pending = (g_parts[0], routed[2], g2)
    return _final(g_parts, x, routed[2], g2, final_g)
```

```python
import functools

import jax
import jax.numpy as jnp
from jax import lax
from jax.experimental import pallas as pl
from jax.experimental.pallas import tpu as pltpu
from jax.experimental.pallas import tpu_sc as plsc

F32 = jnp.float32
BF16 = jnp.bfloat16
I32 = jnp.int32
U32 = jnp.uint32

RMS_EPS = 1e-5
GMLP_HEADS = 8
CHUNK = 128
POOL_WINDOWS = (2, 4, 8, 16)
POOL_CARRY = 16
CONV_K = 3
CONV_CARRY = 8
N_EXPERTS = 32
TOP_K = 4
SWIGLU_LIMIT = 7.0
SWIGLU_ALPHA = 1.702

MIX_ROWS = 512
MIX_ROWS_PENDING = 512
MIX_SUB_ROWS = 256
DEST_COLS = 4096
EXPERT_ROWS = 512
EXPERT_BLOCKS_PER_STEP = 2
FINAL_ROWS = 512
BATCH_GROUPS = 2
SC_SUBCORES = 16
SC_WORKERS = 2 * SC_SUBCORES
SC_WINDOW = 128
SC_GATHER_PARTS = 2
SC_GATHER_BUFFERS = 3

VMEM_LIMIT = 58 * 1024 * 1024


def _rms(x, g):
    return x * lax.rsqrt(jnp.mean(x * x, axis=-1, keepdims=True) + RMS_EPS) * g


def _dot(a, b):
    return jnp.dot(a, b, preferred_element_type=F32)


def _params(n_axes, vmem=VMEM_LIMIT):
    return pltpu.CompilerParams(dimension_semantics=("arbitrary",) * n_axes, vmem_limit_bytes=vmem)


def _pack_bf16_pairs(x):
    half = x.shape[1] // 2
    bits = lax.bitcast_convert_type(x.astype(BF16).astype(F32), U32)
    return (bits[:, :half] >> 16) | bits[:, half:]


def _unpack_bf16_pairs(w):
    lo = lax.bitcast_convert_type(w << 16, F32)
    hi = lax.bitcast_convert_type(w & jnp.uint32(0xFFFF0000), F32)
    return jnp.concatenate([lo, hi], axis=1)


def _resident(shape):
    zeros = (0,) * len(shape)
    return pl.BlockSpec(shape, lambda *_: zeros, pipeline_mode=pl.Buffered(1))


def _ada_kernel(c_ref, w_ref, b_ref, o_ref):
    c = c_ref[...]
    cond = c * jax.nn.sigmoid(c)
    o_ref[0] = _dot(cond.astype(BF16), w_ref[0].astype(BF16)) + b_ref[0]


def _ada(c, ada_w, ada_b):
    depth, d, six_d = ada_w.shape
    bsz = c.shape[0]
    return pl.pallas_call(
        _ada_kernel,
        grid=(depth, six_d // d),
        in_specs=[
            pl.BlockSpec((bsz, d), lambda l, j: (0, 0)),
            pl.BlockSpec((1, d, d), lambda l, j: (l, 0, j)),
            pl.BlockSpec((1, 1, d), lambda l, j: (l, 0, j)),
        ],
        out_specs=pl.BlockSpec((1, bsz, d), lambda l, j: (l, 0, j)),
        out_shape=jax.ShapeDtypeStruct((depth, bsz, six_d), F32),
        compiler_params=_params(2),
        name="ada",
    )(c, ada_w, ada_b.reshape(depth, 1, six_d))


def _weighted_rows(g_ref, gw):
    acc = gw[:, 0:1] * _unpack_bf16_pairs(g_ref[0])
    for k in range(1, TOP_K):
        acc = acc + gw[:, k:k + 1] * _unpack_bf16_pairs(g_ref[k])
    return acc


def _mixer_kernel(*refs, pending):
    if pending:
        g_ref, gwp_ref, gtp_ref, *refs = refs
    (x_ref, sh_ref, sc_ref, gt_ref, sh2_ref, sc2_ref, n1_ref, win_ref, gng_ref, ws_ref, bst_ref,
     wpa_ref, pw_ref, ps_ref, cw_ref, wpc_ref, wout_ref, n2_ref, rwt_ref, rb_ref,
     o_ref, hpk_ref, idx_ref, gw_ref, rank_ref, cnt_ref,
     s_ref, pext_ref, zext_ref, carry_ref) = refs
    tm, d = x_ref.shape[1], x_ref.shape[2]
    j = pl.program_id(1)

    @pl.when(j == 0)
    def _():
        pext_ref[0:POOL_CARRY, :] = jnp.zeros((POOL_CARRY, d), F32)
        zext_ref[0:CONV_CARRY, :] = jnp.zeros((CONV_CARRY, d), F32)

    @pl.when(jnp.logical_and(pl.program_id(0) == 0, j == 0))
    def _():
        carry_ref[...] = jnp.zeros(carry_ref.shape, F32)

    hd = d // GMLP_HEADS
    gd = d // len(POOL_WINDOWS)
    row = lax.broadcasted_iota(I32, (CHUNK, CHUNK), 0)
    col = lax.broadcasted_iota(I32, (CHUNK, CHUNK), 1)
    spatial_w = [jnp.where(row >= col, ws_ref[h], 0.0).astype(BF16) for h in range(GMLP_HEADS)]

    fine = not pending

    def rows_block(r0, n):
        rs = slice(r0, r0 + n)
        x = x_ref[0, rs, :]
        if pending:
            x = x + gtp_ref[0] * _weighted_rows(g_ref.at[:, rs, :], gwp_ref[:, rs].T)
        hb = (_rms(x, n1_ref[...]) * (1.0 + sc_ref[0]) + sh_ref[0]).astype(BF16)

        def proj(c):
            return _dot(hb, win_ref[:, c * d:(c + 1) * d])

        u_raw = proj(0)
        if fine:
            yield
        v_raw = proj(1)
        yield
        u = jax.nn.gelu(u_raw)
        if fine:
            yield
        vb = _rms(jax.nn.gelu(v_raw), gng_ref[...]).astype(BF16)
        yield
        for h in range(GMLP_HEADS):
            bias = bst_ref[:, h:h + 1]
            for ci in range(n // CHUNK):
                blk = vb[ci * CHUNK:(ci + 1) * CHUNK, h * hd:(h + 1) * hd]
                s_ref[r0 + ci * CHUNK:r0 + (ci + 1) * CHUNK, h * hd:(h + 1) * hd] = _dot(spatial_w[h], blk) + bias
        if fine:
            yield
        gate_a = proj(6)
        yield
        ya = _dot((u * s_ref[rs, :]).astype(BF16), wpa_ref[...])
        if fine:
            yield
        mix = jax.nn.sigmoid(gate_a) * ya

        p = proj(2)
        if fine:
            yield
        gate_b = proj(7)
        yield
        p0 = POOL_CARRY + r0
        pext_ref[p0:p0 + n, :] = p
        pos1 = (j * tm + r0 + 1 + lax.broadcasted_iota(I32, (n, 1), 0))
        diffs = []
        for gi, w in enumerate(POOL_WINDOWS):
            cs = slice(gi * gd, (gi + 1) * gd)
            acc = p[:, cs]
            for k in range(1, w):
                acc = acc + pext_ref[p0 - k:p0 - k + n, cs]
            cnt = jnp.minimum(pos1, w).astype(F32)
            diffs.append((acc / cnt - p[:, cs]).astype(BF16))
            if fine:
                yield
        if not fine:
            yield
        yb = jnp.concatenate([_dot(dg, pw_ref[gi]) for gi, dg in enumerate(diffs)], axis=1) * ps_ref[...]
        mix = mix + jax.nn.sigmoid(gate_b) * yb

        conv_c = proj(5)
        if fine:
            yield
        conv_x = proj(3)
        if fine:
            yield
        conv_b = proj(4)
        if fine:
            yield
        gate_c = proj(8)
        yield
        z = conv_c * conv_x
        z0 = CONV_CARRY + r0
        zext_ref[z0:z0 + n, :] = z
        conv = cw_ref[CONV_K - 1:CONV_K, :] * z
        for k in range(CONV_K - 1):
            lag = CONV_K - 1 - k
            conv = conv + cw_ref[k:k + 1, :] * zext_ref[z0 - lag:z0 - lag + n, :]
        gated = (conv_b * conv).astype(BF16)
        yield
        yc = _dot(gated, wpc_ref[...])
        if fine:
            yield
        mix = mix + jax.nn.sigmoid(gate_c) * yc

        x_new = x + gt_ref[0] * _dot(mix.astype(BF16), wout_ref[...])
        o_ref[0, rs, :] = x_new
        _route_tile(x_new, sh2_ref[0], sc2_ref[0], n2_ref[...], rwt_ref[...], rb_ref[...],
                    hpk_ref.at[rs, :], idx_ref.at[:, rs], gw_ref.at[:, rs], rank_ref.at[:, rs], cnt_ref, carry_ref)

    sub = min(MIX_SUB_ROWS, tm)
    waiting = [rows_block(r0, sub) for r0 in range(0, tm, sub)]
    running = []
    while waiting or running:
        if waiting:
            running.append(waiting.pop(0))
        for block in list(running):
            if next(block, "done") == "done":
                running.remove(block)
    pext_ref[0:POOL_CARRY, :] = pext_ref[tm:tm + POOL_CARRY, :]
    zext_ref[0:CONV_CARRY, :] = zext_ref[tm:tm + CONV_CARRY, :]


def _mixer(x, x_b0, b0, bsz, pending, sh, sc, gt, sh2, sc2, n1, w_in, gng, ws, bs, wpa, pool_w, pool_scale, conv_w,
           wpc, w_out, n2, router_w, router_b):
    _, seq, d = x.shape
    n_tok = bsz * seq
    tm = min(MIX_ROWS if pending is None else MIX_ROWS_PENDING, seq)
    nj = seq // tm
    vec = pl.BlockSpec((1, 1, d), lambda b, j: (b0 + b, 0, 0))
    tile = pl.BlockSpec((1, tm, d), lambda b, j: (b, j, 0))
    tile_in = pl.BlockSpec((1, tm, d), lambda b, j: (x_b0 + b, j, 0))
    per_tok = pl.BlockSpec((TOP_K, tm), lambda b, j: (0, b * nj + j))
    pending_specs, pending_args = [], []
    if pending is not None:
        g, gw, gt_prev = pending
        pending_specs = [pl.BlockSpec((TOP_K, tm, d // 2), lambda b, j: (0, b * nj + j, 0)),
                         pl.BlockSpec((TOP_K, tm), lambda b, j: (0, b * nj + j)), vec]
        pending_args = [g, gw, gt_prev]
    return pl.pallas_call(
        functools.partial(_mixer_kernel, pending=pending is not None),
        grid=(bsz, nj),
        in_specs=pending_specs + [
            tile_in, vec, vec, vec, vec, vec,
            _resident((1, d)),
            _resident(w_in.shape),
            _resident((1, d)),
            _resident(ws.shape),
            _resident((CHUNK, GMLP_HEADS)),
            _resident(wpa.shape),
            _resident(pool_w.shape),
            _resident((1, d)),
            _resident(conv_w.shape),
            _resident(wpc.shape),
            _resident(w_out.shape),
            _resident((1, d)), _resident((N_EXPERTS, d)), _resident((N_EXPERTS, 1)),
        ],
        out_specs=[
            tile,
            pl.BlockSpec((tm, d // 2), lambda b, j: (b * nj + j, 0)),
            per_tok, per_tok, per_tok,
            pl.BlockSpec((N_EXPERTS, 128), lambda b, j: (0, 0)),
        ],
        out_shape=[
            jax.ShapeDtypeStruct((bsz, seq, d), F32),
            jax.ShapeDtypeStruct((n_tok, d // 2), U32),
            jax.ShapeDtypeStruct((TOP_K, n_tok), I32),
            jax.ShapeDtypeStruct((TOP_K, n_tok), F32),
            jax.ShapeDtypeStruct((TOP_K, n_tok), I32),
            jax.ShapeDtypeStruct((N_EXPERTS, 128), I32),
        ],
        scratch_shapes=[
            pltpu.VMEM((tm, d), F32),
            pltpu.VMEM((POOL_CARRY + tm, d), F32),
            pltpu.VMEM((CONV_CARRY + tm, d), F32),
            pltpu.VMEM((N_EXPERTS, 128), F32),
        ],
        compiler_params=_params(2),
        name="mixer",
    )(*pending_args, x, sh, sc, gt, sh2, sc2, n1.reshape(1, d), w_in.astype(BF16), gng.reshape(1, d), ws, bs.T,
      wpa.astype(BF16), pool_w.astype(BF16), pool_scale.reshape(1, d), conv_w, wpc.astype(BF16),
      w_out.astype(BF16), n2.reshape(1, d), router_w.T.astype(BF16), router_b.reshape(N_EXPERTS, 1))


def _route_tile(x, sh, sc, n2, rwt, rb, hpk_ref, idx_ref, gw_ref, rank_ref, cnt_ref, carry_ref):
    tm = x.shape[0]

    h = _rms(x, n2) * (1.0 + sc) + sh
    hb = h.astype(BF16)
    hpk_ref[...] = _pack_bf16_pairs(h)

    logits = lax.dot_general(rwt, hb, (((1,), (1,)), ((), ())), preferred_element_type=F32) + rb
    iota_e = lax.broadcasted_iota(I32, logits.shape, 0)
    vals, idxs, sels = [], [], []
    rest = logits
    for _ in range(TOP_K):
        m = jnp.max(rest, axis=0, keepdims=True)
        ik = jnp.min(jnp.where(rest == m, iota_e, N_EXPERTS), axis=0, keepdims=True)
        sel = iota_e == ik
        rest = jnp.where(sel, -jnp.inf, rest)
        vals.append(m)
        idxs.append(ik)
        sels.append(sel)
    exps = [jnp.exp(v - vals[0]) for v in vals]
    denom = exps[0] + exps[1] + exps[2] + exps[3]

    chosen = jnp.logical_or(jnp.logical_or(sels[0], sels[1]), jnp.logical_or(sels[2], sels[3]))
    a = jnp.where(chosen, 1.0, 0.0)
    before = lax.broadcasted_iota(I32, (tm, tm), 0) < lax.broadcasted_iota(I32, (tm, tm), 1)
    prior = _dot(a.astype(BF16), jnp.where(before, 1.0, 0.0).astype(BF16)) + carry_ref[:, 0:1]
    for k in range(TOP_K):
        idx_ref[k:k + 1, :] = idxs[k]
        gw_ref[k:k + 1, :] = exps[k] / denom
        rank_ref[k:k + 1, :] = jnp.sum(jnp.where(sels[k], prior, 0.0), axis=0, keepdims=True).astype(I32)
    total = carry_ref[...] + jnp.sum(a, axis=1, keepdims=True)
    carry_ref[...] = total
    cnt_ref[...] = total.astype(I32)


def _dest_kernel(start_ref, idx_ref, rank_ref, o_ref):
    idx = idx_ref[...]
    base = jnp.zeros(idx.shape, I32)
    for e in range(N_EXPERTS):
        base = jnp.where(idx == e, start_ref[e], base)
    o_ref[...] = base + rank_ref[...]


def _dest(pad_start, idx, rank):
    n_tok = idx.shape[1]
    tc = min(DEST_COLS, n_tok)
    blk = pl.BlockSpec((TOP_K, tc), lambda i, s: (0, i))
    return pl.pallas_call(
        _dest_kernel,
        grid_spec=pltpu.PrefetchScalarGridSpec(
            num_scalar_prefetch=1, grid=(n_tok // tc,), in_specs=[blk, blk], out_specs=blk),
        out_shape=jax.ShapeDtypeStruct(idx.shape, I32),
        compiler_params=_params(1),
        name="dest",
    )(pad_start, idx, rank)


def _dispatch(hpk, dest, n_rows):
    n_tok, half = hpk.shape
    win = SC_WINDOW
    assert n_tok % (win * SC_WORKERS) == 0
    per_worker = n_tok // win // SC_WORKERS
    mesh = plsc.VectorSubcoreMesh(core_axis_name="c", subcore_axis_name="s")

    @pl.kernel(out_type=jax.ShapeDtypeStruct((n_rows, half), U32), mesh=mesh,
               scratch_types=[pltpu.VMEM((win, half), U32), pltpu.VMEM((TOP_K, win), I32),
                              pltpu.SemaphoreType.DMA((TOP_K + 1,))])
    def scatter_rows(hpk_hbm, dest_hbm, xs_hbm, xbuf, ibuf, sem):
        worker = lax.axis_index("c") * SC_SUBCORES + lax.axis_index("s")

        @pl.loop(0, per_worker)
        def _(j):
            t0 = pl.multiple_of((worker * per_worker + j) * win, win)
            rows_in = pltpu.async_copy(hpk_hbm.at[pl.ds(t0, win)], xbuf, sem.at[TOP_K])
            pltpu.sync_copy(dest_hbm.at[:, pl.ds(t0, win)], ibuf)
            rows_in.wait()
            scatters = [pltpu.async_copy(xbuf, xs_hbm.at[ibuf.at[k]], sem.at[k]) for k in range(TOP_K)]
            for scatter in scatters:
                scatter.wait()

    return scatter_rows(hpk, dest)


def _expert_kernel(be_ref, nv_ref, live_ref, first_ref, slot_ref, next_ref, xs_ref, wgu_hbm, bgu_ref, wd_hbm,
                   bd_ref, ys_ref, wgu_f32, wd_f32, wgu_bf, wd_bf, sem, *, layer):
    step = pl.program_id(0)
    tb = EXPERT_ROWS

    def weight_copies(expert, slot):
        return (pltpu.make_async_copy(wgu_hbm.at[layer, expert], wgu_f32.at[slot], sem.at[0, slot]),
                pltpu.make_async_copy(wd_hbm.at[layer, expert], wd_f32.at[slot], sem.at[1, slot]))

    @pl.when(step == 0)
    def _():
        for copy in weight_copies(be_ref[0], 0):
            copy.start()

    for sub in range(xs_ref.shape[0] // tb):
        i = step * (xs_ref.shape[0] // tb) + sub
        rows = slice(sub * tb, (sub + 1) * tb)

        @pl.when(i >= nv_ref[0])
        def _():
            ys_ref[rows, :] = jnp.zeros((tb, ys_ref.shape[1]), U32)

        @pl.when(first_ref[i] == 1)
        def _():
            slot = slot_ref[i]

            @pl.when(next_ref[i] >= 0)
            def _():
                for copy in weight_copies(next_ref[i], 1 - slot):
                    copy.start()

            for copy in weight_copies(be_ref[i], slot):
                copy.wait()
            wgu_bf[...] = wgu_f32[slot].astype(BF16)
            wd_bf[...] = wd_f32[slot].astype(BF16)

        @pl.when(i < nv_ref[0])
        def _():
            live = lax.broadcasted_iota(I32, (tb, 1), 0) < live_ref[i]
            xb = jnp.where(live, _unpack_bf16_pairs(xs_ref[rows, :]), 0.0).astype(BF16)
            gu = _dot(xb, wgu_bf[...]) + bgu_ref[layer, be_ref[i]]
            ff = gu.shape[1] // 2
            gate = jnp.minimum(gu[:, :ff], SWIGLU_LIMIT)
            up = jnp.clip(gu[:, ff:], -SWIGLU_LIMIT, SWIGLU_LIMIT)
            glu = gate * jax.nn.sigmoid(SWIGLU_ALPHA * gate)
            y = _dot(((up + 1.0) * glu).astype(BF16), wd_bf[...]) + bd_ref[layer, be_ref[i]]
            ys_ref[rows, :] = _pack_bf16_pairs(y)


def _experts(layer, plan, xs, w_gu, b_gu, w_down, b_down):
    n_rows, half = xs.shape
    depth, _, d, two_f = w_gu.shape
    step_rows = EXPERT_ROWS * EXPERT_BLOCKS_PER_STEP

    def rows(i, be, nv, *_):
        return (jnp.minimum(i, (nv[0] - 1) // EXPERT_BLOCKS_PER_STEP), 0)

    return pl.pallas_call(
        functools.partial(_expert_kernel, layer=layer),
        grid_spec=pltpu.PrefetchScalarGridSpec(
            num_scalar_prefetch=len(plan),
            grid=(n_rows // step_rows,),
            in_specs=[
                pl.BlockSpec((step_rows, half), rows),
                pl.BlockSpec(memory_space=pl.ANY),
                _resident((depth, N_EXPERTS, 1, two_f)),
                pl.BlockSpec(memory_space=pl.ANY),
                _resident((depth, N_EXPERTS, 1, d)),
            ],
            out_specs=pl.BlockSpec((step_rows, half), lambda i, *_: (i, 0)),
            scratch_shapes=[
                pltpu.VMEM((2, d, two_f), F32), pltpu.VMEM((2, two_f // 2, d), F32),
                pltpu.VMEM((d, two_f), BF16), pltpu.VMEM((two_f // 2, d), BF16),
                pltpu.SemaphoreType.DMA((2, 2)),
            ],
        ),
        out_shape=jax.ShapeDtypeStruct((n_rows, half), U32),
        compiler_params=_params(1),
        name="experts",
    )(*plan, xs, w_gu, b_gu.reshape(depth, N_EXPERTS, 1, two_f), w_down, b_down.reshape(depth, N_EXPERTS, 1, d))


def _gather_rows(dest, ys):
    n_tok = dest.shape[1]
    width = ys.shape[1]
    win = SC_WINDOW
    assert n_tok % (win * SC_WORKERS) == 0
    per_worker = n_tok // win // SC_WORKERS
    mesh = plsc.VectorSubcoreMesh(core_axis_name="c", subcore_axis_name="s")

    part = win // SC_GATHER_PARTS
    n_items = TOP_K * SC_GATHER_PARTS
    n_buf = SC_GATHER_BUFFERS

    @pl.kernel(out_type=jax.ShapeDtypeStruct((TOP_K, n_tok, width), ys.dtype), mesh=mesh,
               scratch_types=[pltpu.VMEM((n_buf, part, width), ys.dtype), pltpu.VMEM((TOP_K, win), I32),
                              pltpu.SemaphoreType.DMA((2, n_buf))])
    def gather(ys_hbm, idx_hbm, g_hbm, buf, ibuf, sem):
        worker = lax.axis_index("c") * SC_SUBCORES + lax.axis_index("s")

        @pl.loop(0, per_worker)
        def _(j):
            t0 = pl.multiple_of((worker * per_worker + j) * win, win)
            pltpu.sync_copy(idx_hbm.at[:, pl.ds(t0, win)], ibuf)

            def fetch(n):
                k, h = divmod(n, SC_GATHER_PARTS)
                return pltpu.async_copy(ys_hbm.at[ibuf.at[k, pl.ds(h * part, part)]], buf.at[n % n_buf],
                                        sem.at[0, n % n_buf])

            def store(n):
                k, h = divmod(n, SC_GATHER_PARTS)
                return pltpu.async_copy(buf.at[n % n_buf], g_hbm.at[k, pl.ds(t0 + h * part, part)],
                                        sem.at[1, n % n_buf])

            fetches = {n: fetch(n) for n in range(n_buf - 1)}
            stores = {}
            for n in range(n_items):
                fetches[n].wait()
                stores[n] = store(n)
                nxt = n + n_buf - 1
                if nxt < n_items:
                    if n >= 1:
                        stores.pop(n - 1).wait()
                    fetches[nxt] = fetch(nxt)
            for pending_store in stores.values():
                pending_store.wait()

    return gather(ys, dest)


def _final_kernel(g_ref, x_ref, gw_ref, gt_ref, fg_ref, *rest):
    o_ref = rest[-1]
    y = x_ref[...] + gt_ref[0] * _weighted_rows(g_ref, gw_ref[...].T)
    o_ref[...] = _rms(y, fg_ref[...])


def _final(out, g, x, gw, gt, final_g, b0, batch):
    bsz, seq, d = x.shape
    tm = min(FINAL_ROWS, seq)
    per_seq = seq // tm
    base = b0 * per_seq
    in_specs = [
        pl.BlockSpec((TOP_K, tm, d // 2), lambda i: (0, i, 0)),
        pl.BlockSpec((tm, d), lambda i: (i, 0)),
        pl.BlockSpec((TOP_K, tm), lambda i: (0, i)),
        pl.BlockSpec((1, 1, d), lambda i: (b0 + i // per_seq, 0, 0)),
        pl.BlockSpec((1, d), lambda i: (0, 0)),
    ]
    args = [g, x.reshape(bsz * seq, d), gw, gt, final_g.reshape(1, d)]
    aliases = {}
    if out is not None:
        in_specs.append(pl.BlockSpec(memory_space=pl.ANY))
        args.append(out)
        aliases = {len(args) - 1: 0}
    return pl.pallas_call(
        _final_kernel,
        grid=(bsz * per_seq,),
        in_specs=in_specs,
        out_specs=pl.BlockSpec((tm, d), lambda i: (base + i, 0)),
        out_shape=jax.ShapeDtypeStruct((batch * seq, d), F32),
        input_output_aliases=aliases,
        compiler_params=_params(1),
        name="final",
    )(*args)


def _plan_kernel(cnt_ref, start_ref, be_ref, nv_ref, live_ref, first_ref, slot_ref, next_ref):
    tb = EXPERT_ROWS
    n_blocks = be_ref.shape[0]

    def per_expert(e, carry):
        blk0, group = carry
        count = cnt_ref[e]
        n_blk = (count + (tb - 1)) // tb
        start_ref[e] = blk0 * tb

        def per_block(b, c):
            i = blk0 + b
            be_ref[i] = e
            live_ref[i] = jnp.minimum(count - b * tb, tb)
            first_ref[i] = jnp.where(b == 0, 1, 0)
            slot_ref[i] = group % 2
            return c

        lax.fori_loop(0, n_blk, per_block, 0)
        return blk0 + n_blk, group + jnp.where(n_blk > 0, 1, 0)

    used, _ = lax.fori_loop(0, N_EXPERTS, per_expert, (jnp.int32(0), jnp.int32(0)))
    nv_ref[0] = used

    def backwards(t, nxt):
        i = used - 1 - t
        after = be_ref[jnp.minimum(i + 1, used - 1)]
        nxt = jnp.where(jnp.logical_and(i < used - 1, after != be_ref[i]), after, nxt)
        next_ref[i] = nxt
        return nxt

    lax.fori_loop(0, used, backwards, jnp.int32(-1))
    last = be_ref[used - 1]

    def tail(i, c):
        be_ref[i] = last
        live_ref[i] = 0
        first_ref[i] = 0
        slot_ref[i] = 0
        next_ref[i] = -1
        return c

    lax.fori_loop(used, n_blocks, tail, 0)


def _plan(counts, n_blocks):
    smem = pl.BlockSpec(memory_space=pltpu.SMEM)
    per_block = jax.ShapeDtypeStruct((n_blocks,), I32)
    return pl.pallas_call(
        _plan_kernel,
        in_specs=[smem],
        out_specs=[smem] * 7,
        out_shape=[jax.ShapeDtypeStruct((N_EXPERTS,), I32), per_block, jax.ShapeDtypeStruct((1,), I32),
                   per_block, per_block, per_block, per_block],
        name="plan",
    )(counts)


def _moe(layer, routed, w_gu, b_gu, w_down, b_down):
    n_tok = routed[0].shape[0]
    tb = EXPERT_ROWS
    step_rows = tb * EXPERT_BLOCKS_PER_STEP
    n_rows = -(-(n_tok * TOP_K + N_EXPERTS * (tb - 1)) // step_rows) * step_rows
    hpk, idx, gw, rank, cnt = routed
    pad_start, *plan = _plan(cnt[:, 0], n_rows // tb)
    dest = _dest(pad_start, idx, rank)
    xs = _dispatch(hpk, dest, n_rows)
    ys = _experts(layer, plan, xs, w_gu, b_gu, w_down, b_down)
    return _gather_rows(dest, ys)


def kernel(x, c, norm1_g, ada_w, ada_b, w_in, gmlp_norm_g, gmlp_ws, gmlp_bs, w_proj_a, pool_w, pool_scale,
           conv_w, w_proj_c, w_out, norm2_g, router_w, router_b, exp_w_gu, exp_b_gu, exp_w_down,
           exp_b_down, final_g):
    depth = ada_w.shape[0]
    bsz, seq, d = x.shape
    mods = _ada(c, ada_w, ada_b)
    group = bsz // BATCH_GROUPS
    streams = [x] * BATCH_GROUPS
    pending = [None] * BATCH_GROUPS
    g2 = None
    for l in range(depth):
        g2_prev = g2
        sh1, sc1, g1, sh2, sc2, g2 = [mods[l, :, i * d:(i + 1) * d].reshape(bsz, 1, d) for i in range(6)]
        routed = [None] * BATCH_GROUPS
        for h in range(BATCH_GROUPS):
            held = None if pending[h] is None else (*pending[h], g2_prev)
            streams[h], *routed[h] = _mixer(
                streams[h], h * group if l == 0 else 0, h * group, group, held, sh1, sc1, g1, sh2, sc2,
                norm1_g[l], w_in[l], gmlp_norm_g[l], gmlp_ws[l], gmlp_bs[l], w_proj_a[l], pool_w[l],
                pool_scale[l], conv_w[l], w_proj_c[l], w_out[l], norm2_g[l], router_w[l], router_b[l])
        for h in range(BATCH_GROUPS):
            pending[h] = (_moe(l, routed[h], exp_w_gu, exp_b_gu, exp_w_down, exp_b_down), routed[h][2])
    out = None
    for h in range(BATCH_GROUPS):
        out = _final(out, pending[h][0], streams[h], pending[h][1], g2, final_g, h * group, bsz)
    return out.reshape(bsz, seq, d)
```

```python
import functools

import jax
import jax.numpy as jnp
from jax import lax
from jax.experimental import pallas as pl
from jax.experimental.pallas import tpu as pltpu
from jax.experimental.pallas import tpu_sc as plsc

F32 = jnp.float32
BF16 = jnp.bfloat16
I32 = jnp.int32
U32 = jnp.uint32

RMS_EPS = 1e-5
GMLP_HEADS = 8
CHUNK = 128
POOL_WINDOWS = (2, 4, 8, 16)
POOL_CARRY = 16
CONV_K = 3
CONV_CARRY = 8
N_EXPERTS = 32
TOP_K = 4
SWIGLU_LIMIT = 7.0
SWIGLU_ALPHA = 1.702

MIX_ROWS = 512
MIX_ROWS_PENDING = 512
MIX_SUB_ROWS = 256
DEST_COLS = 4096
EXPERT_ROWS = 256
EXPERT_BLOCKS_PER_STEP = 4
FINAL_ROWS = 512
BATCH_GROUPS = 2
SC_SUBCORES = 16
SC_WORKERS = 2 * SC_SUBCORES
SC_WINDOW = 128
SC_GATHER_PARTS = 2
SC_GATHER_BUFFERS = 3

VMEM_LIMIT = 58 * 1024 * 1024


def _rms(x, g):
    return x * lax.rsqrt(jnp.mean(x * x, axis=-1, keepdims=True) + RMS_EPS) * g


def _dot(a, b):
    return jnp.dot(a, b, preferred_element_type=F32)


def _params(n_axes, vmem=VMEM_LIMIT):
    return pltpu.CompilerParams(dimension_semantics=("arbitrary",) * n_axes, vmem_limit_bytes=vmem)


def _pack_bf16_pairs(x):
    half = x.shape[1] // 2
    bits = lax.bitcast_convert_type(x.astype(BF16).astype(F32), U32)
    return (bits[:, :half] >> 16) | bits[:, half:]


def _unpack_bf16_pairs(w):
    lo = lax.bitcast_convert_type(w << 16, F32)
    hi = lax.bitcast_convert_type(w & jnp.uint32(0xFFFF0000), F32)
    return jnp.concatenate([lo, hi], axis=1)


def _resident(shape):
    zeros = (0,) * len(shape)
    return pl.BlockSpec(shape, lambda *_: zeros, pipeline_mode=pl.Buffered(1))


def _ada_kernel(c_ref, w_ref, b_ref, o_ref):
    c = c_ref[...]
    cond = c * jax.nn.sigmoid(c)
    o_ref[0] = _dot(cond.astype(BF16), w_ref[0].astype(BF16)) + b_ref[0]


def _ada(c, ada_w, ada_b):
    depth, d, six_d = ada_w.shape
    bsz = c.shape[0]
    return pl.pallas_call(
        _ada_kernel,
        grid=(depth, six_d // d),
        in_specs=[
            pl.BlockSpec((bsz, d), lambda l, j: (0, 0)),
            pl.BlockSpec((1, d, d), lambda l, j: (l, 0, j)),
            pl.BlockSpec((1, 1, d), lambda l, j: (l, 0, j)),
        ],
        out_specs=pl.BlockSpec((1, bsz, d), lambda l, j: (l, 0, j)),
        out_shape=jax.ShapeDtypeStruct((depth, bsz, six_d), F32),
        compiler_params=_params(2),
        name="ada",
    )(c, ada_w, ada_b.reshape(depth, 1, six_d))


def _weighted_rows(g_ref, gw):
    acc = gw[:, 0:1] * _unpack_bf16_pairs(g_ref[0])
    for k in range(1, TOP_K):
        acc = acc + gw[:, k:k + 1] * _unpack_bf16_pairs(g_ref[k])
    return acc


def _mixer_kernel(*refs, pending):
    if pending:
        g_ref, gwp_ref, gtp_ref, *refs = refs
    (x_ref, sh_ref, sc_ref, gt_ref, sh2_ref, sc2_ref, n1_ref, win_ref, gng_ref, ws_ref, bst_ref,
     wpa_ref, pw_ref, ps_ref, cw_ref, wpc_ref, wout_ref, n2_ref, rwt_ref, rb_ref,
     o_ref, hpk_ref, idx_ref, gw_ref, rank_ref, cnt_ref,
     s_ref, pext_ref, zext_ref, carry_ref) = refs
    tm, d = x_ref.shape[1], x_ref.shape[2]
    j = pl.program_id(1)

    @pl.when(j == 0)
    def _():
        pext_ref[0:POOL_CARRY, :] = jnp.zeros((POOL_CARRY, d), F32)
        zext_ref[0:CONV_CARRY, :] = jnp.zeros((CONV_CARRY, d), F32)

    @pl.when(jnp.logical_and(pl.program_id(0) == 0, j == 0))
    def _():
        carry_ref[...] = jnp.zeros(carry_ref.shape, F32)

    hd = d // GMLP_HEADS
    gd = d // len(POOL_WINDOWS)
    row = lax.broadcasted_iota(I32, (CHUNK, CHUNK), 0)
    col = lax.broadcasted_iota(I32, (CHUNK, CHUNK), 1)
    spatial_w = [jnp.where(row >= col, ws_ref[h], 0.0).astype(BF16) for h in range(GMLP_HEADS)]

    fine = not pending

    def rows_block(r0, n):
        rs = slice(r0, r0 + n)
        x = x_ref[0, rs, :]
        if pending:
            x = x + gtp_ref[0] * _weighted_rows(g_ref.at[:, rs, :], gwp_ref[:, rs].T)
        hb = (_rms(x, n1_ref[...]) * (1.0 + sc_ref[0]) + sh_ref[0]).astype(BF16)

        def proj(c):
            return _dot(hb, win_ref[:, c * d:(c + 1) * d])

        u_raw = proj(0)
        if fine:
            yield
        v_raw = proj(1)
        yield
        u = jax.nn.gelu(u_raw)
        if fine:
            yield
        vb = _rms(jax.nn.gelu(v_raw), gng_ref[...]).astype(BF16)
        yield
        for h in range(GMLP_HEADS):
            bias = bst_ref[:, h:h + 1]
            for ci in range(n // CHUNK):
                blk = vb[ci * CHUNK:(ci + 1) * CHUNK, h * hd:(h + 1) * hd]
                s_ref[r0 + ci * CHUNK:r0 + (ci + 1) * CHUNK, h * hd:(h + 1) * hd] = _dot(spatial_w[h], blk) + bias
        if fine:
            yield
        gate_a = proj(6)
        yield
        ya = _dot((u * s_ref[rs, :]).astype(BF16), wpa_ref[...])
        if fine:
            yield
        mix = jax.nn.sigmoid(gate_a) * ya

        p = proj(2)
        if fine:
            yield
        gate_b = proj(7)
        yield
        p0 = POOL_CARRY + r0
        pext_ref[p0:p0 + n, :] = p
        pos1 = (j * tm + r0 + 1 + lax.broadcasted_iota(I32, (n, 1), 0))
        diffs = []
        for gi, w in enumerate(POOL_WINDOWS):
            cs = slice(gi * gd, (gi + 1) * gd)
            acc = p[:, cs]
            for k in range(1, w):
                acc = acc + pext_ref[p0 - k:p0 - k + n, cs]
            cnt = jnp.minimum(pos1, w).astype(F32)
            diffs.append((acc / cnt - p[:, cs]).astype(BF16))
            if fine:
                yield
        if not fine:
            yield
        yb = jnp.concatenate([_dot(dg, pw_ref[gi]) for gi, dg in enumerate(diffs)], axis=1) * ps_ref[...]
        mix = mix + jax.nn.sigmoid(gate_b) * yb

        conv_c = proj(5)
        if fine:
            yield
        conv_x = proj(3)
        if fine:
            yield
        conv_b = proj(4)
        if fine:
            yield
        gate_c = proj(8)
        yield
        z = conv_c * conv_x
        z0 = CONV_CARRY + r0
        zext_ref[z0:z0 + n, :] = z
        conv = cw_ref[CONV_K - 1:CONV_K, :] * z
        for k in range(CONV_K - 1):
            lag = CONV_K - 1 - k
            conv = conv + cw_ref[k:k + 1, :] * zext_ref[z0 - lag:z0 - lag + n, :]
        gated = (conv_b * conv).astype(BF16)
        yield
        yc = _dot(gated, wpc_ref[...])
        if fine:
            yield
        mix = mix + jax.nn.sigmoid(gate_c) * yc

        x_new = x + gt_ref[0] * _dot(mix.astype(BF16), wout_ref[...])
        o_ref[0, rs, :] = x_new
        _route_tile(x_new, sh2_ref[0], sc2_ref[0], n2_ref[...], rwt_ref[...], rb_ref[...],
                    hpk_ref.at[rs, :], idx_ref.at[:, rs], gw_ref.at[:, rs], rank_ref.at[:, rs], cnt_ref, carry_ref)

    sub = min(MIX_SUB_ROWS, tm)
    waiting = [rows_block(r0, sub) for r0 in range(0, tm, sub)]
    running = []
    while waiting or running:
        if waiting:
            running.append(waiting.pop(0))
        for block in list(running):
            if next(block, "done") == "done":
                running.remove(block)
    pext_ref[0:POOL_CARRY, :] = pext_ref[tm:tm + POOL_CARRY, :]
    zext_ref[0:CONV_CARRY, :] = zext_ref[tm:tm + CONV_CARRY, :]


def _mixer(x, x_b0, b0, bsz, pending, sh, sc, gt, sh2, sc2, n1, w_in, gng, ws, bs, wpa, pool_w, pool_scale, conv_w,
           wpc, w_out, n2, router_w, router_b):
    _, seq, d = x.shape
    n_tok = bsz * seq
    tm = min(MIX_ROWS if pending is None else MIX_ROWS_PENDING, seq)
    nj = seq // tm
    vec = pl.BlockSpec((1, 1, d), lambda b, j: (b0 + b, 0, 0))
    tile = pl.BlockSpec((1, tm, d), lambda b, j: (b, j, 0))
    tile_in = pl.BlockSpec((1, tm, d), lambda b, j: (x_b0 + b, j, 0))
    per_tok = pl.BlockSpec((TOP_K, tm), lambda b, j: (0, b * nj + j))
    pending_specs, pending_args = [], []
    if pending is not None:
        g, gw, gt_prev = pending
        pending_specs = [pl.BlockSpec((TOP_K, tm, d // 2), lambda b, j: (0, b * nj + j, 0)),
                         pl.BlockSpec((TOP_K, tm), lambda b, j: (0, b * nj + j)), vec]
        pending_args = [g, gw, gt_prev]
    return pl.pallas_call(
        functools.partial(_mixer_kernel, pending=pending is not None),
        grid=(bsz, nj),
        in_specs=pending_specs + [
            tile_in, vec, vec, vec, vec, vec,
            _resident((1, d)),
            _resident(w_in.shape),
            _resident((1, d)),
            _resident(ws.shape),
            _resident((CHUNK, GMLP_HEADS)),
            _resident(wpa.shape),
            _resident(pool_w.shape),
            _resident((1, d)),
            _resident(conv_w.shape),
            _resident(wpc.shape),
            _resident(w_out.shape),
            _resident((1, d)), _resident((N_EXPERTS, d)), _resident((N_EXPERTS, 1)),
        ],
        out_specs=[
            tile,
            pl.BlockSpec((tm, d // 2), lambda b, j: (b * nj + j, 0)),
            per_tok, per_tok, per_tok,
            pl.BlockSpec((N_EXPERTS, 128), lambda b, j: (0, 0)),
        ],
        out_shape=[
            jax.ShapeDtypeStruct((bsz, seq, d), F32),
            jax.ShapeDtypeStruct((n_tok, d // 2), U32),
            jax.ShapeDtypeStruct((TOP_K, n_tok), I32),
            jax.ShapeDtypeStruct((TOP_K, n_tok), F32),
            jax.ShapeDtypeStruct((TOP_K, n_tok), I32),
            jax.ShapeDtypeStruct((N_EXPERTS, 128), I32),
        ],
        scratch_shapes=[
            pltpu.VMEM((tm, d), F32),
            pltpu.VMEM((POOL_CARRY + tm, d), F32),
            pltpu.VMEM((CONV_CARRY + tm, d), F32),
            pltpu.VMEM((N_EXPERTS, 128), F32),
        ],
        compiler_params=_params(2),
        name="mixer",
    )(*pending_args, x, sh, sc, gt, sh2, sc2, n1.reshape(1, d), w_in.astype(BF16), gng.reshape(1, d), ws, bs.T,
      wpa.astype(BF16), pool_w.astype(BF16), pool_scale.reshape(1, d), conv_w, wpc.astype(BF16),
      w_out.astype(BF16), n2.reshape(1, d), router_w.T.astype(BF16), router_b.reshape(N_EXPERTS, 1))


def _route_tile(x, sh, sc, n2, rwt, rb, hpk_ref, idx_ref, gw_ref, rank_ref, cnt_ref, carry_ref):
    tm = x.shape[0]

    h = _rms(x, n2) * (1.0 + sc) + sh
    hb = h.astype(BF16)
    hpk_ref[...] = _pack_bf16_pairs(h)

    logits = lax.dot_general(rwt, hb, (((1,), (1,)), ((), ())), preferred_element_type=F32) + rb
    iota_e = lax.broadcasted_iota(I32, logits.shape, 0)
    vals, idxs, sels = [], [], []
    rest = logits
    for _ in range(TOP_K):
        m = jnp.max(rest, axis=0, keepdims=True)
        ik = jnp.min(jnp.where(rest == m, iota_e, N_EXPERTS), axis=0, keepdims=True)
        sel = iota_e == ik
        rest = jnp.where(sel, -jnp.inf, rest)
        vals.append(m)
        idxs.append(ik)
        sels.append(sel)
    exps = [jnp.exp(v - vals[0]) for v in vals]
    denom = exps[0] + exps[1] + exps[2] + exps[3]

    chosen = jnp.logical_or(jnp.logical_or(sels[0], sels[1]), jnp.logical_or(sels[2], sels[3]))
    a = jnp.where(chosen, 1.0, 0.0)
    before = lax.broadcasted_iota(I32, (tm, tm), 0) < lax.broadcasted_iota(I32, (tm, tm), 1)
    prior = _dot(a.astype(BF16), jnp.where(before, 1.0, 0.0).astype(BF16)) + carry_ref[:, 0:1]
    for k in range(TOP_K):
        idx_ref[k:k + 1, :] = idxs[k]
        gw_ref[k:k + 1, :] = exps[k] / denom
        rank_ref[k:k + 1, :] = jnp.sum(jnp.where(sels[k], prior, 0.0), axis=0, keepdims=True).astype(I32)
    total = carry_ref[...] + jnp.sum(a, axis=1, keepdims=True)
    carry_ref[...] = total
    cnt_ref[...] = total.astype(I32)


def _dest_kernel(start_ref, idx_ref, rank_ref, o_ref):
    idx = idx_ref[...]
    base = jnp.zeros(idx.shape, I32)
    for e in range(N_EXPERTS):
        base = jnp.where(idx == e, start_ref[e], base)
    o_ref[...] = base + rank_ref[...]


def _dest(pad_start, idx, rank):
    n_tok = idx.shape[1]
    tc = min(DEST_COLS, n_tok)
    blk = pl.BlockSpec((TOP_K, tc), lambda i, s: (0, i))
    return pl.pallas_call(
        _dest_kernel,
        grid_spec=pltpu.PrefetchScalarGridSpec(
            num_scalar_prefetch=1, grid=(n_tok // tc,), in_specs=[blk, blk], out_specs=blk),
        out_shape=jax.ShapeDtypeStruct(idx.shape, I32),
        compiler_params=_params(1),
        name="dest",
    )(pad_start, idx, rank)


def _dispatch(hpk, dest, n_rows):
    n_tok, half = hpk.shape
    win = SC_WINDOW
    assert n_tok % (win * SC_WORKERS) == 0
    per_worker = n_tok // win // SC_WORKERS
    mesh = plsc.VectorSubcoreMesh(core_axis_name="c", subcore_axis_name="s")

    @pl.kernel(out_type=jax.ShapeDtypeStruct((n_rows, half), U32), mesh=mesh,
               scratch_types=[pltpu.VMEM((win, half), U32), pltpu.VMEM((TOP_K, win), I32),
                              pltpu.SemaphoreType.DMA((TOP_K + 1,))])
    def scatter_rows(hpk_hbm, dest_hbm, xs_hbm, xbuf, ibuf, sem):
        worker = lax.axis_index("c") * SC_SUBCORES + lax.axis_index("s")

        @pl.loop(0, per_worker)
        def _(j):
            t0 = pl.multiple_of((worker * per_worker + j) * win, win)
            rows_in = pltpu.async_copy(hpk_hbm.at[pl.ds(t0, win)], xbuf, sem.at[TOP_K])
            pltpu.sync_copy(dest_hbm.at[:, pl.ds(t0, win)], ibuf)
            rows_in.wait()
            scatters = [pltpu.async_copy(xbuf, xs_hbm.at[ibuf.at[k]], sem.at[k]) for k in range(TOP_K)]
            for scatter in scatters:
                scatter.wait()

    return scatter_rows(hpk, dest)


def _expert_kernel(be_ref, nv_ref, live_ref, first_ref, slot_ref, next_ref, xs_ref, wgu_hbm, bgu_ref, wd_hbm,
                   bd_ref, ys_ref, wgu_f32, wd_f32, wgu_bf, wd_bf, sem, *, layer):
    step = pl.program_id(0)
    tb = EXPERT_ROWS

    def weight_copies(expert, slot):
        return (pltpu.make_async_copy(wgu_hbm.at[layer, expert], wgu_f32.at[slot], sem.at[0, slot]),
                pltpu.make_async_copy(wd_hbm.at[layer, expert], wd_f32.at[slot], sem.at[1, slot]))

    @pl.when(step == 0)
    def _():
        for copy in weight_copies(be_ref[0], 0):
            copy.start()

    for sub in range(xs_ref.shape[0] // tb):
        i = step * (xs_ref.shape[0] // tb) + sub
        rows = slice(sub * tb, (sub + 1) * tb)

        @pl.when(i >= nv_ref[0])
        def _():
            ys_ref[rows, :] = jnp.zeros((tb, ys_ref.shape[1]), U32)

        @pl.when(first_ref[i] == 1)
        def _():
            slot = slot_ref[i]

            @pl.when(next_ref[i] >= 0)
            def _():
                for copy in weight_copies(next_ref[i], 1 - slot):
                    copy.start()

            for copy in weight_copies(be_ref[i], slot):
                copy.wait()
            wgu_bf[...] = wgu_f32[slot].astype(BF16)
            wd_bf[...] = wd_f32[slot].astype(BF16)

        @pl.when(i < nv_ref[0])
        def _():
            live = lax.broadcasted_iota(I32, (tb, 1), 0) < live_ref[i]
            xb = jnp.where(live, _unpack_bf16_pairs(xs_ref[rows, :]), 0.0).astype(BF16)
            gu = _dot(xb, wgu_bf[...]) + bgu_ref[layer, be_ref[i]]
            ff = gu.shape[1] // 2
            gate = jnp.minimum(gu[:, :ff], SWIGLU_LIMIT)
            up = jnp.clip(gu[:, ff:], -SWIGLU_LIMIT, SWIGLU_LIMIT)
            glu = gate * jax.nn.sigmoid(SWIGLU_ALPHA * gate)
            y = _dot(((up + 1.0) * glu).astype(BF16), wd_bf[...]) + bd_ref[layer, be_ref[i]]
            ys_ref[rows, :] = _pack_bf16_pairs(y)


def _experts(layer, plan, xs, w_gu, b_gu, w_down, b_down):
    n_rows, half = xs.shape
    depth, _, d, two_f = w_gu.shape
    step_rows = EXPERT_ROWS * EXPERT_BLOCKS_PER_STEP

    def rows(i, be, nv, *_):
        return (jnp.minimum(i, (nv[0] - 1) // EXPERT_BLOCKS_PER_STEP), 0)

    return pl.pallas_call(
        functools.partial(_expert_kernel, layer=layer),
        grid_spec=pltpu.PrefetchScalarGridSpec(
            num_scalar_prefetch=len(plan),
            grid=(n_rows // step_rows,),
            in_specs=[
                pl.BlockSpec((step_rows, half), rows),
                pl.BlockSpec(memory_space=pl.ANY),
                _resident((depth, N_EXPERTS, 1, two_f)),
                pl.BlockSpec(memory_space=pl.ANY),
                _resident((depth, N_EXPERTS, 1, d)),
            ],
            out_specs=pl.BlockSpec((step_rows, half), lambda i, *_: (i, 0)),
            scratch_shapes=[
                pltpu.VMEM((2, d, two_f), F32), pltpu.VMEM((2, two_f // 2, d), F32),
                pltpu.VMEM((d, two_f), BF16), pltpu.VMEM((two_f // 2, d), BF16),
                pltpu.SemaphoreType.DMA((2, 2)),
            ],
        ),
        out_shape=jax.ShapeDtypeStruct((n_rows, half), U32),
        compiler_params=_params(1),
        name="experts",
    )(*plan, xs, w_gu, b_gu.reshape(depth, N_EXPERTS, 1, two_f), w_down, b_down.reshape(depth, N_EXPERTS, 1, d))


def _gather_rows(dest, ys):
    n_tok = dest.shape[1]
    width = ys.shape[1]
    win = SC_WINDOW
    assert n_tok % (win * SC_WORKERS) == 0
    per_worker = n_tok // win // SC_WORKERS
    mesh = plsc.VectorSubcoreMesh(core_axis_name="c", subcore_axis_name="s")

    part = win // SC_GATHER_PARTS
    n_items = TOP_K * SC_GATHER_PARTS
    n_buf = SC_GATHER_BUFFERS

    @pl.kernel(out_type=jax.ShapeDtypeStruct((TOP_K, n_tok, width), ys.dtype), mesh=mesh,
               scratch_types=[pltpu.VMEM((n_buf, part, width), ys.dtype), pltpu.VMEM((TOP_K, win), I32),
                              pltpu.SemaphoreType.DMA((2, n_buf))])
    def gather(ys_hbm, idx_hbm, g_hbm, buf, ibuf, sem):
        worker = lax.axis_index("c") * SC_SUBCORES + lax.axis_index("s")

        @pl.loop(0, per_worker)
        def _(j):
            t0 = pl.multiple_of((worker * per_worker + j) * win, win)
            pltpu.sync_copy(idx_hbm.at[:, pl.ds(t0, win)], ibuf)

            def fetch(n):
                k, h = divmod(n, SC_GATHER_PARTS)
                return pltpu.async_copy(ys_hbm.at[ibuf.at[k, pl.ds(h * part, part)]], buf.at[n % n_buf],
                                        sem.at[0, n % n_buf])

            def store(n):
                k, h = divmod(n, SC_GATHER_PARTS)
                return pltpu.async_copy(buf.at[n % n_buf], g_hbm.at[k, pl.ds(t0 + h * part, part)],
                                        sem.at[1, n % n_buf])

            fetches = {n: fetch(n) for n in range(n_buf - 1)}
            stores = {}
            for n in range(n_items):
                fetches[n].wait()
                stores[n] = store(n)
                nxt = n + n_buf - 1
                if nxt < n_items:
                    if n >= 1:
                        stores.pop(n - 1).wait()
                    fetches[nxt] = fetch(nxt)
            for pending_store in stores.values():
                pending_store.wait()

    return gather(ys, dest)


def _final_kernel(g_ref, x_ref, gw_ref, gt_ref, fg_ref, *rest):
    o_ref = rest[-1]
    y = x_ref[...] + gt_ref[0] * _weighted_rows(g_ref, gw_ref[...].T)
    o_ref[...] = _rms(y, fg_ref[...])


def _final(out, g, x, gw, gt, final_g, b0, batch):
    bsz, seq, d = x.shape
    tm = min(FINAL_ROWS, seq)
    per_seq = seq // tm
    base = b0 * per_seq
    in_specs = [
        pl.BlockSpec((TOP_K, tm, d // 2), lambda i: (0, i, 0)),
        pl.BlockSpec((tm, d), lambda i: (i, 0)),
        pl.BlockSpec((TOP_K, tm), lambda i: (0, i)),
        pl.BlockSpec((1, 1, d), lambda i: (b0 + i // per_seq, 0, 0)),
        pl.BlockSpec((1, d), lambda i: (0, 0)),
    ]
    args = [g, x.reshape(bsz * seq, d), gw, gt, final_g.reshape(1, d)]
    aliases = {}
    if out is not None:
        in_specs.append(pl.BlockSpec(memory_space=pl.ANY))
        args.append(out)
        aliases = {len(args) - 1: 0}
    return pl.pallas_call(
        _final_kernel,
        grid=(bsz * per_seq,),
        in_specs=in_specs,
        out_specs=pl.BlockSpec((tm, d), lambda i: (base + i, 0)),
        out_shape=jax.ShapeDtypeStruct((batch * seq, d), F32),
        input_output_aliases=aliases,
        compiler_params=_params(1),
        name="final",
    )(*args)


def _plan_kernel(cnt_ref, start_ref, be_ref, nv_ref, live_ref, first_ref, slot_ref, next_ref):
    tb = EXPERT_ROWS
    n_blocks = be_ref.shape[0]

    def per_expert(e, carry):
        blk0, group = carry
        count = cnt_ref[e]
        n_blk = (count + (tb - 1)) // tb
        start_ref[e] = blk0 * tb

        def per_block(b, c):
            i = blk0 + b
            be_ref[i] = e
            live_ref[i] = jnp.minimum(count - b * tb, tb)
            first_ref[i] = jnp.where(b == 0, 1, 0)
            slot_ref[i] = group % 2
            return c

        lax.fori_loop(0, n_blk, per_block, 0)
        return blk0 + n_blk, group + jnp.where(n_blk > 0, 1, 0)

    used, _ = lax.fori_loop(0, N_EXPERTS, per_expert, (jnp.int32(0), jnp.int32(0)))
    nv_ref[0] = used

    def backwards(t, nxt):
        i = used - 1 - t
        after = be_ref[jnp.minimum(i + 1, used - 1)]
        nxt = jnp.where(jnp.logical_and(i < used - 1, after != be_ref[i]), after, nxt)
        next_ref[i] = nxt
        return nxt

    lax.fori_loop(0, used, backwards, jnp.int32(-1))
    last = be_ref[used - 1]

    def tail(i, c):
        be_ref[i] = last
        live_ref[i] = 0
        first_ref[i] = 0
        slot_ref[i] = 0
        next_ref[i] = -1
        return c

    lax.fori_loop(used, n_blocks, tail, 0)


def _plan(counts, n_blocks):
    smem = pl.BlockSpec(memory_space=pltpu.SMEM)
    per_block = jax.ShapeDtypeStruct((n_blocks,), I32)
    return pl.pallas_call(
        _plan_kernel,
        in_specs=[smem],
        out_specs=[smem] * 7,
        out_shape=[jax.ShapeDtypeStruct((N_EXPERTS,), I32), per_block, jax.ShapeDtypeStruct((1,), I32),
                   per_block, per_block, per_block, per_block],
        name="plan",
    )(counts)


def _moe(layer, routed, w_gu, b_gu, w_down, b_down):
    n_tok = routed[0].shape[0]
    tb = EXPERT_ROWS
    step_rows = tb * EXPERT_BLOCKS_PER_STEP
    n_rows = -(-(n_tok * TOP_K + N_EXPERTS * (tb - 1)) // step_rows) * step_rows
    hpk, idx, gw, rank, cnt = routed
    pad_start, *plan = _plan(cnt[:, 0], n_rows // tb)
    dest = _dest(pad_start, idx, rank)
    xs = _dispatch(hpk, dest, n_rows)
    ys = _experts(layer, plan, xs, w_gu, b_gu, w_down, b_down)
    return _gather_rows(dest, ys)


def kernel(x, c, norm1_g, ada_w, ada_b, w_in, gmlp_norm_g, gmlp_ws, gmlp_bs, w_proj_a, pool_w, pool_scale,
           conv_w, w_proj_c, w_out, norm2_g, router_w, router_b, exp_w_gu, exp_b_gu, exp_w_down,
           exp_b_down, final_g):
    depth = ada_w.shape[0]
    bsz, seq, d = x.shape
    mods = _ada(c, ada_w, ada_b)
    group = bsz // BATCH_GROUPS
    streams = [x] * BATCH_GROUPS
    pending = [None] * BATCH_GROUPS
    g2 = None
    for l in range(depth):
        g2_prev = g2
        sh1, sc1, g1, sh2, sc2, g2 = [mods[l, :, i * d:(i + 1) * d].reshape(bsz, 1, d) for i in range(6)]
        routed = [None] * BATCH_GROUPS
        for h in range(BATCH_GROUPS):
            held = None if pending[h] is None else (*pending[h], g2_prev)
            streams[h], *routed[h] = _mixer(
                streams[h], h * group if l == 0 else 0, h * group, group, held, sh1, sc1, g1, sh2, sc2,
                norm1_g[l], w_in[l], gmlp_norm_g[l], gmlp_ws[l], gmlp_bs[l], w_proj_a[l], pool_w[l],
                pool_scale[l], conv_w[l], w_proj_c[l], w_out[l], norm2_g[l], router_w[l], router_b[l])
        for h in range(BATCH_GROUPS):
            pending[h] = (_moe(l, routed[h], exp_w_gu, exp_b_gu, exp_w_down, exp_b_down), routed[h][2])
    out = None
    for h in range(BATCH_GROUPS):
        out = _final(out, pending[h][0], streams[h], pending[h][1], g2, final_g, h * group, bsz)
    return out.reshape(bsz, seq, d)
```

```python
import functools

import jax
import jax.numpy as jnp
from jax import lax
from jax.experimental import pallas as pl
from jax.experimental.pallas import tpu as pltpu
from jax.experimental.pallas import tpu_sc as plsc

F32 = jnp.float32
BF16 = jnp.bfloat16
I32 = jnp.int32
U32 = jnp.uint32

RMS_EPS = 1e-5
GMLP_HEADS = 8
CHUNK = 128
POOL_WINDOWS = (2, 4, 8, 16)
POOL_CARRY = 16
CONV_K = 3
CONV_CARRY = 8
N_EXPERTS = 32
TOP_K = 4
SWIGLU_LIMIT = 7.0
SWIGLU_ALPHA = 1.702

MIX_ROWS = 512
MIX_SUB_ROWS = 256
DEST_COLS = 4096
EXPERT_ROWS = 256
EXPERT_BLOCKS_PER_STEP = 4
FINAL_ROWS = 512
BATCH_GROUPS = 2
SC_SUBCORES = 16
SC_WORKERS = 2 * SC_SUBCORES
SC_WINDOW = 128
SC_GATHER_PARTS = 2
SC_GATHER_BUFFERS = 3

VMEM_LIMIT = 58 * 1024 * 1024


def _rms(x, g):
    return x * lax.rsqrt(jnp.mean(x * x, axis=-1, keepdims=True) + RMS_EPS) * g


def _dot(a, b):
    return jnp.dot(a, b, preferred_element_type=F32)


def _params(n_axes, vmem=VMEM_LIMIT):
    return pltpu.CompilerParams(dimension_semantics=("arbitrary",) * n_axes, vmem_limit_bytes=vmem)


def _pack_bf16_pairs(x):
    half = x.shape[1] // 2
    bits = lax.bitcast_convert_type(x.astype(BF16).astype(F32), U32)
    return (bits[:, :half] >> 16) | bits[:, half:]


def _unpack_bf16_pairs(w):
    lo = lax.bitcast_convert_type(w << 16, F32)
    hi = lax.bitcast_convert_type(w & jnp.uint32(0xFFFF0000), F32)
    return jnp.concatenate([lo, hi], axis=1)


def _resident(shape):
    zeros = (0,) * len(shape)
    return pl.BlockSpec(shape, lambda *_: zeros, pipeline_mode=pl.Buffered(1))


def _ada_kernel(c_ref, w_ref, b_ref, o_ref):
    c = c_ref[...]
    cond = c * jax.nn.sigmoid(c)
    o_ref[0] = _dot(cond.astype(BF16), w_ref[0].astype(BF16)) + b_ref[0]


def _ada(c, ada_w, ada_b):
    depth, d, six_d = ada_w.shape
    bsz = c.shape[0]
    return pl.pallas_call(
        _ada_kernel,
        grid=(depth, six_d // d),
        in_specs=[
            pl.BlockSpec((bsz, d), lambda l, j: (0, 0)),
            pl.BlockSpec((1, d, d), lambda l, j: (l, 0, j)),
            pl.BlockSpec((1, 1, d), lambda l, j: (l, 0, j)),
        ],
        out_specs=pl.BlockSpec((1, bsz, d), lambda l, j: (l, 0, j)),
        out_shape=jax.ShapeDtypeStruct((depth, bsz, six_d), F32),
        compiler_params=_params(2),
        name="ada",
    )(c, ada_w, ada_b.reshape(depth, 1, six_d))


def _weighted_rows(g_ref, gw):
    acc = gw[:, 0:1] * _unpack_bf16_pairs(g_ref[0])
    for k in range(1, TOP_K):
        acc = acc + gw[:, k:k + 1] * _unpack_bf16_pairs(g_ref[k])
    return acc


def _mixer_kernel(*refs, pending):
    if pending:
        g_ref, gwp_ref, gtp_ref, *refs = refs
    (x_ref, sh_ref, sc_ref, gt_ref, sh2_ref, sc2_ref, n1_ref, win_ref, gng_ref, ws_ref, bst_ref,
     wpa_ref, pw_ref, ps_ref, cw_ref, wpc_ref, wout_ref, n2_ref, rwt_ref, rb_ref,
     o_ref, hpk_ref, idx_ref, gw_ref, rank_ref, cnt_ref,
     s_ref, pext_ref, zext_ref, carry_ref) = refs
    tm, d = x_ref.shape[1], x_ref.shape[2]
    j = pl.program_id(1)

    @pl.when(j == 0)
    def _():
        pext_ref[0:POOL_CARRY, :] = jnp.zeros((POOL_CARRY, d), F32)
        zext_ref[0:CONV_CARRY, :] = jnp.zeros((CONV_CARRY, d), F32)

    @pl.when(jnp.logical_and(pl.program_id(0) == 0, j == 0))
    def _():
        carry_ref[...] = jnp.zeros(carry_ref.shape, F32)

    hd = d // GMLP_HEADS
    gd = d // len(POOL_WINDOWS)
    row = lax.broadcasted_iota(I32, (CHUNK, CHUNK), 0)
    col = lax.broadcasted_iota(I32, (CHUNK, CHUNK), 1)
    spatial_w = [jnp.where(row >= col, ws_ref[h], 0.0).astype(BF16) for h in range(GMLP_HEADS)]

    fine = not pending

    def rows_block(r0, n):
        rs = slice(r0, r0 + n)
        x = x_ref[0, rs, :]
        if pending:
            x = x + gtp_ref[0] * _weighted_rows(g_ref.at[:, rs, :], gwp_ref[:, rs].T)
        hb = (_rms(x, n1_ref[...]) * (1.0 + sc_ref[0]) + sh_ref[0]).astype(BF16)

        def proj(c):
            return _dot(hb, win_ref[:, c * d:(c + 1) * d])

        u_raw = proj(0)
        if fine:
            yield
        v_raw = proj(1)
        yield
        u = jax.nn.gelu(u_raw)
        if fine:
            yield
        vb = _rms(jax.nn.gelu(v_raw), gng_ref[...]).astype(BF16)
        yield
        for h in range(GMLP_HEADS):
            bias = bst_ref[:, h:h + 1]
            for ci in range(n // CHUNK):
                blk = vb[ci * CHUNK:(ci + 1) * CHUNK, h * hd:(h + 1) * hd]
                s_ref[r0 + ci * CHUNK:r0 + (ci + 1) * CHUNK, h * hd:(h + 1) * hd] = _dot(spatial_w[h], blk) + bias
        if fine:
            yield
        gate_a = proj(6)
        yield
        ya = _dot((u * s_ref[rs, :]).astype(BF16), wpa_ref[...])
        if fine:
            yield
        mix = jax.nn.sigmoid(gate_a) * ya

        p = proj(2)
        if fine:
            yield
        gate_b = proj(7)
        yield
        p0 = POOL_CARRY + r0
        pext_ref[p0:p0 + n, :] = p
        pos1 = (j * tm + r0 + 1 + lax.broadcasted_iota(I32, (n, 1), 0))
        diffs = []
        for gi, w in enumerate(POOL_WINDOWS):
            cs = slice(gi * gd, (gi + 1) * gd)
            acc = p[:, cs]
            for k in range(1, w):
                acc = acc + pext_ref[p0 - k:p0 - k + n, cs]
            cnt = jnp.minimum(pos1, w).astype(F32)
            diffs.append((acc / cnt - p[:, cs]).astype(BF16))
            if fine:
                yield
        if not fine:
            yield
        yb = jnp.concatenate([_dot(dg, pw_ref[gi]) for gi, dg in enumerate(diffs)], axis=1) * ps_ref[...]
        mix = mix + jax.nn.sigmoid(gate_b) * yb

        conv_c = proj(5)
        if fine:
            yield
        conv_x = proj(3)
        if fine:
            yield
        conv_b = proj(4)
        if fine:
            yield
        gate_c = proj(8)
        yield
        z = conv_c * conv_x
        z0 = CONV_CARRY + r0
        zext_ref[z0:z0 + n, :] = z
        conv = cw_ref[CONV_K - 1:CONV_K, :] * z
        for k in range(CONV_K - 1):
            lag = CONV_K - 1 - k
            conv = conv + cw_ref[k:k + 1, :] * zext_ref[z0 - lag:z0 - lag + n, :]
        gated = (conv_b * conv).astype(BF16)
        yield
        yc = _dot(gated, wpc_ref[...])
        if fine:
            yield
        mix = mix + jax.nn.sigmoid(gate_c) * yc

        x_new = x + gt_ref[0] * _dot(mix.astype(BF16), wout_ref[...])
        o_ref[0, rs, :] = x_new
        _route_tile(x_new, sh2_ref[0], sc2_ref[0], n2_ref[...], rwt_ref[...], rb_ref[...],
                    hpk_ref.at[rs, :], idx_ref.at[:, rs], gw_ref.at[:, rs], rank_ref.at[:, rs], cnt_ref, carry_ref)

    sub = min(MIX_SUB_ROWS, tm)
    waiting = [rows_block(r0, sub) for r0 in range(0, tm, sub)]
    running = []
    while waiting or running:
        if waiting:
            running.append(waiting.pop(0))
        for block in list(running):
            if next(block, "done") == "done":
                running.remove(block)
    pext_ref[0:POOL_CARRY, :] = pext_ref[tm:tm + POOL_CARRY, :]
    zext_ref[0:CONV_CARRY, :] = zext_ref[tm:tm + CONV_CARRY, :]


def _mixer(x, x_b0, b0, bsz, pending, sh, sc, gt, sh2, sc2, n1, w_in, gng, ws, bs, wpa, pool_w, pool_scale, conv_w,
           wpc, w_out, n2, router_w, router_b):
    _, seq, d = x.shape
    n_tok = bsz * seq
    tm = min(MIX_ROWS, seq)
    nj = seq // tm
    vec = pl.BlockSpec((1, 1, d), lambda b, j: (b0 + b, 0, 0))
    tile = pl.BlockSpec((1, tm, d), lambda b, j: (b, j, 0))
    tile_in = pl.BlockSpec((1, tm, d), lambda b, j: (x_b0 + b, j, 0))
    per_tok = pl.BlockSpec((TOP_K, tm), lambda b, j: (0, b * nj + j))
    pending_specs, pending_args = [], []
    if pending is not None:
        g, gw, gt_prev = pending
        pending_specs = [pl.BlockSpec((TOP_K, tm, d // 2), lambda b, j: (0, b * nj + j, 0)),
                         pl.BlockSpec((TOP_K, tm), lambda b, j: (0, b * nj + j)), vec]
        pending_args = [g, gw, gt_prev]
    return pl.pallas_call(
        functools.partial(_mixer_kernel, pending=pending is not None),
        grid=(bsz, nj),
        in_specs=pending_specs + [
            tile_in, vec, vec, vec, vec, vec,
            _resident((1, d)),
            _resident(w_in.shape),
            _resident((1, d)),
            _resident(ws.shape),
            _resident((CHUNK, GMLP_HEADS)),
            _resident(wpa.shape),
            _resident(pool_w.shape),
            _resident((1, d)),
            _resident(conv_w.shape),
            _resident(wpc.shape),
            _resident(w_out.shape),
            _resident((1, d)), _resident((N_EXPERTS, d)), _resident((N_EXPERTS, 1)),
        ],
        out_specs=[
            tile,
            pl.BlockSpec((tm, d // 2), lambda b, j: (b * nj + j, 0)),
            per_tok, per_tok, per_tok,
            pl.BlockSpec((N_EXPERTS, 128), lambda b, j: (0, 0)),
        ],
        out_shape=[
            jax.ShapeDtypeStruct((bsz, seq, d), F32),
            jax.ShapeDtypeStruct((n_tok, d // 2), U32),
            jax.ShapeDtypeStruct((TOP_K, n_tok), I32),
            jax.ShapeDtypeStruct((TOP_K, n_tok), F32),
            jax.ShapeDtypeStruct((TOP_K, n_tok), I32),
            jax.ShapeDtypeStruct((N_EXPERTS, 128), I32),
        ],
        scratch_shapes=[
            pltpu.VMEM((tm, d), F32),
            pltpu.VMEM((POOL_CARRY + tm, d), F32),
            pltpu.VMEM((CONV_CARRY + tm, d), F32),
            pltpu.VMEM((N_EXPERTS, 128), F32),
        ],
        compiler_params=_params(2),
        name="mixer",
    )(*pending_args, x, sh, sc, gt, sh2, sc2, n1.reshape(1, d), w_in.astype(BF16), gng.reshape(1, d), ws, bs.T,
      wpa.astype(BF16), pool_w.astype(BF16), pool_scale.reshape(1, d), conv_w, wpc.astype(BF16),
      w_out.astype(BF16), n2.reshape(1, d), router_w.T.astype(BF16), router_b.reshape(N_EXPERTS, 1))


def _route_tile(x, sh, sc, n2, rwt, rb, hpk_ref, idx_ref, gw_ref, rank_ref, cnt_ref, carry_ref):
    tm = x.shape[0]

    h = _rms(x, n2) * (1.0 + sc) + sh
    hb = h.astype(BF16)
    hpk_ref[...] = _pack_bf16_pairs(h)

    logits = lax.dot_general(rwt, hb, (((1,), (1,)), ((), ())), preferred_element_type=F32) + rb
    iota_e = lax.broadcasted_iota(I32, logits.shape, 0)
    vals, idxs, sels = [], [], []
    rest = logits
    for _ in range(TOP_K):
        m = jnp.max(rest, axis=0, keepdims=True)
        ik = jnp.min(jnp.where(rest == m, iota_e, N_EXPERTS), axis=0, keepdims=True)
        sel = iota_e == ik
        rest = jnp.where(sel, -jnp.inf, rest)
        vals.append(m)
        idxs.append(ik)
        sels.append(sel)
    exps = [jnp.exp(v - vals[0]) for v in vals]
    denom = exps[0] + exps[1] + exps[2] + exps[3]

    chosen = jnp.logical_or(jnp.logical_or(sels[0], sels[1]), jnp.logical_or(sels[2], sels[3]))
    a = jnp.where(chosen, 1.0, 0.0)
    before = lax.broadcasted_iota(I32, (tm, tm), 0) < lax.broadcasted_iota(I32, (tm, tm), 1)
    prior = _dot(a.astype(BF16), jnp.where(before, 1.0, 0.0).astype(BF16)) + carry_ref[:, 0:1]
    for k in range(TOP_K):
        idx_ref[k:k + 1, :] = idxs[k]
        gw_ref[k:k + 1, :] = exps[k] / denom
        rank_ref[k:k + 1, :] = jnp.sum(jnp.where(sels[k], prior, 0.0), axis=0, keepdims=True).astype(I32)
    total = carry_ref[...] + jnp.sum(a, axis=1, keepdims=True)
    carry_ref[...] = total
    cnt_ref[...] = total.astype(I32)


def _dest_kernel(start_ref, idx_ref, rank_ref, o_ref):
    idx = idx_ref[...]
    base = jnp.zeros(idx.shape, I32)
    for e in range(N_EXPERTS):
        base = jnp.where(idx == e, start_ref[e], base)
    o_ref[...] = base + rank_ref[...]


def _dest(pad_start, idx, rank):
    n_tok = idx.shape[1]
    tc = min(DEST_COLS, n_tok)
    blk = pl.BlockSpec((TOP_K, tc), lambda i, s: (0, i))
    return pl.pallas_call(
        _dest_kernel,
        grid_spec=pltpu.PrefetchScalarGridSpec(
            num_scalar_prefetch=1, grid=(n_tok // tc,), in_specs=[blk, blk], out_specs=blk),
        out_shape=jax.ShapeDtypeStruct(idx.shape, I32),
        compiler_params=_params(1),
        name="dest",
    )(pad_start, idx, rank)


def _dispatch(hpk, dest, n_rows):
    n_tok, half = hpk.shape
    win = SC_WINDOW
    assert n_tok % (win * SC_WORKERS) == 0
    per_worker = n_tok // win // SC_WORKERS
    mesh = plsc.VectorSubcoreMesh(core_axis_name="c", subcore_axis_name="s")

    @pl.kernel(out_type=jax.ShapeDtypeStruct((n_rows, half), U32), mesh=mesh,
               scratch_types=[pltpu.VMEM((win, half), U32), pltpu.VMEM((TOP_K, win), I32),
                              pltpu.SemaphoreType.DMA((TOP_K + 1,))])
    def scatter_rows(hpk_hbm, dest_hbm, xs_hbm, xbuf, ibuf, sem):
        worker = lax.axis_index("c") * SC_SUBCORES + lax.axis_index("s")

        @pl.loop(0, per_worker)
        def _(j):
            t0 = pl.multiple_of((worker * per_worker + j) * win, win)
            rows_in = pltpu.async_copy(hpk_hbm.at[pl.ds(t0, win)], xbuf, sem.at[TOP_K])
            pltpu.sync_copy(dest_hbm.at[:, pl.ds(t0, win)], ibuf)
            rows_in.wait()
            scatters = [pltpu.async_copy(xbuf, xs_hbm.at[ibuf.at[k]], sem.at[k]) for k in range(TOP_K)]
            for scatter in scatters:
                scatter.wait()

    return scatter_rows(hpk, dest)


def _expert_kernel(be_ref, nv_ref, live_ref, first_ref, slot_ref, next_ref, xs_ref, wgu_hbm, bgu_ref, wd_hbm,
                   bd_ref, ys_ref, wgu_f32, wd_f32, wgu_bf, wd_bf, sem, *, layer):
    step = pl.program_id(0)
    tb = EXPERT_ROWS

    def weight_copies(expert, slot):
        return (pltpu.make_async_copy(wgu_hbm.at[layer, expert], wgu_f32.at[slot], sem.at[0, slot]),
                pltpu.make_async_copy(wd_hbm.at[layer, expert], wd_f32.at[slot], sem.at[1, slot]))

    @pl.when(step == 0)
    def _():
        for copy in weight_copies(be_ref[0], 0):
            copy.start()

    for sub in range(xs_ref.shape[0] // tb):
        i = step * (xs_ref.shape[0] // tb) + sub
        rows = slice(sub * tb, (sub + 1) * tb)

        @pl.when(i >= nv_ref[0])
        def _():
            ys_ref[rows, :] = jnp.zeros((tb, ys_ref.shape[1]), U32)

        @pl.when(first_ref[i] == 1)
        def _():
            slot = slot_ref[i]

            @pl.when(next_ref[i] >= 0)
            def _():
                for copy in weight_copies(next_ref[i], 1 - slot):
                    copy.start()

            for copy in weight_copies(be_ref[i], slot):
                copy.wait()
            wgu_bf[...] = wgu_f32[slot].astype(BF16)
            wd_bf[...] = wd_f32[slot].astype(BF16)

        @pl.when(i < nv_ref[0])
        def _():
            live = lax.broadcasted_iota(I32, (tb, 1), 0) < live_ref[i]
            xb = jnp.where(live, _unpack_bf16_pairs(xs_ref[rows, :]), 0.0).astype(BF16)
            gu = _dot(xb, wgu_bf[...]) + bgu_ref[layer, be_ref[i]]
            ff = gu.shape[1] // 2
            gate = jnp.minimum(gu[:, :ff], SWIGLU_LIMIT)
            up = jnp.clip(gu[:, ff:], -SWIGLU_LIMIT, SWIGLU_LIMIT)
            glu = gate * jax.nn.sigmoid(SWIGLU_ALPHA * gate)
            y = _dot(((up + 1.0) * glu).astype(BF16), wd_bf[...]) + bd_ref[layer, be_ref[i]]
            ys_ref[rows, :] = _pack_bf16_pairs(y)


def _experts(layer, plan, xs, w_gu, b_gu, w_down, b_down):
    n_rows, half = xs.shape
    depth, _, d, two_f = w_gu.shape
    step_rows = EXPERT_ROWS * EXPERT_BLOCKS_PER_STEP

    def rows(i, be, nv, *_):
        return (jnp.minimum(i, (nv[0] - 1) // EXPERT_BLOCKS_PER_STEP), 0)

    return pl.pallas_call(
        functools.partial(_expert_kernel, layer=layer),
        grid_spec=pltpu.PrefetchScalarGridSpec(
            num_scalar_prefetch=len(plan),
            grid=(n_rows // step_rows,),
            in_specs=[
                pl.BlockSpec((step_rows, half), rows),
                pl.BlockSpec(memory_space=pl.ANY),
                _resident((depth, N_EXPERTS, 1, two_f)),
                pl.BlockSpec(memory_space=pl.ANY),
                _resident((depth, N_EXPERTS, 1, d)),
            ],
            out_specs=pl.BlockSpec((step_rows, half), lambda i, *_: (i, 0)),
            scratch_shapes=[
                pltpu.VMEM((2, d, two_f), F32), pltpu.VMEM((2, two_f // 2, d), F32),
                pltpu.VMEM((d, two_f), BF16), pltpu.VMEM((two_f // 2, d), BF16),
                pltpu.SemaphoreType.DMA((2, 2)),
            ],
        ),
        out_shape=jax.ShapeDtypeStruct((n_rows, half), U32),
        compiler_params=_params(1),
        name="experts",
    )(*plan, xs, w_gu, b_gu.reshape(depth, N_EXPERTS, 1, two_f), w_down, b_down.reshape(depth, N_EXPERTS, 1, d))


def _gather_rows(dest, ys):
    n_tok = dest.shape[1]
    width = ys.shape[1]
    win = SC_WINDOW
    assert n_tok % (win * SC_WORKERS) == 0
    per_worker = n_tok // win // SC_WORKERS
    mesh = plsc.VectorSubcoreMesh(core_axis_name="c", subcore_axis_name="s")

    part = win // SC_GATHER_PARTS
    n_items = TOP_K * SC_GATHER_PARTS
    n_buf = SC_GATHER_BUFFERS

    @pl.kernel(out_type=jax.ShapeDtypeStruct((TOP_K, n_tok, width), ys.dtype), mesh=mesh,
               scratch_types=[pltpu.VMEM((n_buf, part, width), ys.dtype), pltpu.VMEM((TOP_K, win), I32),
                              pltpu.SemaphoreType.DMA((2, n_buf))])
    def gather(ys_hbm, idx_hbm, g_hbm, buf, ibuf, sem):
        worker = lax.axis_index("c") * SC_SUBCORES + lax.axis_index("s")

        @pl.loop(0, per_worker)
        def _(j):
            t0 = pl.multiple_of((worker * per_worker + j) * win, win)
            pltpu.sync_copy(idx_hbm.at[:, pl.ds(t0, win)], ibuf)

            def fetch(n):
                k, h = divmod(n, SC_GATHER_PARTS)
                return pltpu.async_copy(ys_hbm.at[ibuf.at[k, pl.ds(h * part, part)]], buf.at[n % n_buf],
                                        sem.at[0, n % n_buf])

            def store(n):
                k, h = divmod(n, SC_GATHER_PARTS)
                return pltpu.async_copy(buf.at[n % n_buf], g_hbm.at[k, pl.ds(t0 + h * part, part)],
                                        sem.at[1, n % n_buf])

            fetches = {n: fetch(n) for n in range(n_buf - 1)}
            stores = {}
            for n in range(n_items):
                fetches[n].wait()
                stores[n] = store(n)
                nxt = n + n_buf - 1
                if nxt < n_items:
                    if n >= 1:
                        stores.pop(n - 1).wait()
                    fetches[nxt] = fetch(nxt)
            for pending_store in stores.values():
                pending_store.wait()

    return gather(ys, dest)


def _final_kernel(g_ref, x_ref, gw_ref, gt_ref, fg_ref, *rest):
    o_ref = rest[-1]
    y = x_ref[...] + gt_ref[0] * _weighted_rows(g_ref, gw_ref[...].T)
    o_ref[...] = _rms(y, fg_ref[...])


def _final(out, g, x, gw, gt, final_g, b0, batch):
    bsz, seq, d = x.shape
    tm = min(FINAL_ROWS, seq)
    per_seq = seq // tm
    base = b0 * per_seq
    in_specs = [
        pl.BlockSpec((TOP_K, tm, d // 2), lambda i: (0, i, 0)),
        pl.BlockSpec((tm, d), lambda i: (i, 0)),
        pl.BlockSpec((TOP_K, tm), lambda i: (0, i)),
        pl.BlockSpec((1, 1, d), lambda i: (b0 + i // per_seq, 0, 0)),
        pl.BlockSpec((1, d), lambda i: (0, 0)),
    ]
    args = [g, x.reshape(bsz * seq, d), gw, gt, final_g.reshape(1, d)]
    aliases = {}
    if out is not None:
        in_specs.append(pl.BlockSpec(memory_space=pl.ANY))
        args.append(out)
        aliases = {len(args) - 1: 0}
    return pl.pallas_call(
        _final_kernel,
        grid=(bsz * per_seq,),
        in_specs=in_specs,
        out_specs=pl.BlockSpec((tm, d), lambda i: (base + i, 0)),
        out_shape=jax.ShapeDtypeStruct((batch * seq, d), F32),
        input_output_aliases=aliases,
        compiler_params=_params(1),
        name="final",
    )(*args)


def _plan_kernel(cnt_ref, start_ref, be_ref, nv_ref, live_ref, first_ref, slot_ref, next_ref):
    tb = EXPERT_ROWS
    n_blocks = be_ref.shape[0]

    def per_expert(e, carry):
        blk0, group = carry
        count = cnt_ref[e]
        n_blk = (count + (tb - 1)) // tb
        start_ref[e] = blk0 * tb

        def per_block(b, c):
            i = blk0 + b
            be_ref[i] = e
            live_ref[i] = jnp.minimum(count - b * tb, tb)
            first_ref[i] = jnp.where(b == 0, 1, 0)
            slot_ref[i] = group % 2
            return c

        lax.fori_loop(0, n_blk, per_block, 0)
        return blk0 + n_blk, group + jnp.where(n_blk > 0, 1, 0)

    used, _ = lax.fori_loop(0, N_EXPERTS, per_expert, (jnp.int32(0), jnp.int32(0)))
    nv_ref[0] = used

    def backwards(t, nxt):
        i = used - 1 - t
        after = be_ref[jnp.minimum(i + 1, used - 1)]
        nxt = jnp.where(jnp.logical_and(i < used - 1, after != be_ref[i]), after, nxt)
        next_ref[i] = nxt
        return nxt

    lax.fori_loop(0, used, backwards, jnp.int32(-1))
    last = be_ref[used - 1]

    def tail(i, c):
        be_ref[i] = last
        live_ref[i] = 0
        first_ref[i] = 0
        slot_ref[i] = 0
        next_ref[i] = -1
        return c

    lax.fori_loop(used, n_blocks, tail, 0)


def _plan(counts, n_blocks):
    smem = pl.BlockSpec(memory_space=pltpu.SMEM)
    per_block = jax.ShapeDtypeStruct((n_blocks,), I32)
    return pl.pallas_call(
        _plan_kernel,
        in_specs=[smem],
        out_specs=[smem] * 7,
        out_shape=[jax.ShapeDtypeStruct((N_EXPERTS,), I32), per_block, jax.ShapeDtypeStruct((1,), I32),
                   per_block, per_block, per_block, per_block],
        name="plan",
    )(counts)


def _moe(layer, routed, w_gu, b_gu, w_down, b_down):
    n_tok = routed[0].shape[0]
    tb = EXPERT_ROWS
    step_rows = tb * EXPERT_BLOCKS_PER_STEP
    n_rows = -(-(n_tok * TOP_K + N_EXPERTS * (tb - 1)) // step_rows) * step_rows
    hpk, idx, gw, rank, cnt = routed
    pad_start, *plan = _plan(cnt[:, 0], n_rows // tb)
    dest = _dest(pad_start, idx, rank)
    xs = _dispatch(hpk, dest, n_rows)
    ys = _experts(layer, plan, xs, w_gu, b_gu, w_down, b_down)
    return _gather_rows(dest, ys)


def kernel(x, c, norm1_g, ada_w, ada_b, w_in, gmlp_norm_g, gmlp_ws, gmlp_bs, w_proj_a, pool_w, pool_scale,
           conv_w, w_proj_c, w_out, norm2_g, router_w, router_b, exp_w_gu, exp_b_gu, exp_w_down,
           exp_b_down, final_g):
    depth = ada_w.shape[0]
    bsz, seq, d = x.shape
    mods = _ada(c, ada_w, ada_b)
    group = bsz // BATCH_GROUPS
    streams = [x] * BATCH_GROUPS
    pending = [None] * BATCH_GROUPS
    g2 = None
    for l in range(depth):
        g2_prev = g2
        sh1, sc1, g1, sh2, sc2, g2 = [mods[l, :, i * d:(i + 1) * d].reshape(bsz, 1, d) for i in range(6)]
        routed = [None] * BATCH_GROUPS
        for h in range(BATCH_GROUPS):
            held = None if pending[h] is None else (*pending[h], g2_prev)
            streams[h], *routed[h] = _mixer(
                streams[h], h * group if l == 0 else 0, h * group, group, held, sh1, sc1, g1, sh2, sc2,
                norm1_g[l], w_in[l], gmlp_norm_g[l], gmlp_ws[l], gmlp_bs[l], w_proj_a[l], pool_w[l],
                pool_scale[l], conv_w[l], w_proj_c[l], w_out[l], norm2_g[l], router_w[l], router_b[l])
        for h in range(BATCH_GROUPS):
            pending[h] = (_moe(l, routed[h], exp_w_gu, exp_b_gu, exp_w_down, exp_b_down), routed[h][2])
    out = None
    for h in range(BATCH_GROUPS):
        out = _final(out, pending[h][0], streams[h], pending[h][1], g2, final_g, h * group, bsz)
    return out.reshape(bsz, seq, d)
```

```python
import functools

import jax
import jax.numpy as jnp
from jax import lax
from jax.experimental import pallas as pl
from jax.experimental.pallas import tpu as pltpu
from jax.experimental.pallas import tpu_sc as plsc

F32 = jnp.float32
BF16 = jnp.bfloat16
I32 = jnp.int32
U32 = jnp.uint32

RMS_EPS = 1e-5
GMLP_HEADS = 8
CHUNK = 128
POOL_WINDOWS = (2, 4, 8, 16)
POOL_CARRY = 16
CONV_K = 3
CONV_CARRY = 8
N_EXPERTS = 32
TOP_K = 4
SWIGLU_LIMIT = 7.0
SWIGLU_ALPHA = 1.702

MIX_ROWS = 512
MIX_SUB_ROWS = 256
DEST_COLS = 4096
EXPERT_ROWS = 256
EXPERT_BLOCKS_PER_STEP = 8
FINAL_ROWS = 512
BATCH_GROUPS = 2
SC_SUBCORES = 16
SC_WORKERS = 2 * SC_SUBCORES
SC_WINDOW = 128
SC_GATHER_PARTS = 2
SC_GATHER_BUFFERS = 3

VMEM_LIMIT = 58 * 1024 * 1024


def _rms(x, g):
    return x * lax.rsqrt(jnp.mean(x * x, axis=-1, keepdims=True) + RMS_EPS) * g


def _dot(a, b):
    return jnp.dot(a, b, preferred_element_type=F32)


def _params(n_axes, vmem=VMEM_LIMIT):
    return pltpu.CompilerParams(dimension_semantics=("arbitrary",) * n_axes, vmem_limit_bytes=vmem)


def _pack_bf16_pairs(x):
    half = x.shape[1] // 2
    bits = lax.bitcast_convert_type(x.astype(BF16).astype(F32), U32)
    return (bits[:, :half] >> 16) | bits[:, half:]


def _unpack_bf16_pairs(w):
    lo = lax.bitcast_convert_type(w << 16, F32)
    hi = lax.bitcast_convert_type(w & jnp.uint32(0xFFFF0000), F32)
    return jnp.concatenate([lo, hi], axis=1)


def _resident(shape):
    zeros = (0,) * len(shape)
    return pl.BlockSpec(shape, lambda *_: zeros, pipeline_mode=pl.Buffered(1))


def _ada_kernel(c_ref, w_ref, b_ref, o_ref):
    c = c_ref[...]
    cond = c * jax.nn.sigmoid(c)
    o_ref[0] = _dot(cond.astype(BF16), w_ref[0].astype(BF16)) + b_ref[0]


def _ada(c, ada_w, ada_b):
    depth, d, six_d = ada_w.shape
    bsz = c.shape[0]
    return pl.pallas_call(
        _ada_kernel,
        grid=(depth, six_d // d),
        in_specs=[
            pl.BlockSpec((bsz, d), lambda l, j: (0, 0)),
            pl.BlockSpec((1, d, d), lambda l, j: (l, 0, j)),
            pl.BlockSpec((1, 1, d), lambda l, j: (l, 0, j)),
        ],
        out_specs=pl.BlockSpec((1, bsz, d), lambda l, j: (l, 0, j)),
        out_shape=jax.ShapeDtypeStruct((depth, bsz, six_d), F32),
        compiler_params=_params(2),
        name="ada",
    )(c, ada_w, ada_b.reshape(depth, 1, six_d))


def _weighted_rows(g_ref, gw):
    acc = gw[:, 0:1] * _unpack_bf16_pairs(g_ref[0])
    for k in range(1, TOP_K):
        acc = acc + gw[:, k:k + 1] * _unpack_bf16_pairs(g_ref[k])
    return acc


def _mixer_kernel(*refs, pending):
    if pending:
        g_ref, gwp_ref, gtp_ref, *refs = refs
    (x_ref, sh_ref, sc_ref, gt_ref, sh2_ref, sc2_ref, n1_ref, win_ref, gng_ref, ws_ref, bst_ref,
     wpa_ref, pw_ref, ps_ref, cw_ref, wpc_ref, wout_ref, n2_ref, rwt_ref, rb_ref,
     o_ref, hpk_ref, idx_ref, gw_ref, rank_ref, cnt_ref,
     s_ref, pext_ref, zext_ref, carry_ref) = refs
    tm, d = x_ref.shape[1], x_ref.shape[2]
    j = pl.program_id(1)

    @pl.when(j == 0)
    def _():
        pext_ref[0:POOL_CARRY, :] = jnp.zeros((POOL_CARRY, d), F32)
        zext_ref[0:CONV_CARRY, :] = jnp.zeros((CONV_CARRY, d), F32)

    @pl.when(jnp.logical_and(pl.program_id(0) == 0, j == 0))
    def _():
        carry_ref[...] = jnp.zeros(carry_ref.shape, F32)

    hd = d // GMLP_HEADS
    gd = d // len(POOL_WINDOWS)
    row = lax.broadcasted_iota(I32, (CHUNK, CHUNK), 0)
    col = lax.broadcasted_iota(I32, (CHUNK, CHUNK), 1)
    spatial_w = [jnp.where(row >= col, ws_ref[h], 0.0).astype(BF16) for h in range(GMLP_HEADS)]

    fine = not pending

    def rows_block(r0, n):
        rs = slice(r0, r0 + n)
        x = x_ref[0, rs, :]
        if pending:
            x = x + gtp_ref[0] * _weighted_rows(g_ref.at[:, rs, :], gwp_ref[:, rs].T)
        hb = (_rms(x, n1_ref[...]) * (1.0 + sc_ref[0]) + sh_ref[0]).astype(BF16)

        def proj(c):
            return _dot(hb, win_ref[:, c * d:(c + 1) * d])

        u_raw = proj(0)
        if fine:
            yield
        v_raw = proj(1)
        yield
        u = jax.nn.gelu(u_raw)
        if fine:
            yield
        vb = _rms(jax.nn.gelu(v_raw), gng_ref[...]).astype(BF16)
        yield
        for h in range(GMLP_HEADS):
            bias = bst_ref[:, h:h + 1]
            for ci in range(n // CHUNK):
                blk = vb[ci * CHUNK:(ci + 1) * CHUNK, h * hd:(h + 1) * hd]
                s_ref[r0 + ci * CHUNK:r0 + (ci + 1) * CHUNK, h * hd:(h + 1) * hd] = _dot(spatial_w[h], blk) + bias
        if fine:
            yield
        gate_a = proj(6)
        yield
        ya = _dot((u * s_ref[rs, :]).astype(BF16), wpa_ref[...])
        if fine:
            yield
        mix = jax.nn.sigmoid(gate_a) * ya

        p = proj(2)
        if fine:
            yield
        gate_b = proj(7)
        yield
        p0 = POOL_CARRY + r0
        pext_ref[p0:p0 + n, :] = p
        pos1 = (j * tm + r0 + 1 + lax.broadcasted_iota(I32, (n, 1), 0))
        diffs = []
        for gi, w in enumerate(POOL_WINDOWS):
            cs = slice(gi * gd, (gi + 1) * gd)
            acc = p[:, cs]
            for k in range(1, w):
                acc = acc + pext_ref[p0 - k:p0 - k + n, cs]
            cnt = jnp.minimum(pos1, w).astype(F32)
            diffs.append((acc / cnt - p[:, cs]).astype(BF16))
            if fine:
                yield
        if not fine:
            yield
        yb = jnp.concatenate([_dot(dg, pw_ref[gi]) for gi, dg in enumerate(diffs)], axis=1) * ps_ref[...]
        mix = mix + jax.nn.sigmoid(gate_b) * yb

        conv_c = proj(5)
        if fine:
            yield
        conv_x = proj(3)
        if fine:
            yield
        conv_b = proj(4)
        if fine:
            yield
        gate_c = proj(8)
        yield
        z = conv_c * conv_x
        z0 = CONV_CARRY + r0
        zext_ref[z0:z0 + n, :] = z
        conv = cw_ref[CONV_K - 1:CONV_K, :] * z
        for k in range(CONV_K - 1):
            lag = CONV_K - 1 - k
            conv = conv + cw_ref[k:k + 1, :] * zext_ref[z0 - lag:z0 - lag + n, :]
        gated = (conv_b * conv).astype(BF16)
        yield
        yc = _dot(gated, wpc_ref[...])
        if fine:
            yield
        mix = mix + jax.nn.sigmoid(gate_c) * yc

        x_new = x + gt_ref[0] * _dot(mix.astype(BF16), wout_ref[...])
        o_ref[0, rs, :] = x_new
        _route_tile(x_new, sh2_ref[0], sc2_ref[0], n2_ref[...], rwt_ref[...], rb_ref[...],
                    hpk_ref.at[rs, :], idx_ref.at[:, rs], gw_ref.at[:, rs], rank_ref.at[:, rs], cnt_ref, carry_ref)

    sub = min(MIX_SUB_ROWS, tm)
    waiting = [rows_block(r0, sub) for r0 in range(0, tm, sub)]
    running = []
    while waiting or running:
        if waiting:
            running.append(waiting.pop(0))
        for block in list(running):
            if next(block, "done") == "done":
                running.remove(block)
    pext_ref[0:POOL_CARRY, :] = pext_ref[tm:tm + POOL_CARRY, :]
    zext_ref[0:CONV_CARRY, :] = zext_ref[tm:tm + CONV_CARRY, :]


def _mixer(x, x_b0, b0, bsz, pending, sh, sc, gt, sh2, sc2, n1, w_in, gng, ws, bs, wpa, pool_w, pool_scale, conv_w,
           wpc, w_out, n2, router_w, router_b):
    _, seq, d = x.shape
    n_tok = bsz * seq
    tm = min(MIX_ROWS, seq)
    nj = seq // tm
    vec = pl.BlockSpec((1, 1, d), lambda b, j: (b0 + b, 0, 0))
    tile = pl.BlockSpec((1, tm, d), lambda b, j: (b, j, 0))
    tile_in = pl.BlockSpec((1, tm, d), lambda b, j: (x_b0 + b, j, 0))
    per_tok = pl.BlockSpec((TOP_K, tm), lambda b, j: (0, b * nj + j))
    pending_specs, pending_args = [], []
    if pending is not None:
        g, gw, gt_prev = pending
        pending_specs = [pl.BlockSpec((TOP_K, tm, d // 2), lambda b, j: (0, b * nj + j, 0)),
                         pl.BlockSpec((TOP_K, tm), lambda b, j: (0, b * nj + j)), vec]
        pending_args = [g, gw, gt_prev]
    return pl.pallas_call(
        functools.partial(_mixer_kernel, pending=pending is not None),
        grid=(bsz, nj),
        in_specs=pending_specs + [
            tile_in, vec, vec, vec, vec, vec,
            _resident((1, d)),
            _resident(w_in.shape),
            _resident((1, d)),
            _resident(ws.shape),
            _resident((CHUNK, GMLP_HEADS)),
            _resident(wpa.shape),
            _resident(pool_w.shape),
            _resident((1, d)),
            _resident(conv_w.shape),
            _resident(wpc.shape),
            _resident(w_out.shape),
            _resident((1, d)), _resident((N_EXPERTS, d)), _resident((N_EXPERTS, 1)),
        ],
        out_specs=[
            tile,
            pl.BlockSpec((tm, d // 2), lambda b, j: (b * nj + j, 0)),
            per_tok, per_tok, per_tok,
            pl.BlockSpec((N_EXPERTS, 128), lambda b, j: (0, 0)),
        ],
        out_shape=[
            jax.ShapeDtypeStruct((bsz, seq, d), F32),
            jax.ShapeDtypeStruct((n_tok, d // 2), U32),
            jax.ShapeDtypeStruct((TOP_K, n_tok), I32),
            jax.ShapeDtypeStruct((TOP_K, n_tok), F32),
            jax.ShapeDtypeStruct((TOP_K, n_tok), I32),
            jax.ShapeDtypeStruct((N_EXPERTS, 128), I32),
        ],
        scratch_shapes=[
            pltpu.VMEM((tm, d), F32),
            pltpu.VMEM((POOL_CARRY + tm, d), F32),
            pltpu.VMEM((CONV_CARRY + tm, d), F32),
            pltpu.VMEM((N_EXPERTS, 128), F32),
        ],
        compiler_params=_params(2),
        name="mixer",
    )(*pending_args, x, sh, sc, gt, sh2, sc2, n1.reshape(1, d), w_in.astype(BF16), gng.reshape(1, d), ws, bs.T,
      wpa.astype(BF16), pool_w.astype(BF16), pool_scale.reshape(1, d), conv_w, wpc.astype(BF16),
      w_out.astype(BF16), n2.reshape(1, d), router_w.T.astype(BF16), router_b.reshape(N_EXPERTS, 1))


def _route_tile(x, sh, sc, n2, rwt, rb, hpk_ref, idx_ref, gw_ref, rank_ref, cnt_ref, carry_ref):
    tm = x.shape[0]

    h = _rms(x, n2) * (1.0 + sc) + sh
    hb = h.astype(BF16)
    hpk_ref[...] = _pack_bf16_pairs(h)

    logits = lax.dot_general(rwt, hb, (((1,), (1,)), ((), ())), preferred_element_type=F32) + rb
    iota_e = lax.broadcasted_iota(I32, logits.shape, 0)
    vals, idxs, sels = [], [], []
    rest = logits
    for _ in range(TOP_K):
        m = jnp.max(rest, axis=0, keepdims=True)
        ik = jnp.min(jnp.where(rest == m, iota_e, N_EXPERTS), axis=0, keepdims=True)
        sel = iota_e == ik
        rest = jnp.where(sel, -jnp.inf, rest)
        vals.append(m)
        idxs.append(ik)
        sels.append(sel)
    exps = [jnp.exp(v - vals[0]) for v in vals]
    denom = exps[0] + exps[1] + exps[2] + exps[3]

    chosen = jnp.logical_or(jnp.logical_or(sels[0], sels[1]), jnp.logical_or(sels[2], sels[3]))
    a = jnp.where(chosen, 1.0, 0.0)
    before = lax.broadcasted_iota(I32, (tm, tm), 0) < lax.broadcasted_iota(I32, (tm, tm), 1)
    prior = _dot(a.astype(BF16), jnp.where(before, 1.0, 0.0).astype(BF16)) + carry_ref[:, 0:1]
    for k in range(TOP_K):
        idx_ref[k:k + 1, :] = idxs[k]
        gw_ref[k:k + 1, :] = exps[k] / denom
        rank_ref[k:k + 1, :] = jnp.sum(jnp.where(sels[k], prior, 0.0), axis=0, keepdims=True).astype(I32)
    total = carry_ref[...] + jnp.sum(a, axis=1, keepdims=True)
    carry_ref[...] = total
    cnt_ref[...] = total.astype(I32)


def _dest_kernel(start_ref, idx_ref, rank_ref, o_ref):
    idx = idx_ref[...]
    base = jnp.zeros(idx.shape, I32)
    for e in range(N_EXPERTS):
        base = jnp.where(idx == e, start_ref[e], base)
    o_ref[...] = base + rank_ref[...]


def _dest(pad_start, idx, rank):
    n_tok = idx.shape[1]
    tc = min(DEST_COLS, n_tok)
    blk = pl.BlockSpec((TOP_K, tc), lambda i, s: (0, i))
    return pl.pallas_call(
        _dest_kernel,
        grid_spec=pltpu.PrefetchScalarGridSpec(
            num_scalar_prefetch=1, grid=(n_tok // tc,), in_specs=[blk, blk], out_specs=blk),
        out_shape=jax.ShapeDtypeStruct(idx.shape, I32),
        compiler_params=_params(1),
        name="dest",
    )(pad_start, idx, rank)


def _dispatch(hpk, dest, n_rows):
    n_tok, half = hpk.shape
    win = SC_WINDOW
    assert n_tok % (win * SC_WORKERS) == 0
    per_worker = n_tok // win // SC_WORKERS
    mesh = plsc.VectorSubcoreMesh(core_axis_name="c", subcore_axis_name="s")

    @pl.kernel(out_type=jax.ShapeDtypeStruct((n_rows, half), U32), mesh=mesh,
               scratch_types=[pltpu.VMEM((win, half), U32), pltpu.VMEM((TOP_K, win), I32),
                              pltpu.SemaphoreType.DMA((TOP_K + 1,))])
    def scatter_rows(hpk_hbm, dest_hbm, xs_hbm, xbuf, ibuf, sem):
        worker = lax.axis_index("c") * SC_SUBCORES + lax.axis_index("s")

        @pl.loop(0, per_worker)
        def _(j):
            t0 = pl.multiple_of((worker * per_worker + j) * win, win)
            rows_in = pltpu.async_copy(hpk_hbm.at[pl.ds(t0, win)], xbuf, sem.at[TOP_K])
            pltpu.sync_copy(dest_hbm.at[:, pl.ds(t0, win)], ibuf)
            rows_in.wait()
            scatters = [pltpu.async_copy(xbuf, xs_hbm.at[ibuf.at[k]], sem.at[k]) for k in range(TOP_K)]
            for scatter in scatters:
                scatter.wait()

    return scatter_rows(hpk, dest)


def _expert_kernel(be_ref, nv_ref, live_ref, first_ref, slot_ref, next_ref, xs_ref, wgu_hbm, bgu_ref, wd_hbm,
                   bd_ref, ys_ref, wgu_f32, wd_f32, wgu_bf, wd_bf, sem, *, layer):
    step = pl.program_id(0)
    tb = EXPERT_ROWS

    def weight_copies(expert, slot):
        return (pltpu.make_async_copy(wgu_hbm.at[layer, expert], wgu_f32.at[slot], sem.at[0, slot]),
                pltpu.make_async_copy(wd_hbm.at[layer, expert], wd_f32.at[slot], sem.at[1, slot]))

    @pl.when(step == 0)
    def _():
        for copy in weight_copies(be_ref[0], 0):
            copy.start()

    for sub in range(xs_ref.shape[0] // tb):
        i = step * (xs_ref.shape[0] // tb) + sub
        rows = slice(sub * tb, (sub + 1) * tb)

        @pl.when(i >= nv_ref[0])
        def _():
            ys_ref[rows, :] = jnp.zeros((tb, ys_ref.shape[1]), U32)

        @pl.when(first_ref[i] == 1)
        def _():
            slot = slot_ref[i]

            @pl.when(next_ref[i] >= 0)
            def _():
                for copy in weight_copies(next_ref[i], 1 - slot):
                    copy.start()

            for copy in weight_copies(be_ref[i], slot):
                copy.wait()
            wgu_bf[...] = wgu_f32[slot].astype(BF16)
            wd_bf[...] = wd_f32[slot].astype(BF16)

        @pl.when(i < nv_ref[0])
        def _():
            live = lax.broadcasted_iota(I32, (tb, 1), 0) < live_ref[i]
            xb = jnp.where(live, _unpack_bf16_pairs(xs_ref[rows, :]), 0.0).astype(BF16)
            gu = _dot(xb, wgu_bf[...]) + bgu_ref[layer, be_ref[i]]
            ff = gu.shape[1] // 2
            gate = jnp.minimum(gu[:, :ff], SWIGLU_LIMIT)
            up = jnp.clip(gu[:, ff:], -SWIGLU_LIMIT, SWIGLU_LIMIT)
            glu = gate * jax.nn.sigmoid(SWIGLU_ALPHA * gate)
            y = _dot(((up + 1.0) * glu).astype(BF16), wd_bf[...]) + bd_ref[layer, be_ref[i]]
            ys_ref[rows, :] = _pack_bf16_pairs(y)


def _experts(layer, plan, xs, w_gu, b_gu, w_down, b_down):
    n_rows, half = xs.shape
    depth, _, d, two_f = w_gu.shape
    step_rows = EXPERT_ROWS * EXPERT_BLOCKS_PER_STEP

    def rows(i, be, nv, *_):
        return (jnp.minimum(i, (nv[0] - 1) // EXPERT_BLOCKS_PER_STEP), 0)

    return pl.pallas_call(
        functools.partial(_expert_kernel, layer=layer),
        grid_spec=pltpu.PrefetchScalarGridSpec(
            num_scalar_prefetch=len(plan),
            grid=(n_rows // step_rows,),
            in_specs=[
                pl.BlockSpec((step_rows, half), rows),
                pl.BlockSpec(memory_space=pl.ANY),
                _resident((depth, N_EXPERTS, 1, two_f)),
                pl.BlockSpec(memory_space=pl.ANY),
                _resident((depth, N_EXPERTS, 1, d)),
            ],
            out_specs=pl.BlockSpec((step_rows, half), lambda i, *_: (i, 0)),
            scratch_shapes=[
                pltpu.VMEM((2, d, two_f), F32), pltpu.VMEM((2, two_f // 2, d), F32),
                pltpu.VMEM((d, two_f), BF16), pltpu.VMEM((two_f // 2, d), BF16),
                pltpu.SemaphoreType.DMA((2, 2)),
            ],
        ),
        out_shape=jax.ShapeDtypeStruct((n_rows, half), U32),
        compiler_params=_params(1),
        name="experts",
    )(*plan, xs, w_gu, b_gu.reshape(depth, N_EXPERTS, 1, two_f), w_down, b_down.reshape(depth, N_EXPERTS, 1, d))


def _gather_rows(dest, ys):
    n_tok = dest.shape[1]
    width = ys.shape[1]
    win = SC_WINDOW
    assert n_tok % (win * SC_WORKERS) == 0
    per_worker = n_tok // win // SC_WORKERS
    mesh = plsc.VectorSubcoreMesh(core_axis_name="c", subcore_axis_name="s")

    part = win // SC_GATHER_PARTS
    n_items = TOP_K * SC_GATHER_PARTS
    n_buf = SC_GATHER_BUFFERS

    @pl.kernel(out_type=jax.ShapeDtypeStruct((TOP_K, n_tok, width), ys.dtype), mesh=mesh,
               scratch_types=[pltpu.VMEM((n_buf, part, width), ys.dtype), pltpu.VMEM((TOP_K, win), I32),
                              pltpu.SemaphoreType.DMA((2, n_buf))])
    def gather(ys_hbm, idx_hbm, g_hbm, buf, ibuf, sem):
        worker = lax.axis_index("c") * SC_SUBCORES + lax.axis_index("s")

        @pl.loop(0, per_worker)
        def _(j):
            t0 = pl.multiple_of((worker * per_worker + j) * win, win)
            pltpu.sync_copy(idx_hbm.at[:, pl.ds(t0, win)], ibuf)

            def fetch(n):
                k, h = divmod(n, SC_GATHER_PARTS)
                return pltpu.async_copy(ys_hbm.at[ibuf.at[k, pl.ds(h * part, part)]], buf.at[n % n_buf],
                                        sem.at[0, n % n_buf])

            def store(n):
                k, h = divmod(n, SC_GATHER_PARTS)
                return pltpu.async_copy(buf.at[n % n_buf], g_hbm.at[k, pl.ds(t0 + h * part, part)],
                                        sem.at[1, n % n_buf])

            fetches = {n: fetch(n) for n in range(n_buf - 1)}
            stores = {}
            for n in range(n_items):
                fetches[n].wait()
                stores[n] = store(n)
                nxt = n + n_buf - 1
                if nxt < n_items:
                    if n >= 1:
                        stores.pop(n - 1).wait()
                    fetches[nxt] = fetch(nxt)
            for pending_store in stores.values():
                pending_store.wait()

    return gather(ys, dest)


def _final_kernel(g_ref, x_ref, gw_ref, gt_ref, fg_ref, *rest):
    o_ref = rest[-1]
    y = x_ref[...] + gt_ref[0] * _weighted_rows(g_ref, gw_ref[...].T)
    o_ref[...] = _rms(y, fg_ref[...])


def _final(out, g, x, gw, gt, final_g, b0, batch):
    bsz, seq, d = x.shape
    tm = min(FINAL_ROWS, seq)
    per_seq = seq // tm
    base = b0 * per_seq
    in_specs = [
        pl.BlockSpec((TOP_K, tm, d // 2), lambda i: (0, i, 0)),
        pl.BlockSpec((tm, d), lambda i: (i, 0)),
        pl.BlockSpec((TOP_K, tm), lambda i: (0, i)),
        pl.BlockSpec((1, 1, d), lambda i: (b0 + i // per_seq, 0, 0)),
        pl.BlockSpec((1, d), lambda i: (0, 0)),
    ]
    args = [g, x.reshape(bsz * seq, d), gw, gt, final_g.reshape(1, d)]
    aliases = {}
    if out is not None:
        in_specs.append(pl.BlockSpec(memory_space=pl.ANY))
        args.append(out)
        aliases = {len(args) - 1: 0}
    return pl.pallas_call(
        _final_kernel,
        grid=(bsz * per_seq,),
        in_specs=in_specs,
        out_specs=pl.BlockSpec((tm, d), lambda i: (base + i, 0)),
        out_shape=jax.ShapeDtypeStruct((batch * seq, d), F32),
        input_output_aliases=aliases,
        compiler_params=_params(1),
        name="final",
    )(*args)


def _plan_kernel(cnt_ref, start_ref, be_ref, nv_ref, live_ref, first_ref, slot_ref, next_ref):
    tb = EXPERT_ROWS
    n_blocks = be_ref.shape[0]

    def per_expert(e, carry):
        blk0, group = carry
        count = cnt_ref[e]
        n_blk = (count + (tb - 1)) // tb
        start_ref[e] = blk0 * tb

        def per_block(b, c):
            i = blk0 + b
            be_ref[i] = e
            live_ref[i] = jnp.minimum(count - b * tb, tb)
            first_ref[i] = jnp.where(b == 0, 1, 0)
            slot_ref[i] = group % 2
            return c

        lax.fori_loop(0, n_blk, per_block, 0)
        return blk0 + n_blk, group + jnp.where(n_blk > 0, 1, 0)

    used, _ = lax.fori_loop(0, N_EXPERTS, per_expert, (jnp.int32(0), jnp.int32(0)))
    nv_ref[0] = used

    def backwards(t, nxt):
        i = used - 1 - t
        after = be_ref[jnp.minimum(i + 1, used - 1)]
        nxt = jnp.where(jnp.logical_and(i < used - 1, after != be_ref[i]), after, nxt)
        next_ref[i] = nxt
        return nxt

    lax.fori_loop(0, used, backwards, jnp.int32(-1))
    last = be_ref[used - 1]

    def tail(i, c):
        be_ref[i] = last
        live_ref[i] = 0
        first_ref[i] = 0
        slot_ref[i] = 0
        next_ref[i] = -1
        return c

    lax.fori_loop(used, n_blocks, tail, 0)


def _plan(counts, n_blocks):
    smem = pl.BlockSpec(memory_space=pltpu.SMEM)
    per_block = jax.ShapeDtypeStruct((n_blocks,), I32)
    return pl.pallas_call(
        _plan_kernel,
        in_specs=[smem],
        out_specs=[smem] * 7,
        out_shape=[jax.ShapeDtypeStruct((N_EXPERTS,), I32), per_block, jax.ShapeDtypeStruct((1,), I32),
                   per_block, per_block, per_block, per_block],
        name="plan",
    )(counts)


def _moe(layer, routed, w_gu, b_gu, w_down, b_down):
    n_tok = routed[0].shape[0]
    tb = EXPERT_ROWS
    step_rows = tb * EXPERT_BLOCKS_PER_STEP
    n_rows = -(-(n_tok * TOP_K + N_EXPERTS * (tb - 1)) // step_rows) * step_rows
    hpk, idx, gw, rank, cnt = routed
    pad_start, *plan = _plan(cnt[:, 0], n_rows // tb)
    dest = _dest(pad_start, idx, rank)
    xs = _dispatch(hpk, dest, n_rows)
    ys = _experts(layer, plan, xs, w_gu, b_gu, w_down, b_down)
    return _gather_rows(dest, ys)


def kernel(x, c, norm1_g, ada_w, ada_b, w_in, gmlp_norm_g, gmlp_ws, gmlp_bs, w_proj_a, pool_w, pool_scale,
           conv_w, w_proj_c, w_out, norm2_g, router_w, router_b, exp_w_gu, exp_b_gu, exp_w_down,
           exp_b_down, final_g):
    depth = ada_w.shape[0]
    bsz, seq, d = x.shape
    mods = _ada(c, ada_w, ada_b)
    group = bsz // BATCH_GROUPS
    streams = [x] * BATCH_GROUPS
    pending = [None] * BATCH_GROUPS
    g2 = None
    for l in range(depth):
        g2_prev = g2
        sh1, sc1, g1, sh2, sc2, g2 = [mods[l, :, i * d:(i + 1) * d].reshape(bsz, 1, d) for i in range(6)]
        routed = [None] * BATCH_GROUPS
        for h in range(BATCH_GROUPS):
            held = None if pending[h] is None else (*pending[h], g2_prev)
            streams[h], *routed[h] = _mixer(
                streams[h], h * group if l == 0 else 0, h * group, group, held, sh1, sc1, g1, sh2, sc2,
                norm1_g[l], w_in[l], gmlp_norm_g[l], gmlp_ws[l], gmlp_bs[l], w_proj_a[l], pool_w[l],
                pool_scale[l], conv_w[l], w_proj_c[l], w_out[l], norm2_g[l], router_w[l], router_b[l])
        for h in range(BATCH_GROUPS):
            pending[h] = (_moe(l, routed[h], exp_w_gu, exp_b_gu, exp_w_down, exp_b_down), routed[h][2])
    out = None
    for h in range(BATCH_GROUPS):
        out = _final(out, pending[h][0], streams[h], pending[h][1], g2, final_g, h * group, bsz)
    return out.reshape(bsz, seq, d)
```

```python
import functools

import jax
import jax.numpy as jnp
from jax import lax
from jax.experimental import pallas as pl
from jax.experimental.pallas import tpu as pltpu
from jax.experimental.pallas import tpu_sc as plsc

F32 = jnp.float32
BF16 = jnp.bfloat16
I32 = jnp.int32
U32 = jnp.uint32

RMS_EPS = 1e-5
GMLP_HEADS = 8
CHUNK = 128
POOL_WINDOWS = (2, 4, 8, 16)
POOL_CARRY = 16
CONV_K = 3
CONV_CARRY = 8
N_EXPERTS = 32
TOP_K = 4
SWIGLU_LIMIT = 7.0
SWIGLU_ALPHA = 1.702

LANES = 128

MIX_ROWS = 512
MIX_SUB_ROWS = 256
DEST_COLS = 4096
EXPERT_ROWS = 256
EXPERT_BLOCKS_PER_STEP = 4
FINAL_ROWS = 512
BATCH_GROUPS = 2
SC_SUBCORES = 16
SC_WORKERS = 2 * SC_SUBCORES
SC_WINDOW = 128
SC_GATHER_PARTS = 2
SC_GATHER_BUFFERS = 3

VMEM_LIMIT = 58 * 1024 * 1024


def _rms(x, g):
    return x * lax.rsqrt(jnp.mean(x * x, axis=-1, keepdims=True) + RMS_EPS) * g


def _dot(a, b):
    return jnp.dot(a, b, preferred_element_type=F32)


def _params(n_axes, vmem=VMEM_LIMIT):
    return pltpu.CompilerParams(dimension_semantics=("arbitrary",) * n_axes, vmem_limit_bytes=vmem)


def _pack_bf16_pairs(x):
    half = x.shape[1] // 2
    bits = lax.bitcast_convert_type(x.astype(BF16).astype(F32), U32)
    return (bits[:, :half] >> 16) | bits[:, half:]


def _unpack_bf16_pairs(w):
    lo = lax.bitcast_convert_type(w << 16, F32)
    hi = lax.bitcast_convert_type(w & jnp.uint32(0xFFFF0000), F32)
    return jnp.concatenate([lo, hi], axis=1)


def _resident(shape):
    zeros = (0,) * len(shape)
    return pl.BlockSpec(shape, lambda *_: zeros, pipeline_mode=pl.Buffered(1))


def _ada_kernel(c_ref, w_ref, b_ref, o_ref):
    c = c_ref[...]
    cond = c * jax.nn.sigmoid(c)
    o_ref[0] = _dot(cond.astype(BF16), w_ref[0].astype(BF16)) + b_ref[0]


def _ada(c, ada_w, ada_b):
    depth, d, six_d = ada_w.shape
    bsz = c.shape[0]
    return pl.pallas_call(
        _ada_kernel,
        grid=(depth, six_d // d),
        in_specs=[
            pl.BlockSpec((bsz, d), lambda l, j: (0, 0)),
            pl.BlockSpec((1, d, d), lambda l, j: (l, 0, j)),
            pl.BlockSpec((1, 1, d), lambda l, j: (l, 0, j)),
        ],
        out_specs=pl.BlockSpec((1, bsz, d), lambda l, j: (l, 0, j)),
        out_shape=jax.ShapeDtypeStruct((depth, bsz, six_d), F32),
        compiler_params=_params(2),
        name="ada",
    )(c, ada_w, ada_b.reshape(depth, 1, six_d))


def _weighted_rows(g_ref, gw):
    acc = gw[:, 0:1] * _unpack_bf16_pairs(g_ref[0])
    for k in range(1, TOP_K):
        acc = acc + gw[:, k:k + 1] * _unpack_bf16_pairs(g_ref[k])
    return acc


def _mixer_kernel(*refs, pending):
    if pending:
        g_ref, gwp_ref, gtp_ref, *refs = refs
    (x_ref, sh_ref, sc_ref, gt_ref, sh2_ref, sc2_ref, n1_ref, win_ref, gng_ref, ws_ref, bst_ref,
     wpa_ref, pw_ref, ps_ref, cw_ref, wpc_ref, wout_ref, n2_ref, rwt_ref, rb_ref,
     o_ref, hpk_ref, idx_ref, gw_ref, rank_ref, cnt_ref,
     s_ref, pext_ref, zext_ref, carry_ref) = refs
    tm, d = x_ref.shape[1], x_ref.shape[2]
    j = pl.program_id(1)

    @pl.when(j == 0)
    def _():
        pext_ref[0:POOL_CARRY, :] = jnp.zeros((POOL_CARRY, d), F32)
        zext_ref[0:CONV_CARRY, :] = jnp.zeros((CONV_CARRY, d), F32)

    @pl.when(jnp.logical_and(pl.program_id(0) == 0, j == 0))
    def _():
        carry_ref[...] = jnp.zeros(carry_ref.shape, F32)

    hd = d // GMLP_HEADS
    gd = d // len(POOL_WINDOWS)
    row = lax.broadcasted_iota(I32, (CHUNK, CHUNK), 0)
    col = lax.broadcasted_iota(I32, (CHUNK, CHUNK), 1)
    spatial_w = [jnp.where(row >= col, ws_ref[h], 0.0).astype(BF16) for h in range(GMLP_HEADS)]

    fine = not pending

    def rows_block(r0, n):
        rs = slice(r0, r0 + n)
        x = x_ref[0, rs, :]
        if pending:
            x = x + gtp_ref[0] * _weighted_rows(g_ref.at[:, rs, :], gwp_ref[:, rs].T)
        hb = (_rms(x, n1_ref[...]) * (1.0 + sc_ref[0]) + sh_ref[0]).astype(BF16)

        def proj(c):
            return _dot(hb, win_ref[:, c * d:(c + 1) * d])

        u_raw = proj(0)
        if fine:
            yield
        v_raw = proj(1)
        yield
        u = jax.nn.gelu(u_raw)
        if fine:
            yield
        vb = _rms(jax.nn.gelu(v_raw), gng_ref[...]).astype(BF16)
        yield
        for h in range(GMLP_HEADS):
            bias = bst_ref[:, h:h + 1]
            for ci in range(n // CHUNK):
                blk = vb[ci * CHUNK:(ci + 1) * CHUNK, h * hd:(h + 1) * hd]
                s_ref[r0 + ci * CHUNK:r0 + (ci + 1) * CHUNK, h * hd:(h + 1) * hd] = _dot(spatial_w[h], blk) + bias
        if fine:
            yield
        gate_a = proj(6)
        yield
        ya = _dot((u * s_ref[rs, :]).astype(BF16), wpa_ref[...])
        if fine:
            yield
        mix = jax.nn.sigmoid(gate_a) * ya

        p = proj(2)
        if fine:
            yield
        gate_b = proj(7)
        yield
        p0 = POOL_CARRY + r0
        pext_ref[p0:p0 + n, :] = p
        pos1 = (j * tm + r0 + 1 + lax.broadcasted_iota(I32, (n, 1), 0))
        diffs = []
        for gi, w in enumerate(POOL_WINDOWS):
            cs = slice(gi * gd, (gi + 1) * gd)
            acc = p[:, cs]
            for k in range(1, w):
                acc = acc + pext_ref[p0 - k:p0 - k + n, cs]
            cnt = jnp.minimum(pos1, w).astype(F32)
            diffs.append((acc / cnt - p[:, cs]).astype(BF16))
            if fine:
                yield
        if not fine:
            yield
        yb = jnp.concatenate([_dot(dg, pw_ref[gi]) for gi, dg in enumerate(diffs)], axis=1) * ps_ref[...]
        mix = mix + jax.nn.sigmoid(gate_b) * yb

        conv_c = proj(5)
        if fine:
            yield
        conv_x = proj(3)
        if fine:
            yield
        conv_b = proj(4)
        if fine:
            yield
        gate_c = proj(8)
        yield
        z = conv_c * conv_x
        z0 = CONV_CARRY + r0
        zext_ref[z0:z0 + n, :] = z
        conv = cw_ref[CONV_K - 1:CONV_K, :] * z
        for k in range(CONV_K - 1):
            lag = CONV_K - 1 - k
            conv = conv + cw_ref[k:k + 1, :] * zext_ref[z0 - lag:z0 - lag + n, :]
        gated = (conv_b * conv).astype(BF16)
        yield
        yc = _dot(gated, wpc_ref[...])
        if fine:
            yield
        mix = mix + jax.nn.sigmoid(gate_c) * yc

        x_new = x + gt_ref[0] * _dot(mix.astype(BF16), wout_ref[...])
        o_ref[0, rs, :] = x_new
        _route_tile(x_new, sh2_ref[0], sc2_ref[0], n2_ref[...], rwt_ref[...], rb_ref[...],
                    hpk_ref.at[rs, :], idx_ref.at[:, rs], gw_ref.at[:, rs], rank_ref.at[:, rs], cnt_ref, carry_ref)

    sub = min(MIX_SUB_ROWS, tm)
    waiting = [rows_block(r0, sub) for r0 in range(0, tm, sub)]
    running = []
    while waiting or running:
        if waiting:
            running.append(waiting.pop(0))
        for block in list(running):
            if next(block, "done") == "done":
                running.remove(block)
    pext_ref[0:POOL_CARRY, :] = pext_ref[tm:tm + POOL_CARRY, :]
    zext_ref[0:CONV_CARRY, :] = zext_ref[tm:tm + CONV_CARRY, :]


def _mixer(x, x_b0, b0, bsz, pending, sh, sc, gt, sh2, sc2, n1, w_in, gng, ws, bs, wpa, pool_w, pool_scale, conv_w,
           wpc, w_out, n2, router_w, router_b):
    _, seq, d = x.shape
    n_tok = bsz * seq
    tm = min(MIX_ROWS, seq)
    nj = seq // tm
    vec = pl.BlockSpec((1, 1, d), lambda b, j: (b0 + b, 0, 0))
    tile = pl.BlockSpec((1, tm, d), lambda b, j: (b, j, 0))
    tile_in = pl.BlockSpec((1, tm, d), lambda b, j: (x_b0 + b, j, 0))
    per_tok = pl.BlockSpec((TOP_K, tm), lambda b, j: (0, b * nj + j))
    pending_specs, pending_args = [], []
    if pending is not None:
        g, gw, gt_prev = pending
        pending_specs = [pl.BlockSpec((TOP_K, tm, d // 2), lambda b, j: (0, b * nj + j, 0)),
                         pl.BlockSpec((TOP_K, tm), lambda b, j: (0, b * nj + j)), vec]
        pending_args = [g, gw, gt_prev]
    return pl.pallas_call(
        functools.partial(_mixer_kernel, pending=pending is not None),
        grid=(bsz, nj),
        in_specs=pending_specs + [
            tile_in, vec, vec, vec, vec, vec,
            _resident((1, d)),
            _resident(w_in.shape),
            _resident((1, d)),
            _resident(ws.shape),
            _resident((CHUNK, GMLP_HEADS)),
            _resident(wpa.shape),
            _resident(pool_w.shape),
            _resident((1, d)),
            _resident(conv_w.shape),
            _resident(wpc.shape),
            _resident(w_out.shape),
            _resident((1, d)), _resident((N_EXPERTS, d)), _resident((N_EXPERTS, 1)),
        ],
        out_specs=[
            tile,
            pl.BlockSpec((tm, d // 2), lambda b, j: (b * nj + j, 0)),
            per_tok, per_tok, per_tok,
            pl.BlockSpec((N_EXPERTS, LANES), lambda b, j: (0, 0)),
        ],
        out_shape=[
            jax.ShapeDtypeStruct((bsz, seq, d), F32),
            jax.ShapeDtypeStruct((n_tok, d // 2), U32),
            jax.ShapeDtypeStruct((TOP_K, n_tok), I32),
            jax.ShapeDtypeStruct((TOP_K, n_tok), F32),
            jax.ShapeDtypeStruct((TOP_K, n_tok), I32),
            jax.ShapeDtypeStruct((N_EXPERTS, LANES), I32),
        ],
        scratch_shapes=[
            pltpu.VMEM((tm, d), F32),
            pltpu.VMEM((POOL_CARRY + tm, d), F32),
            pltpu.VMEM((CONV_CARRY + tm, d), F32),
            pltpu.VMEM((N_EXPERTS, LANES), F32),
        ],
        compiler_params=_params(2),
        name="mixer",
    )(*pending_args, x, sh, sc, gt, sh2, sc2, n1.reshape(1, d), w_in.astype(BF16), gng.reshape(1, d), ws, bs.T,
      wpa.astype(BF16), pool_w.astype(BF16), pool_scale.reshape(1, d), conv_w, wpc.astype(BF16),
      w_out.astype(BF16), n2.reshape(1, d), router_w.T.astype(BF16), router_b.reshape(N_EXPERTS, 1))


def _route_tile(x, sh, sc, n2, rwt, rb, hpk_ref, idx_ref, gw_ref, rank_ref, cnt_ref, carry_ref):
    tm = x.shape[0]

    h = _rms(x, n2) * (1.0 + sc) + sh
    hb = h.astype(BF16)
    hpk_ref[...] = _pack_bf16_pairs(h)

    logits = lax.dot_general(rwt, hb, (((1,), (1,)), ((), ())), preferred_element_type=F32) + rb
    iota_e = lax.broadcasted_iota(I32, logits.shape, 0)
    vals, idxs, sels = [], [], []
    rest = logits
    for _ in range(TOP_K):
        m = jnp.max(rest, axis=0, keepdims=True)
        ik = jnp.min(jnp.where(rest == m, iota_e, N_EXPERTS), axis=0, keepdims=True)
        sel = iota_e == ik
        rest = jnp.where(sel, -jnp.inf, rest)
        vals.append(m)
        idxs.append(ik)
        sels.append(sel)
    exps = [jnp.exp(v - vals[0]) for v in vals]
    denom = exps[0] + exps[1] + exps[2] + exps[3]

    chosen = jnp.logical_or(jnp.logical_or(sels[0], sels[1]), jnp.logical_or(sels[2], sels[3]))
    a = jnp.where(chosen, 1.0, 0.0)
    before = lax.broadcasted_iota(I32, (tm, tm), 0) < lax.broadcasted_iota(I32, (tm, tm), 1)
    prior = _dot(a.astype(BF16), jnp.where(before, 1.0, 0.0).astype(BF16)) + carry_ref[:, 0:1]
    for k in range(TOP_K):
        idx_ref[k:k + 1, :] = idxs[k]
        gw_ref[k:k + 1, :] = exps[k] / denom
        rank_ref[k:k + 1, :] = jnp.sum(jnp.where(sels[k], prior, 0.0), axis=0, keepdims=True).astype(I32)
    total = carry_ref[...] + jnp.sum(a, axis=1, keepdims=True)
    carry_ref[...] = total
    cnt_ref[...] = total.astype(I32)


def _dest_kernel(start_ref, idx_ref, rank_ref, o_ref):
    idx = idx_ref[...]
    base = jnp.zeros(idx.shape, I32)
    for e in range(N_EXPERTS):
        base = jnp.where(idx == e, start_ref[e], base)
    o_ref[...] = base + rank_ref[...]


def _dest(pad_start, idx, rank):
    n_tok = idx.shape[1]
    tc = min(DEST_COLS, n_tok)
    blk = pl.BlockSpec((TOP_K, tc), lambda i, s: (0, i))
    return pl.pallas_call(
        _dest_kernel,
        grid_spec=pltpu.PrefetchScalarGridSpec(
            num_scalar_prefetch=1, grid=(n_tok // tc,), in_specs=[blk, blk], out_specs=blk),
        out_shape=jax.ShapeDtypeStruct(idx.shape, I32),
        compiler_params=_params(1),
        name="dest",
    )(pad_start, idx, rank)


def _dispatch(hpk, dest, n_rows):
    n_tok, half = hpk.shape
    win = SC_WINDOW
    assert n_tok % (win * SC_WORKERS) == 0
    per_worker = n_tok // win // SC_WORKERS
    mesh = plsc.VectorSubcoreMesh(core_axis_name="c", subcore_axis_name="s")

    @pl.kernel(out_type=jax.ShapeDtypeStruct((n_rows, half), U32), mesh=mesh,
               scratch_types=[pltpu.VMEM((win, half), U32), pltpu.VMEM((TOP_K, win), I32),
                              pltpu.SemaphoreType.DMA((TOP_K + 1,))])
    def scatter_rows(hpk_hbm, dest_hbm, xs_hbm, xbuf, ibuf, sem):
        worker = lax.axis_index("c") * SC_SUBCORES + lax.axis_index("s")

        @pl.loop(0, per_worker)
        def _(j):
            t0 = pl.multiple_of((worker * per_worker + j) * win, win)
            rows_in = pltpu.async_copy(hpk_hbm.at[pl.ds(t0, win)], xbuf, sem.at[TOP_K])
            pltpu.sync_copy(dest_hbm.at[:, pl.ds(t0, win)], ibuf)
            rows_in.wait()
            scatters = [pltpu.async_copy(xbuf, xs_hbm.at[ibuf.at[k]], sem.at[k]) for k in range(TOP_K)]
            for scatter in scatters:
                scatter.wait()

    return scatter_rows(hpk, dest)


def _expert_kernel(be_ref, nv_ref, live_ref, first_ref, slot_ref, next_ref, xs_ref, wgu_hbm, bgu_ref, wd_hbm,
                   bd_ref, ys_ref, wgu_f32, wd_f32, wgu_bf, wd_bf, sem, *, layer):
    step = pl.program_id(0)
    tb = EXPERT_ROWS

    def weight_copies(expert, slot):
        return (pltpu.make_async_copy(wgu_hbm.at[layer, expert], wgu_f32.at[slot], sem.at[0, slot]),
                pltpu.make_async_copy(wd_hbm.at[layer, expert], wd_f32.at[slot], sem.at[1, slot]))

    @pl.when(step == 0)
    def _():
        for copy in weight_copies(be_ref[0], 0):
            copy.start()

    for sub in range(xs_ref.shape[0] // tb):
        i = step * (xs_ref.shape[0] // tb) + sub
        rows = slice(sub * tb, (sub + 1) * tb)

        @pl.when(i >= nv_ref[0])
        def _():
            ys_ref[rows, :] = jnp.zeros((tb, ys_ref.shape[1]), U32)

        @pl.when(first_ref[i] == 1)
        def _():
            slot = slot_ref[i]

            @pl.when(next_ref[i] >= 0)
            def _():
                for copy in weight_copies(next_ref[i], 1 - slot):
                    copy.start()

            for copy in weight_copies(be_ref[i], slot):
                copy.wait()
            wgu_bf[...] = wgu_f32[slot].astype(BF16)
            wd_bf[...] = wd_f32[slot].astype(BF16)

        @pl.when(i < nv_ref[0])
        def _():
            live = lax.broadcasted_iota(I32, (tb, 1), 0) < live_ref[i]
            xb = jnp.where(live, _unpack_bf16_pairs(xs_ref[rows, :]), 0.0).astype(BF16)
            gu = _dot(xb, wgu_bf[...]) + bgu_ref[layer, be_ref[i]]
            ff = gu.shape[1] // 2
            gate = jnp.minimum(gu[:, :ff], SWIGLU_LIMIT)
            up = jnp.clip(gu[:, ff:], -SWIGLU_LIMIT, SWIGLU_LIMIT)
            glu = gate * jax.nn.sigmoid(SWIGLU_ALPHA * gate)
            y = _dot(((up + 1.0) * glu).astype(BF16), wd_bf[...]) + bd_ref[layer, be_ref[i]]
            ys_ref[rows, :] = _pack_bf16_pairs(y)


def _experts(layer, plan, xs, w_gu, b_gu, w_down, b_down):
    n_rows, half = xs.shape
    depth, _, d, two_f = w_gu.shape
    step_rows = EXPERT_ROWS * EXPERT_BLOCKS_PER_STEP

    def rows(i, be, nv, *_):
        return (jnp.minimum(i, (nv[0] - 1) // EXPERT_BLOCKS_PER_STEP), 0)

    return pl.pallas_call(
        functools.partial(_expert_kernel, layer=layer),
        grid_spec=pltpu.PrefetchScalarGridSpec(
            num_scalar_prefetch=len(plan),
            grid=(n_rows // step_rows,),
            in_specs=[
                pl.BlockSpec((step_rows, half), rows),
                pl.BlockSpec(memory_space=pl.ANY),
                _resident((depth, N_EXPERTS, 1, two_f)),
                pl.BlockSpec(memory_space=pl.ANY),
                _resident((depth, N_EXPERTS, 1, d)),
            ],
            out_specs=pl.BlockSpec((step_rows, half), lambda i, *_: (i, 0)),
            scratch_shapes=[
                pltpu.VMEM((2, d, two_f), F32), pltpu.VMEM((2, two_f // 2, d), F32),
                pltpu.VMEM((d, two_f), BF16), pltpu.VMEM((two_f // 2, d), BF16),
                pltpu.SemaphoreType.DMA((2, 2)),
            ],
        ),
        out_shape=jax.ShapeDtypeStruct((n_rows, half), U32),
        compiler_params=_params(1),
        name="experts",
    )(*plan, xs, w_gu, b_gu.reshape(depth, N_EXPERTS, 1, two_f), w_down, b_down.reshape(depth, N_EXPERTS, 1, d))


def _gather_rows(dest, ys):
    n_tok = dest.shape[1]
    width = ys.shape[1]
    win = SC_WINDOW
    assert n_tok % (win * SC_WORKERS) == 0
    per_worker = n_tok // win // SC_WORKERS
    mesh = plsc.VectorSubcoreMesh(core_axis_name="c", subcore_axis_name="s")

    part = win // SC_GATHER_PARTS
    n_items = TOP_K * SC_GATHER_PARTS
    n_buf = SC_GATHER_BUFFERS

    @pl.kernel(out_type=jax.ShapeDtypeStruct((TOP_K, n_tok, width), ys.dtype), mesh=mesh,
               scratch_types=[pltpu.VMEM((n_buf, part, width), ys.dtype), pltpu.VMEM((TOP_K, win), I32),
                              pltpu.SemaphoreType.DMA((2, n_buf))])
    def gather(ys_hbm, idx_hbm, g_hbm, buf, ibuf, sem):
        worker = lax.axis_index("c") * SC_SUBCORES + lax.axis_index("s")

        @pl.loop(0, per_worker)
        def _(j):
            t0 = pl.multiple_of((worker * per_worker + j) * win, win)
            pltpu.sync_copy(idx_hbm.at[:, pl.ds(t0, win)], ibuf)

            def fetch(n):
                k, h = divmod(n, SC_GATHER_PARTS)
                return pltpu.async_copy(ys_hbm.at[ibuf.at[k, pl.ds(h * part, part)]], buf.at[n % n_buf],
                                        sem.at[0, n % n_buf])

            def store(n):
                k, h = divmod(n, SC_GATHER_PARTS)
                return pltpu.async_copy(buf.at[n % n_buf], g_hbm.at[k, pl.ds(t0 + h * part, part)],
                                        sem.at[1, n % n_buf])

            fetches = {n: fetch(n) for n in range(n_buf - 1)}
            stores = {}
            for n in range(n_items):
                fetches[n].wait()
                stores[n] = store(n)
                nxt = n + n_buf - 1
                if nxt < n_items:
                    if n >= 1:
                        stores.pop(n - 1).wait()
                    fetches[nxt] = fetch(nxt)
            for pending_store in stores.values():
                pending_store.wait()

    return gather(ys, dest)


def _final_kernel(g_ref, x_ref, gw_ref, gt_ref, fg_ref, *rest):
    o_ref = rest[-1]
    y = x_ref[...] + gt_ref[0] * _weighted_rows(g_ref, gw_ref[...].T)
    o_ref[...] = _rms(y, fg_ref[...])


def _final(out, g, x, gw, gt, final_g, b0, batch):
    bsz, seq, d = x.shape
    tm = min(FINAL_ROWS, seq)
    per_seq = seq // tm
    base = b0 * per_seq
    in_specs = [
        pl.BlockSpec((TOP_K, tm, d // 2), lambda i: (0, i, 0)),
        pl.BlockSpec((tm, d), lambda i: (i, 0)),
        pl.BlockSpec((TOP_K, tm), lambda i: (0, i)),
        pl.BlockSpec((1, 1, d), lambda i: (b0 + i // per_seq, 0, 0)),
        pl.BlockSpec((1, d), lambda i: (0, 0)),
    ]
    args = [g, x.reshape(bsz * seq, d), gw, gt, final_g.reshape(1, d)]
    aliases = {}
    if out is not None:
        in_specs.append(pl.BlockSpec(memory_space=pl.ANY))
        args.append(out)
        aliases = {len(args) - 1: 0}
    return pl.pallas_call(
        _final_kernel,
        grid=(bsz * per_seq,),
        in_specs=in_specs,
        out_specs=pl.BlockSpec((tm, d), lambda i: (base + i, 0)),
        out_shape=jax.ShapeDtypeStruct((batch * seq, d), F32),
        input_output_aliases=aliases,
        compiler_params=_params(1),
        name="final",
    )(*args)


def _plan_kernel(cnt_ref, start_ref, be_ref, nv_ref, live_ref, first_ref, slot_ref, next_ref):
    tb = EXPERT_ROWS
    n_blocks = be_ref.shape[0]

    def per_expert(e, carry):
        blk0, group = carry
        count = cnt_ref[e]
        n_blk = (count + (tb - 1)) // tb
        start_ref[e] = blk0 * tb

        def per_block(b, c):
            i = blk0 + b
            be_ref[i] = e
            live_ref[i] = jnp.minimum(count - b * tb, tb)
            first_ref[i] = jnp.where(b == 0, 1, 0)
            slot_ref[i] = group % 2
            return c

        lax.fori_loop(0, n_blk, per_block, 0)
        return blk0 + n_blk, group + jnp.where(n_blk > 0, 1, 0)

    used, _ = lax.fori_loop(0, N_EXPERTS, per_expert, (jnp.int32(0), jnp.int32(0)))
    nv_ref[0] = used

    def backwards(t, nxt):
        i = used - 1 - t
        after = be_ref[jnp.minimum(i + 1, used - 1)]
        nxt = jnp.where(jnp.logical_and(i < used - 1, after != be_ref[i]), after, nxt)
        next_ref[i] = nxt
        return nxt

    lax.fori_loop(0, used, backwards, jnp.int32(-1))
    last = be_ref[used - 1]

    def tail(i, c):
        be_ref[i] = last
        live_ref[i] = 0
        first_ref[i] = 0
        slot_ref[i] = 0
        next_ref[i] = -1
        return c

    lax.fori_loop(used, n_blocks, tail, 0)


def _plan(counts, n_blocks):
    smem = pl.BlockSpec(memory_space=pltpu.SMEM)
    per_block = jax.ShapeDtypeStruct((n_blocks,), I32)
    return pl.pallas_call(
        _plan_kernel,
        in_specs=[smem],
        out_specs=[smem] * 7,
        out_shape=[jax.ShapeDtypeStruct((N_EXPERTS,), I32), per_block, jax.ShapeDtypeStruct((1,), I32),
                   per_block, per_block, per_block, per_block],
        name="plan",
    )(counts)


def _moe(layer, routed, w_gu, b_gu, w_down, b_down):
    n_tok = routed[0].shape[0]
    tb = EXPERT_ROWS
    step_rows = tb * EXPERT_BLOCKS_PER_STEP
    n_rows = -(-(n_tok * TOP_K + N_EXPERTS * (tb - 1)) // step_rows) * step_rows
    hpk, idx, gw, rank, cnt = routed
    pad_start, *plan = _plan(cnt[:, 0], n_rows // tb)
    dest = _dest(pad_start, idx, rank)
    xs = _dispatch(hpk, dest, n_rows)
    ys = _experts(layer, plan, xs, w_gu, b_gu, w_down, b_down)
    return _gather_rows(dest, ys)


def kernel(x, c, norm1_g, ada_w, ada_b, w_in, gmlp_norm_g, gmlp_ws, gmlp_bs, w_proj_a, pool_w, pool_scale,
           conv_w, w_proj_c, w_out, norm2_g, router_w, router_b, exp_w_gu, exp_b_gu, exp_w_down,
           exp_b_down, final_g):
    depth = ada_w.shape[0]
    bsz, seq, d = x.shape
    mods = _ada(c, ada_w, ada_b)
    group = bsz // BATCH_GROUPS
    streams = [x] * BATCH_GROUPS
    pending = [None] * BATCH_GROUPS
    g2 = None
    for l in range(depth):
        g2_prev = g2
        sh1, sc1, g1, sh2, sc2, g2 = [mods[l, :, i * d:(i + 1) * d].reshape(bsz, 1, d) for i in range(6)]
        routed = [None] * BATCH_GROUPS
        for h in range(BATCH_GROUPS):
            held = None if pending[h] is None else (*pending[h], g2_prev)
            streams[h], *routed[h] = _mixer(
                streams[h], h * group if l == 0 else 0, h * group, group, held, sh1, sc1, g1, sh2, sc2,
                norm1_g[l], w_in[l], gmlp_norm_g[l], gmlp_ws[l], gmlp_bs[l], w_proj_a[l], pool_w[l],
                pool_scale[l], conv_w[l], w_proj_c[l], w_out[l], norm2_g[l], router_w[l], router_b[l])
        for h in range(BATCH_GROUPS):
            pending[h] = (_moe(l, routed[h], exp_w_gu, exp_b_gu, exp_w_down, exp_b_down), routed[h][2])
    out = None
    for h in range(BATCH_GROUPS):
        out = _final(out, pending[h][0], streams[h], pending[h][1], g2, final_g, h * group, bsz)
    return out.reshape(bsz, seq, d)
```

```python
import functools

import jax
import jax.numpy as jnp
from jax import lax
from jax.experimental import pallas as pl
from jax.experimental.pallas import tpu as pltpu
from jax.experimental.pallas import tpu_sc as plsc

F32 = jnp.float32
BF16 = jnp.bfloat16
I32 = jnp.int32
U32 = jnp.uint32

RMS_EPS = 1e-5
GMLP_HEADS = 8
CHUNK = 128
POOL_WINDOWS = (2, 4, 8, 16)
POOL_CARRY = 16
CONV_K = 3
CONV_CARRY = 8
N_EXPERTS = 32
TOP_K = 4
SWIGLU_LIMIT = 7.0
SWIGLU_ALPHA = 1.702

LANES = 128

MIX_ROWS = 512
MIX_SUB_ROWS = 256
DEST_COLS = 4096
EXPERT_ROWS = 256
EXPERT_BLOCKS_PER_STEP = 4
FINAL_ROWS = 512
BATCH_GROUPS = 2
SC_SUBCORES = 16
SC_WORKERS = 2 * SC_SUBCORES
SC_WINDOW = 128
SC_GATHER_PARTS = 2
SC_GATHER_BUFFERS = 3
SC_LANES = 16
SC_COMBINE_PART = 32

VMEM_LIMIT = 58 * 1024 * 1024


def _rms(x, g):
    return x * lax.rsqrt(jnp.mean(x * x, axis=-1, keepdims=True) + RMS_EPS) * g


def _dot(a, b):
    return jnp.dot(a, b, preferred_element_type=F32)


def _params(n_axes, vmem=VMEM_LIMIT):
    return pltpu.CompilerParams(dimension_semantics=("arbitrary",) * n_axes, vmem_limit_bytes=vmem)


def _pack_bf16_pairs(x):
    half = x.shape[1] // 2
    bits = lax.bitcast_convert_type(x.astype(BF16).astype(F32), U32)
    return (bits[:, :half] >> 16) | bits[:, half:]


def _unpack_bf16_pairs(w):
    lo = lax.bitcast_convert_type(w << 16, F32)
    hi = lax.bitcast_convert_type(w & jnp.uint32(0xFFFF0000), F32)
    return jnp.concatenate([lo, hi], axis=1)


def _resident(shape):
    zeros = (0,) * len(shape)
    return pl.BlockSpec(shape, lambda *_: zeros, pipeline_mode=pl.Buffered(1))


def _ada_kernel(c_ref, w_ref, b_ref, o_ref):
    c = c_ref[...]
    cond = c * jax.nn.sigmoid(c)
    o_ref[0] = _dot(cond.astype(BF16), w_ref[0].astype(BF16)) + b_ref[0]


def _ada(c, ada_w, ada_b):
    depth, d, six_d = ada_w.shape
    bsz = c.shape[0]
    return pl.pallas_call(
        _ada_kernel,
        grid=(depth, six_d // d),
        in_specs=[
            pl.BlockSpec((bsz, d), lambda l, j: (0, 0)),
            pl.BlockSpec((1, d, d), lambda l, j: (l, 0, j)),
            pl.BlockSpec((1, 1, d), lambda l, j: (l, 0, j)),
        ],
        out_specs=pl.BlockSpec((1, bsz, d), lambda l, j: (l, 0, j)),
        out_shape=jax.ShapeDtypeStruct((depth, bsz, six_d), F32),
        compiler_params=_params(2),
        name="ada",
    )(c, ada_w, ada_b.reshape(depth, 1, six_d))


def _weighted_rows(g_ref, gw):
    acc = gw[:, 0:1] * _unpack_bf16_pairs(g_ref[0])
    for k in range(1, TOP_K):
        acc = acc + gw[:, k:k + 1] * _unpack_bf16_pairs(g_ref[k])
    return acc


def _mixer_kernel(*refs, pending):
    if pending:
        moe_ref, gtp_ref, *refs = refs
    (x_ref, sh_ref, sc_ref, gt_ref, sh2_ref, sc2_ref, n1_ref, win_ref, gng_ref, ws_ref, bst_ref,
     wpa_ref, pw_ref, ps_ref, cw_ref, wpc_ref, wout_ref, n2_ref, rwt_ref, rb_ref,
     o_ref, hpk_ref, idx_ref, gw_ref, rank_ref, cnt_ref,
     s_ref, pext_ref, zext_ref, carry_ref) = refs
    tm, d = x_ref.shape[1], x_ref.shape[2]
    j = pl.program_id(1)

    @pl.when(j == 0)
    def _():
        pext_ref[0:POOL_CARRY, :] = jnp.zeros((POOL_CARRY, d), F32)
        zext_ref[0:CONV_CARRY, :] = jnp.zeros((CONV_CARRY, d), F32)

    @pl.when(jnp.logical_and(pl.program_id(0) == 0, j == 0))
    def _():
        carry_ref[...] = jnp.zeros(carry_ref.shape, F32)

    hd = d // GMLP_HEADS
    gd = d // len(POOL_WINDOWS)
    row = lax.broadcasted_iota(I32, (CHUNK, CHUNK), 0)
    col = lax.broadcasted_iota(I32, (CHUNK, CHUNK), 1)
    spatial_w = [jnp.where(row >= col, ws_ref[h], 0.0).astype(BF16) for h in range(GMLP_HEADS)]

    fine = True

    def rows_block(r0, n):
        rs = slice(r0, r0 + n)
        x = x_ref[0, rs, :]
        if pending:
            x = x + gtp_ref[0] * moe_ref[rs, :]
        hb = (_rms(x, n1_ref[...]) * (1.0 + sc_ref[0]) + sh_ref[0]).astype(BF16)

        def proj(c):
            return _dot(hb, win_ref[:, c * d:(c + 1) * d])

        u_raw = proj(0)
        if fine:
            yield
        v_raw = proj(1)
        yield
        u = jax.nn.gelu(u_raw)
        if fine:
            yield
        vb = _rms(jax.nn.gelu(v_raw), gng_ref[...]).astype(BF16)
        yield
        for h in range(GMLP_HEADS):
            bias = bst_ref[:, h:h + 1]
            for ci in range(n // CHUNK):
                blk = vb[ci * CHUNK:(ci + 1) * CHUNK, h * hd:(h + 1) * hd]
                s_ref[r0 + ci * CHUNK:r0 + (ci + 1) * CHUNK, h * hd:(h + 1) * hd] = _dot(spatial_w[h], blk) + bias
        if fine:
            yield
        gate_a = proj(6)
        yield
        ya = _dot((u * s_ref[rs, :]).astype(BF16), wpa_ref[...])
        if fine:
            yield
        mix = jax.nn.sigmoid(gate_a) * ya

        p = proj(2)
        if fine:
            yield
        gate_b = proj(7)
        yield
        p0 = POOL_CARRY + r0
        pext_ref[p0:p0 + n, :] = p
        pos1 = (j * tm + r0 + 1 + lax.broadcasted_iota(I32, (n, 1), 0))
        diffs = []
        for gi, w in enumerate(POOL_WINDOWS):
            cs = slice(gi * gd, (gi + 1) * gd)
            acc = p[:, cs]
            for k in range(1, w):
                acc = acc + pext_ref[p0 - k:p0 - k + n, cs]
            cnt = jnp.minimum(pos1, w).astype(F32)
            diffs.append((acc / cnt - p[:, cs]).astype(BF16))
            if fine:
                yield
        if not fine:
            yield
        yb = jnp.concatenate([_dot(dg, pw_ref[gi]) for gi, dg in enumerate(diffs)], axis=1) * ps_ref[...]
        mix = mix + jax.nn.sigmoid(gate_b) * yb

        conv_c = proj(5)
        if fine:
            yield
        conv_x = proj(3)
        if fine:
            yield
        conv_b = proj(4)
        if fine:
            yield
        gate_c = proj(8)
        yield
        z = conv_c * conv_x
        z0 = CONV_CARRY + r0
        zext_ref[z0:z0 + n, :] = z
        conv = cw_ref[CONV_K - 1:CONV_K, :] * z
        for k in range(CONV_K - 1):
            lag = CONV_K - 1 - k
            conv = conv + cw_ref[k:k + 1, :] * zext_ref[z0 - lag:z0 - lag + n, :]
        gated = (conv_b * conv).astype(BF16)
        yield
        yc = _dot(gated, wpc_ref[...])
        if fine:
            yield
        mix = mix + jax.nn.sigmoid(gate_c) * yc

        x_new = x + gt_ref[0] * _dot(mix.astype(BF16), wout_ref[...])
        o_ref[0, rs, :] = x_new
        _route_tile(x_new, sh2_ref[0], sc2_ref[0], n2_ref[...], rwt_ref[...], rb_ref[...],
                    hpk_ref.at[rs, :], idx_ref.at[:, rs], gw_ref.at[:, rs], rank_ref.at[:, rs], cnt_ref, carry_ref)

    sub = min(MIX_SUB_ROWS, tm)
    waiting = [rows_block(r0, sub) for r0 in range(0, tm, sub)]
    running = []
    while waiting or running:
        if waiting:
            running.append(waiting.pop(0))
        for block in list(running):
            if next(block, "done") == "done":
                running.remove(block)
    pext_ref[0:POOL_CARRY, :] = pext_ref[tm:tm + POOL_CARRY, :]
    zext_ref[0:CONV_CARRY, :] = zext_ref[tm:tm + CONV_CARRY, :]


def _mixer(x, x_b0, b0, bsz, pending, sh, sc, gt, sh2, sc2, n1, w_in, gng, ws, bs, wpa, pool_w, pool_scale, conv_w,
           wpc, w_out, n2, router_w, router_b):
    _, seq, d = x.shape
    n_tok = bsz * seq
    tm = min(MIX_ROWS, seq)
    nj = seq // tm
    vec = pl.BlockSpec((1, 1, d), lambda b, j: (b0 + b, 0, 0))
    tile = pl.BlockSpec((1, tm, d), lambda b, j: (b, j, 0))
    tile_in = pl.BlockSpec((1, tm, d), lambda b, j: (x_b0 + b, j, 0))
    per_tok = pl.BlockSpec((TOP_K, tm), lambda b, j: (0, b * nj + j))
    pending_specs, pending_args = [], []
    if pending is not None:
        moe, gt_prev = pending
        pending_specs = [pl.BlockSpec((tm, d), lambda b, j: (b * nj + j, 0)), vec]
        pending_args = [moe, gt_prev]
    return pl.pallas_call(
        functools.partial(_mixer_kernel, pending=pending is not None),
        grid=(bsz, nj),
        in_specs=pending_specs + [
            tile_in, vec, vec, vec, vec, vec,
            _resident((1, d)),
            _resident(w_in.shape),
            _resident((1, d)),
            _resident(ws.shape),
            _resident((CHUNK, GMLP_HEADS)),
            _resident(wpa.shape),
            _resident(pool_w.shape),
            _resident((1, d)),
            _resident(conv_w.shape),
            _resident(wpc.shape),
            _resident(w_out.shape),
            _resident((1, d)), _resident((N_EXPERTS, d)), _resident((N_EXPERTS, 1)),
        ],
        out_specs=[
            tile,
            pl.BlockSpec((tm, d // 2), lambda b, j: (b * nj + j, 0)),
            per_tok, per_tok, per_tok,
            pl.BlockSpec((N_EXPERTS, LANES), lambda b, j: (0, 0)),
        ],
        out_shape=[
            jax.ShapeDtypeStruct((bsz, seq, d), F32),
            jax.ShapeDtypeStruct((n_tok, d // 2), U32),
            jax.ShapeDtypeStruct((TOP_K, n_tok), I32),
            jax.ShapeDtypeStruct((TOP_K, n_tok), F32),
            jax.ShapeDtypeStruct((TOP_K, n_tok), I32),
            jax.ShapeDtypeStruct((N_EXPERTS, LANES), I32),
        ],
        scratch_shapes=[
            pltpu.VMEM((tm, d), F32),
            pltpu.VMEM((POOL_CARRY + tm, d), F32),
            pltpu.VMEM((CONV_CARRY + tm, d), F32),
            pltpu.VMEM((N_EXPERTS, LANES), F32),
        ],
        compiler_params=_params(2),
        name="mixer",
    )(*pending_args, x, sh, sc, gt, sh2, sc2, n1.reshape(1, d), w_in.astype(BF16), gng.reshape(1, d), ws, bs.T,
      wpa.astype(BF16), pool_w.astype(BF16), pool_scale.reshape(1, d), conv_w, wpc.astype(BF16),
      w_out.astype(BF16), n2.reshape(1, d), router_w.T.astype(BF16), router_b.reshape(N_EXPERTS, 1))


def _route_tile(x, sh, sc, n2, rwt, rb, hpk_ref, idx_ref, gw_ref, rank_ref, cnt_ref, carry_ref):
    tm = x.shape[0]

    h = _rms(x, n2) * (1.0 + sc) + sh
    hb = h.astype(BF16)
    hpk_ref[...] = _pack_bf16_pairs(h)

    logits = lax.dot_general(rwt, hb, (((1,), (1,)), ((), ())), preferred_element_type=F32) + rb
    iota_e = lax.broadcasted_iota(I32, logits.shape, 0)
    vals, idxs, sels = [], [], []
    rest = logits
    for _ in range(TOP_K):
        m = jnp.max(rest, axis=0, keepdims=True)
        ik = jnp.min(jnp.where(rest == m, iota_e, N_EXPERTS), axis=0, keepdims=True)
        sel = iota_e == ik
        rest = jnp.where(sel, -jnp.inf, rest)
        vals.append(m)
        idxs.append(ik)
        sels.append(sel)
    exps = [jnp.exp(v - vals[0]) for v in vals]
    denom = exps[0] + exps[1] + exps[2] + exps[3]

    chosen = jnp.logical_or(jnp.logical_or(sels[0], sels[1]), jnp.logical_or(sels[2], sels[3]))
    a = jnp.where(chosen, 1.0, 0.0)
    before = lax.broadcasted_iota(I32, (tm, tm), 0) < lax.broadcasted_iota(I32, (tm, tm), 1)
    prior = _dot(a.astype(BF16), jnp.where(before, 1.0, 0.0).astype(BF16)) + carry_ref[:, 0:1]
    for k in range(TOP_K):
        idx_ref[k:k + 1, :] = idxs[k]
        gw_ref[k:k + 1, :] = exps[k] / denom
        rank_ref[k:k + 1, :] = jnp.sum(jnp.where(sels[k], prior, 0.0), axis=0, keepdims=True).astype(I32)
    total = carry_ref[...] + jnp.sum(a, axis=1, keepdims=True)
    carry_ref[...] = total
    cnt_ref[...] = total.astype(I32)


def _dest_kernel(start_ref, idx_ref, rank_ref, o_ref):
    idx = idx_ref[...]
    base = jnp.zeros(idx.shape, I32)
    for e in range(N_EXPERTS):
        base = jnp.where(idx == e, start_ref[e], base)
    o_ref[...] = base + rank_ref[...]


def _dest(pad_start, idx, rank):
    n_tok = idx.shape[1]
    tc = min(DEST_COLS, n_tok)
    blk = pl.BlockSpec((TOP_K, tc), lambda i, s: (0, i))
    return pl.pallas_call(
        _dest_kernel,
        grid_spec=pltpu.PrefetchScalarGridSpec(
            num_scalar_prefetch=1, grid=(n_tok // tc,), in_specs=[blk, blk], out_specs=blk),
        out_shape=jax.ShapeDtypeStruct(idx.shape, I32),
        compiler_params=_params(1),
        name="dest",
    )(pad_start, idx, rank)


def _dispatch(hpk, dest, n_rows):
    n_tok, half = hpk.shape
    win = SC_WINDOW
    assert n_tok % (win * SC_WORKERS) == 0
    per_worker = n_tok // win // SC_WORKERS
    mesh = plsc.VectorSubcoreMesh(core_axis_name="c", subcore_axis_name="s")

    @pl.kernel(out_type=jax.ShapeDtypeStruct((n_rows, half), U32), mesh=mesh,
               scratch_types=[pltpu.VMEM((win, half), U32), pltpu.VMEM((TOP_K, win), I32),
                              pltpu.SemaphoreType.DMA((TOP_K + 1,))])
    def scatter_rows(hpk_hbm, dest_hbm, xs_hbm, xbuf, ibuf, sem):
        worker = lax.axis_index("c") * SC_SUBCORES + lax.axis_index("s")

        @pl.loop(0, per_worker)
        def _(j):
            t0 = pl.multiple_of((worker * per_worker + j) * win, win)
            rows_in = pltpu.async_copy(hpk_hbm.at[pl.ds(t0, win)], xbuf, sem.at[TOP_K])
            pltpu.sync_copy(dest_hbm.at[:, pl.ds(t0, win)], ibuf)
            rows_in.wait()
            scatters = [pltpu.async_copy(xbuf, xs_hbm.at[ibuf.at[k]], sem.at[k]) for k in range(TOP_K)]
            for scatter in scatters:
                scatter.wait()

    return scatter_rows(hpk, dest)


def _expert_kernel(be_ref, nv_ref, live_ref, first_ref, slot_ref, next_ref, xs_ref, wgu_hbm, bgu_ref, wd_hbm,
                   bd_ref, ys_ref, wgu_f32, wd_f32, wgu_bf, wd_bf, sem, *, layer):
    step = pl.program_id(0)
    tb = EXPERT_ROWS

    def weight_copies(expert, slot):
        return (pltpu.make_async_copy(wgu_hbm.at[layer, expert], wgu_f32.at[slot], sem.at[0, slot]),
                pltpu.make_async_copy(wd_hbm.at[layer, expert], wd_f32.at[slot], sem.at[1, slot]))

    @pl.when(step == 0)
    def _():
        for copy in weight_copies(be_ref[0], 0):
            copy.start()

    for sub in range(xs_ref.shape[0] // tb):
        i = step * (xs_ref.shape[0] // tb) + sub
        rows = slice(sub * tb, (sub + 1) * tb)

        @pl.when(i >= nv_ref[0])
        def _():
            ys_ref[rows, :] = jnp.zeros((tb, ys_ref.shape[1]), U32)

        @pl.when(first_ref[i] == 1)
        def _():
            slot = slot_ref[i]

            @pl.when(next_ref[i] >= 0)
            def _():
                for copy in weight_copies(next_ref[i], 1 - slot):
                    copy.start()

            for copy in weight_copies(be_ref[i], slot):
                copy.wait()
            wgu_bf[...] = wgu_f32[slot].astype(BF16)
            wd_bf[...] = wd_f32[slot].astype(BF16)

        @pl.when(i < nv_ref[0])
        def _():
            live = lax.broadcasted_iota(I32, (tb, 1), 0) < live_ref[i]
            xb = jnp.where(live, _unpack_bf16_pairs(xs_ref[rows, :]), 0.0).astype(BF16)
            gu = _dot(xb, wgu_bf[...]) + bgu_ref[layer, be_ref[i]]
            ff = gu.shape[1] // 2
            gate = jnp.minimum(gu[:, :ff], SWIGLU_LIMIT)
            up = jnp.clip(gu[:, ff:], -SWIGLU_LIMIT, SWIGLU_LIMIT)
            glu = gate * jax.nn.sigmoid(SWIGLU_ALPHA * gate)
            y = _dot(((up + 1.0) * glu).astype(BF16), wd_bf[...]) + bd_ref[layer, be_ref[i]]
            ys_ref[rows, :] = _pack_bf16_pairs(y)


def _experts(layer, plan, xs, w_gu, b_gu, w_down, b_down):
    n_rows, half = xs.shape
    depth, _, d, two_f = w_gu.shape
    step_rows = EXPERT_ROWS * EXPERT_BLOCKS_PER_STEP

    def rows(i, be, nv, *_):
        return (jnp.minimum(i, (nv[0] - 1) // EXPERT_BLOCKS_PER_STEP), 0)

    return pl.pallas_call(
        functools.partial(_expert_kernel, layer=layer),
        grid_spec=pltpu.PrefetchScalarGridSpec(
            num_scalar_prefetch=len(plan),
            grid=(n_rows // step_rows,),
            in_specs=[
                pl.BlockSpec((step_rows, half), rows),
                pl.BlockSpec(memory_space=pl.ANY),
                _resident((depth, N_EXPERTS, 1, two_f)),
                pl.BlockSpec(memory_space=pl.ANY),
                _resident((depth, N_EXPERTS, 1, d)),
            ],
            out_specs=pl.BlockSpec((step_rows, half), lambda i, *_: (i, 0)),
            scratch_shapes=[
                pltpu.VMEM((2, d, two_f), F32), pltpu.VMEM((2, two_f // 2, d), F32),
                pltpu.VMEM((d, two_f), BF16), pltpu.VMEM((two_f // 2, d), BF16),
                pltpu.SemaphoreType.DMA((2, 2)),
            ],
        ),
        out_shape=jax.ShapeDtypeStruct((n_rows, half), U32),
        compiler_params=_params(1),
        name="experts",
    )(*plan, xs, w_gu, b_gu.reshape(depth, N_EXPERTS, 1, two_f), w_down, b_down.reshape(depth, N_EXPERTS, 1, d))


def _gather_rows(dest, ys):
    n_tok = dest.shape[1]
    width = ys.shape[1]
    win = SC_WINDOW
    assert n_tok % (win * SC_WORKERS) == 0
    per_worker = n_tok // win // SC_WORKERS
    mesh = plsc.VectorSubcoreMesh(core_axis_name="c", subcore_axis_name="s")

    part = win // SC_GATHER_PARTS
    n_items = TOP_K * SC_GATHER_PARTS
    n_buf = SC_GATHER_BUFFERS

    @pl.kernel(out_type=jax.ShapeDtypeStruct((TOP_K, n_tok, width), ys.dtype), mesh=mesh,
               scratch_types=[pltpu.VMEM((n_buf, part, width), ys.dtype), pltpu.VMEM((TOP_K, win), I32),
                              pltpu.SemaphoreType.DMA((2, n_buf))])
    def gather(ys_hbm, idx_hbm, g_hbm, buf, ibuf, sem):
        worker = lax.axis_index("c") * SC_SUBCORES + lax.axis_index("s")

        @pl.loop(0, per_worker)
        def _(j):
            t0 = pl.multiple_of((worker * per_worker + j) * win, win)
            pltpu.sync_copy(idx_hbm.at[:, pl.ds(t0, win)], ibuf)

            def fetch(n):
                k, h = divmod(n, SC_GATHER_PARTS)
                return pltpu.async_copy(ys_hbm.at[ibuf.at[k, pl.ds(h * part, part)]], buf.at[n % n_buf],
                                        sem.at[0, n % n_buf])

            def store(n):
                k, h = divmod(n, SC_GATHER_PARTS)
                return pltpu.async_copy(buf.at[n % n_buf], g_hbm.at[k, pl.ds(t0 + h * part, part)],
                                        sem.at[1, n % n_buf])

            fetches = {n: fetch(n) for n in range(n_buf - 1)}
            stores = {}
            for n in range(n_items):
                fetches[n].wait()
                stores[n] = store(n)
                nxt = n + n_buf - 1
                if nxt < n_items:
                    if n >= 1:
                        stores.pop(n - 1).wait()
                    fetches[nxt] = fetch(nxt)
            for pending_store in stores.values():
                pending_store.wait()

    return gather(ys, dest)


def _gather_combine(dest, gw, ys):
    n_tok = dest.shape[1]
    half = ys.shape[1]
    win = SC_WINDOW
    part = SC_COMBINE_PART
    assert n_tok % (win * SC_WORKERS) == 0
    per_worker = n_tok // win // SC_WORKERS
    mesh = plsc.VectorSubcoreMesh(core_axis_name="c", subcore_axis_name="s")

    @pl.kernel(out_type=jax.ShapeDtypeStruct((n_tok, 2 * half), F32), mesh=mesh,
               compiler_params=pltpu.CompilerParams(needs_layout_passes=False),
               scratch_types=[pltpu.VMEM((TOP_K, part, half), U32), pltpu.VMEM((part, 2 * half), F32),
                              pltpu.VMEM((TOP_K, win), I32), pltpu.VMEM((TOP_K, win), F32),
                              pltpu.SemaphoreType.DMA((TOP_K,))])
    def gather_sum(ys_hbm, idx_hbm, gw_hbm, o_hbm, rows, obuf, ibuf, wbuf, sem):
        worker = lax.axis_index("c") * SC_SUBCORES + lax.axis_index("s")

        @pl.loop(0, per_worker)
        def _(j):
            t0 = pl.multiple_of((worker * per_worker + j) * win, win)
            pltpu.sync_copy(idx_hbm.at[:, pl.ds(t0, win)], ibuf)
            pltpu.sync_copy(gw_hbm.at[:, pl.ds(t0, win)], wbuf)
            for p in range(win // part):
                fetches = [pltpu.async_copy(ys_hbm.at[ibuf.at[k, pl.ds(p * part, part)]], rows.at[k], sem.at[k])
                           for k in range(TOP_K)]
                for fetch in fetches:
                    fetch.wait()

                @pl.loop(0, part)
                def _(t):
                    col = jnp.full((SC_LANES,), p * part, I32) + t
                    wts = [plsc.load_gather(wbuf, [jnp.full((SC_LANES,), k, I32), col]) for k in range(TOP_K)]

                    @plsc.parallel_loop(0, half // SC_LANES, unroll=4)
                    def _(v):
                        words = pl.ds(v * SC_LANES, SC_LANES)
                        lo = jnp.zeros((SC_LANES,), F32)
                        hi = jnp.zeros((SC_LANES,), F32)
                        for k in range(TOP_K):
                            w = rows[k, t, words]
                            lo = lo + wts[k] * plsc.bitcast(w << 16, F32)
                            hi = hi + wts[k] * plsc.bitcast(w & jnp.uint32(0xFFFF0000), F32)
                        obuf[t, words] = lo
                        obuf[t, pl.ds(half + v * SC_LANES, SC_LANES)] = hi

                pltpu.sync_copy(obuf, o_hbm.at[pl.ds(t0 + p * part, part)])

    return gather_sum(ys, dest, gw)


def _final_kernel(g_ref, x_ref, gw_ref, gt_ref, fg_ref, *rest):
    o_ref = rest[-1]
    y = x_ref[...] + gt_ref[0] * _weighted_rows(g_ref, gw_ref[...].T)
    o_ref[...] = _rms(y, fg_ref[...])


def _final(out, g, x, gw, gt, final_g, b0, batch):
    bsz, seq, d = x.shape
    tm = min(FINAL_ROWS, seq)
    per_seq = seq // tm
    base = b0 * per_seq
    in_specs = [
        pl.BlockSpec((TOP_K, tm, d // 2), lambda i: (0, i, 0)),
        pl.BlockSpec((tm, d), lambda i: (i, 0)),
        pl.BlockSpec((TOP_K, tm), lambda i: (0, i)),
        pl.BlockSpec((1, 1, d), lambda i: (b0 + i // per_seq, 0, 0)),
        pl.BlockSpec((1, d), lambda i: (0, 0)),
    ]
    args = [g, x.reshape(bsz * seq, d), gw, gt, final_g.reshape(1, d)]
    aliases = {}
    if out is not None:
        in_specs.append(pl.BlockSpec(memory_space=pl.ANY))
        args.append(out)
        aliases = {len(args) - 1: 0}
    return pl.pallas_call(
        _final_kernel,
        grid=(bsz * per_seq,),
        in_specs=in_specs,
        out_specs=pl.BlockSpec((tm, d), lambda i: (base + i, 0)),
        out_shape=jax.ShapeDtypeStruct((batch * seq, d), F32),
        input_output_aliases=aliases,
        compiler_params=_params(1),
        name="final",
    )(*args)


def _plan_kernel(cnt_ref, start_ref, be_ref, nv_ref, live_ref, first_ref, slot_ref, next_ref):
    tb = EXPERT_ROWS
    n_blocks = be_ref.shape[0]

    def per_expert(e, carry):
        blk0, group = carry
        count = cnt_ref[e]
        n_blk = (count + (tb - 1)) // tb
        start_ref[e] = blk0 * tb

        def per_block(b, c):
            i = blk0 + b
            be_ref[i] = e
            live_ref[i] = jnp.minimum(count - b * tb, tb)
            first_ref[i] = jnp.where(b == 0, 1, 0)
            slot_ref[i] = group % 2
            return c

        lax.fori_loop(0, n_blk, per_block, 0)
        return blk0 + n_blk, group + jnp.where(n_blk > 0, 1, 0)

    used, _ = lax.fori_loop(0, N_EXPERTS, per_expert, (jnp.int32(0), jnp.int32(0)))
    nv_ref[0] = used

    def backwards(t, nxt):
        i = used - 1 - t
        after = be_ref[jnp.minimum(i + 1, used - 1)]
        nxt = jnp.where(jnp.logical_and(i < used - 1, after != be_ref[i]), after, nxt)
        next_ref[i] = nxt
        return nxt

    lax.fori_loop(0, used, backwards, jnp.int32(-1))
    last = be_ref[used - 1]

    def tail(i, c):
        be_ref[i] = last
        live_ref[i] = 0
        first_ref[i] = 0
        slot_ref[i] = 0
        next_ref[i] = -1
        return c

    lax.fori_loop(used, n_blocks, tail, 0)


def _plan(counts, n_blocks):
    smem = pl.BlockSpec(memory_space=pltpu.SMEM)
    per_block = jax.ShapeDtypeStruct((n_blocks,), I32)
    return pl.pallas_call(
        _plan_kernel,
        in_specs=[smem],
        out_specs=[smem] * 7,
        out_shape=[jax.ShapeDtypeStruct((N_EXPERTS,), I32), per_block, jax.ShapeDtypeStruct((1,), I32),
                   per_block, per_block, per_block, per_block],
        name="plan",
    )(counts)


def _moe(layer, routed, w_gu, b_gu, w_down, b_down, combine):
    n_tok = routed[0].shape[0]
    tb = EXPERT_ROWS
    step_rows = tb * EXPERT_BLOCKS_PER_STEP
    n_rows = -(-(n_tok * TOP_K + N_EXPERTS * (tb - 1)) // step_rows) * step_rows
    hpk, idx, gw, rank, cnt = routed
    pad_start, *plan = _plan(cnt[:, 0], n_rows // tb)
    dest = _dest(pad_start, idx, rank)
    xs = _dispatch(hpk, dest, n_rows)
    ys = _experts(layer, plan, xs, w_gu, b_gu, w_down, b_down)
    return _gather_combine(dest, gw, ys) if combine else _gather_rows(dest, ys)


def kernel(x, c, norm1_g, ada_w, ada_b, w_in, gmlp_norm_g, gmlp_ws, gmlp_bs, w_proj_a, pool_w, pool_scale,
           conv_w, w_proj_c, w_out, norm2_g, router_w, router_b, exp_w_gu, exp_b_gu, exp_w_down,
           exp_b_down, final_g):
    depth = ada_w.shape[0]
    bsz, seq, d = x.shape
    mods = _ada(c, ada_w, ada_b)
    group = bsz // BATCH_GROUPS
    streams = [x] * BATCH_GROUPS
    pending = [None] * BATCH_GROUPS
    g2 = None
    for l in range(depth):
        g2_prev = g2
        sh1, sc1, g1, sh2, sc2, g2 = [mods[l, :, i * d:(i + 1) * d].reshape(bsz, 1, d) for i in range(6)]
        routed = [None] * BATCH_GROUPS
        for h in range(BATCH_GROUPS):
            held = None if pending[h] is None else (pending[h], g2_prev)
            streams[h], *routed[h] = _mixer(
                streams[h], h * group if l == 0 else 0, h * group, group, held, sh1, sc1, g1, sh2, sc2,
                norm1_g[l], w_in[l], gmlp_norm_g[l], gmlp_ws[l], gmlp_bs[l], w_proj_a[l], pool_w[l],
                pool_scale[l], conv_w[l], w_proj_c[l], w_out[l], norm2_g[l], router_w[l], router_b[l])
        for h in range(BATCH_GROUPS):
            pending[h] = _moe(l, routed[h], exp_w_gu, exp_b_gu, exp_w_down, exp_b_down, l < depth - 1)
    out = None
    for h in range(BATCH_GROUPS):
        out = _final(out, pending[h], streams[h], routed[h][2], g2, final_g, h * group, bsz)
    return out.reshape(bsz, seq, d)
```

```python
import functools

import jax
import jax.numpy as jnp
from jax import lax
from jax.experimental import pallas as pl
from jax.experimental.pallas import tpu as pltpu
from jax.experimental.pallas import tpu_sc as plsc

F32 = jnp.float32
BF16 = jnp.bfloat16
I32 = jnp.int32
U32 = jnp.uint32

RMS_EPS = 1e-5
GMLP_HEADS = 8
CHUNK = 128
POOL_WINDOWS = (2, 4, 8, 16)
POOL_CARRY = 16
CONV_K = 3
CONV_CARRY = 8
N_EXPERTS = 32
TOP_K = 4
SWIGLU_LIMIT = 7.0
SWIGLU_ALPHA = 1.702

LANES = 128

MIX_ROWS = 512
MIX_SUB_ROWS = 256
DEST_COLS = 4096
EXPERT_ROWS = 256
EXPERT_BLOCKS_PER_STEP = 4
FINAL_ROWS = 512
BATCH_GROUPS = 2
SC_SUBCORES = 16
SC_WORKERS = 2 * SC_SUBCORES
SC_WINDOW = 128
SC_GATHER_PARTS = 2
SC_GATHER_BUFFERS = 3
SC_LANES = 16
SC_COMBINE_PART = 32

VMEM_LIMIT = 58 * 1024 * 1024


def _rms(x, g):
    return x * lax.rsqrt(jnp.mean(x * x, axis=-1, keepdims=True) + RMS_EPS) * g


def _dot(a, b):
    return jnp.dot(a, b, preferred_element_type=F32)


def _params(n_axes, vmem=VMEM_LIMIT):
    return pltpu.CompilerParams(dimension_semantics=("arbitrary",) * n_axes, vmem_limit_bytes=vmem)


def _pack_bf16_pairs(x):
    half = x.shape[1] // 2
    bits = lax.bitcast_convert_type(x.astype(BF16).astype(F32), U32)
    return (bits[:, :half] >> 16) | bits[:, half:]


def _unpack_bf16_pairs(w):
    lo = lax.bitcast_convert_type(w << 16, F32)
    hi = lax.bitcast_convert_type(w & jnp.uint32(0xFFFF0000), F32)
    return jnp.concatenate([lo, hi], axis=1)


def _resident(shape):
    zeros = (0,) * len(shape)
    return pl.BlockSpec(shape, lambda *_: zeros, pipeline_mode=pl.Buffered(1))


def _ada_kernel(c_ref, w_ref, b_ref, o_ref):
    c = c_ref[...]
    cond = c * jax.nn.sigmoid(c)
    o_ref[0] = _dot(cond.astype(BF16), w_ref[0].astype(BF16)) + b_ref[0]


def _ada(c, ada_w, ada_b):
    depth, d, six_d = ada_w.shape
    bsz = c.shape[0]
    return pl.pallas_call(
        _ada_kernel,
        grid=(depth, six_d // d),
        in_specs=[
            pl.BlockSpec((bsz, d), lambda l, j: (0, 0)),
            pl.BlockSpec((1, d, d), lambda l, j: (l, 0, j)),
            pl.BlockSpec((1, 1, d), lambda l, j: (l, 0, j)),
        ],
        out_specs=pl.BlockSpec((1, bsz, d), lambda l, j: (l, 0, j)),
        out_shape=jax.ShapeDtypeStruct((depth, bsz, six_d), F32),
        compiler_params=_params(2),
        name="ada",
    )(c, ada_w, ada_b.reshape(depth, 1, six_d))


def _weighted_rows(g_ref, gw):
    acc = gw[:, 0:1] * _unpack_bf16_pairs(g_ref[0])
    for k in range(1, TOP_K):
        acc = acc + gw[:, k:k + 1] * _unpack_bf16_pairs(g_ref[k])
    return acc


def _mixer_kernel(*refs, pending):
    if pending:
        moe_ref, gtp_ref, *refs = refs
    (x_ref, sh_ref, sc_ref, gt_ref, sh2_ref, sc2_ref, n1_ref, win_ref, gng_ref, ws_ref, bst_ref,
     wpa_ref, pw_ref, ps_ref, cw_ref, wpc_ref, wout_ref, n2_ref, rwt_ref, rb_ref,
     o_ref, hpk_ref, idx_ref, gw_ref, rank_ref, cnt_ref,
     s_ref, pext_ref, zext_ref, carry_ref) = refs
    tm, d = x_ref.shape[1], x_ref.shape[2]
    j = pl.program_id(1)

    @pl.when(j == 0)
    def _():
        pext_ref[0:POOL_CARRY, :] = jnp.zeros((POOL_CARRY, d), F32)
        zext_ref[0:CONV_CARRY, :] = jnp.zeros((CONV_CARRY, d), F32)

    @pl.when(jnp.logical_and(pl.program_id(0) == 0, j == 0))
    def _():
        carry_ref[...] = jnp.zeros(carry_ref.shape, F32)

    hd = d // GMLP_HEADS
    gd = d // len(POOL_WINDOWS)
    row = lax.broadcasted_iota(I32, (CHUNK, CHUNK), 0)
    col = lax.broadcasted_iota(I32, (CHUNK, CHUNK), 1)
    spatial_w = [jnp.where(row >= col, ws_ref[h], 0.0).astype(BF16) for h in range(GMLP_HEADS)]

    def rows_block(r0, n):
        rs = slice(r0, r0 + n)
        x = x_ref[0, rs, :]
        if pending:
            x = x + gtp_ref[0] * moe_ref[rs, :]
        hb = (_rms(x, n1_ref[...]) * (1.0 + sc_ref[0]) + sh_ref[0]).astype(BF16)

        def proj(c):
            return _dot(hb, win_ref[:, c * d:(c + 1) * d])

        u_raw = proj(0)
        yield
        v_raw = proj(1)
        yield
        u = jax.nn.gelu(u_raw)
        yield
        vb = _rms(jax.nn.gelu(v_raw), gng_ref[...]).astype(BF16)
        yield
        for h in range(GMLP_HEADS):
            bias = bst_ref[:, h:h + 1]
            for ci in range(n // CHUNK):
                blk = vb[ci * CHUNK:(ci + 1) * CHUNK, h * hd:(h + 1) * hd]
                s_ref[r0 + ci * CHUNK:r0 + (ci + 1) * CHUNK, h * hd:(h + 1) * hd] = _dot(spatial_w[h], blk) + bias
        yield
        gate_a = proj(6)
        yield
        ya = _dot((u * s_ref[rs, :]).astype(BF16), wpa_ref[...])
        yield
        mix = jax.nn.sigmoid(gate_a) * ya

        p = proj(2)
        yield
        gate_b = proj(7)
        yield
        p0 = POOL_CARRY + r0
        pext_ref[p0:p0 + n, :] = p
        pos1 = (j * tm + r0 + 1 + lax.broadcasted_iota(I32, (n, 1), 0))
        diffs = []
        for gi, w in enumerate(POOL_WINDOWS):
            cs = slice(gi * gd, (gi + 1) * gd)
            acc = p[:, cs]
            for k in range(1, w):
                acc = acc + pext_ref[p0 - k:p0 - k + n, cs]
            cnt = jnp.minimum(pos1, w).astype(F32)
            diffs.append((acc / cnt - p[:, cs]).astype(BF16))
            yield
        yb = jnp.concatenate([_dot(dg, pw_ref[gi]) for gi, dg in enumerate(diffs)], axis=1) * ps_ref[...]
        mix = mix + jax.nn.sigmoid(gate_b) * yb

        conv_c = proj(5)
        yield
        conv_x = proj(3)
        yield
        conv_b = proj(4)
        yield
        gate_c = proj(8)
        yield
        z = conv_c * conv_x
        z0 = CONV_CARRY + r0
        zext_ref[z0:z0 + n, :] = z
        conv = cw_ref[CONV_K - 1:CONV_K, :] * z
        for k in range(CONV_K - 1):
            lag = CONV_K - 1 - k
            conv = conv + cw_ref[k:k + 1, :] * zext_ref[z0 - lag:z0 - lag + n, :]
        gated = (conv_b * conv).astype(BF16)
        yield
        yc = _dot(gated, wpc_ref[...])
        yield
        mix = mix + jax.nn.sigmoid(gate_c) * yc

        x_new = x + gt_ref[0] * _dot(mix.astype(BF16), wout_ref[...])
        o_ref[0, rs, :] = x_new
        _route_tile(x_new, sh2_ref[0], sc2_ref[0], n2_ref[...], rwt_ref[...], rb_ref[...],
                    hpk_ref.at[rs, :], idx_ref.at[:, rs], gw_ref.at[:, rs], rank_ref.at[:, rs], cnt_ref, carry_ref)

    sub = min(MIX_SUB_ROWS, tm)
    waiting = [rows_block(r0, sub) for r0 in range(0, tm, sub)]
    running = []
    while waiting or running:
        if waiting:
            running.append(waiting.pop(0))
        for block in list(running):
            if next(block, "done") == "done":
                running.remove(block)
    pext_ref[0:POOL_CARRY, :] = pext_ref[tm:tm + POOL_CARRY, :]
    zext_ref[0:CONV_CARRY, :] = zext_ref[tm:tm + CONV_CARRY, :]


def _mixer(x, x_b0, b0, bsz, pending, sh, sc, gt, sh2, sc2, n1, w_in, gng, ws, bs, wpa, pool_w, pool_scale, conv_w,
           wpc, w_out, n2, router_w, router_b):
    _, seq, d = x.shape
    n_tok = bsz * seq
    tm = min(MIX_ROWS, seq)
    nj = seq // tm
    vec = pl.BlockSpec((1, 1, d), lambda b, j: (b0 + b, 0, 0))
    tile = pl.BlockSpec((1, tm, d), lambda b, j: (b, j, 0))
    tile_in = pl.BlockSpec((1, tm, d), lambda b, j: (x_b0 + b, j, 0))
    per_tok = pl.BlockSpec((TOP_K, tm), lambda b, j: (0, b * nj + j))
    pending_specs, pending_args = [], []
    if pending is not None:
        moe, gt_prev = pending
        pending_specs = [pl.BlockSpec((tm, d), lambda b, j: (b * nj + j, 0)), vec]
        pending_args = [moe, gt_prev]
    return pl.pallas_call(
        functools.partial(_mixer_kernel, pending=pending is not None),
        grid=(bsz, nj),
        in_specs=pending_specs + [
            tile_in, vec, vec, vec, vec, vec,
            _resident((1, d)),
            _resident(w_in.shape),
            _resident((1, d)),
            _resident(ws.shape),
            _resident((CHUNK, GMLP_HEADS)),
            _resident(wpa.shape),
            _resident(pool_w.shape),
            _resident((1, d)),
            _resident(conv_w.shape),
            _resident(wpc.shape),
            _resident(w_out.shape),
            _resident((1, d)), _resident((N_EXPERTS, d)), _resident((N_EXPERTS, 1)),
        ],
        out_specs=[
            tile,
            pl.BlockSpec((tm, d // 2), lambda b, j: (b * nj + j, 0)),
            per_tok, per_tok, per_tok,
            pl.BlockSpec((N_EXPERTS, LANES), lambda b, j: (0, 0)),
        ],
        out_shape=[
            jax.ShapeDtypeStruct((bsz, seq, d), F32),
            jax.ShapeDtypeStruct((n_tok, d // 2), U32),
            jax.ShapeDtypeStruct((TOP_K, n_tok), I32),
            jax.ShapeDtypeStruct((TOP_K, n_tok), F32),
            jax.ShapeDtypeStruct((TOP_K, n_tok), I32),
            jax.ShapeDtypeStruct((N_EXPERTS, LANES), I32),
        ],
        scratch_shapes=[
            pltpu.VMEM((tm, d), F32),
            pltpu.VMEM((POOL_CARRY + tm, d), F32),
            pltpu.VMEM((CONV_CARRY + tm, d), F32),
            pltpu.VMEM((N_EXPERTS, LANES), F32),
        ],
        compiler_params=_params(2),
        name="mixer",
    )(*pending_args, x, sh, sc, gt, sh2, sc2, n1.reshape(1, d), w_in.astype(BF16), gng.reshape(1, d), ws, bs.T,
      wpa.astype(BF16), pool_w.astype(BF16), pool_scale.reshape(1, d), conv_w, wpc.astype(BF16),
      w_out.astype(BF16), n2.reshape(1, d), router_w.T.astype(BF16), router_b.reshape(N_EXPERTS, 1))


def _route_tile(x, sh, sc, n2, rwt, rb, hpk_ref, idx_ref, gw_ref, rank_ref, cnt_ref, carry_ref):
    tm = x.shape[0]

    h = _rms(x, n2) * (1.0 + sc) + sh
    hb = h.astype(BF16)
    hpk_ref[...] = _pack_bf16_pairs(h)

    logits = lax.dot_general(rwt, hb, (((1,), (1,)), ((), ())), preferred_element_type=F32) + rb
    iota_e = lax.broadcasted_iota(I32, logits.shape, 0)
    vals, idxs, sels = [], [], []
    rest = logits
    for _ in range(TOP_K):
        m = jnp.max(rest, axis=0, keepdims=True)
        ik = jnp.min(jnp.where(rest == m, iota_e, N_EXPERTS), axis=0, keepdims=True)
        sel = iota_e == ik
        rest = jnp.where(sel, -jnp.inf, rest)
        vals.append(m)
        idxs.append(ik)
        sels.append(sel)
    exps = [jnp.exp(v - vals[0]) for v in vals]
    denom = exps[0] + exps[1] + exps[2] + exps[3]

    chosen = jnp.logical_or(jnp.logical_or(sels[0], sels[1]), jnp.logical_or(sels[2], sels[3]))
    a = jnp.where(chosen, 1.0, 0.0)
    before = lax.broadcasted_iota(I32, (tm, tm), 0) < lax.broadcasted_iota(I32, (tm, tm), 1)
    prior = _dot(a.astype(BF16), jnp.where(before, 1.0, 0.0).astype(BF16)) + carry_ref[:, 0:1]
    for k in range(TOP_K):
        idx_ref[k:k + 1, :] = idxs[k]
        gw_ref[k:k + 1, :] = exps[k] / denom
        rank_ref[k:k + 1, :] = jnp.sum(jnp.where(sels[k], prior, 0.0), axis=0, keepdims=True).astype(I32)
    total = carry_ref[...] + jnp.sum(a, axis=1, keepdims=True)
    carry_ref[...] = total
    cnt_ref[...] = total.astype(I32)


def _dest_kernel(start_ref, idx_ref, rank_ref, o_ref):
    idx = idx_ref[...]
    base = jnp.zeros(idx.shape, I32)
    for e in range(N_EXPERTS):
        base = jnp.where(idx == e, start_ref[e], base)
    o_ref[...] = base + rank_ref[...]


def _dest(pad_start, idx, rank):
    n_tok = idx.shape[1]
    tc = min(DEST_COLS, n_tok)
    blk = pl.BlockSpec((TOP_K, tc), lambda i, s: (0, i))
    return pl.pallas_call(
        _dest_kernel,
        grid_spec=pltpu.PrefetchScalarGridSpec(
            num_scalar_prefetch=1, grid=(n_tok // tc,), in_specs=[blk, blk], out_specs=blk),
        out_shape=jax.ShapeDtypeStruct(idx.shape, I32),
        compiler_params=_params(1),
        name="dest",
    )(pad_start, idx, rank)


def _dispatch(hpk, dest, n_rows):
    n_tok, half = hpk.shape
    win = SC_WINDOW
    assert n_tok % (win * SC_WORKERS) == 0
    per_worker = n_tok // win // SC_WORKERS
    mesh = plsc.VectorSubcoreMesh(core_axis_name="c", subcore_axis_name="s")

    @pl.kernel(out_type=jax.ShapeDtypeStruct((n_rows, half), U32), mesh=mesh,
               scratch_types=[pltpu.VMEM((win, half), U32), pltpu.VMEM((TOP_K, win), I32),
                              pltpu.SemaphoreType.DMA((TOP_K + 1,))])
    def scatter_rows(hpk_hbm, dest_hbm, xs_hbm, xbuf, ibuf, sem):
        worker = lax.axis_index("c") * SC_SUBCORES + lax.axis_index("s")

        @pl.loop(0, per_worker)
        def _(j):
            t0 = pl.multiple_of((worker * per_worker + j) * win, win)
            rows_in = pltpu.async_copy(hpk_hbm.at[pl.ds(t0, win)], xbuf, sem.at[TOP_K])
            pltpu.sync_copy(dest_hbm.at[:, pl.ds(t0, win)], ibuf)
            rows_in.wait()
            scatters = [pltpu.async_copy(xbuf, xs_hbm.at[ibuf.at[k]], sem.at[k]) for k in range(TOP_K)]
            for scatter in scatters:
                scatter.wait()

    return scatter_rows(hpk, dest)


def _expert_kernel(be_ref, nv_ref, live_ref, first_ref, slot_ref, next_ref, xs_ref, wgu_hbm, bgu_ref, wd_hbm,
                   bd_ref, ys_ref, wgu_f32, wd_f32, wgu_bf, wd_bf, sem, *, layer):
    step = pl.program_id(0)
    tb = EXPERT_ROWS

    def weight_copies(expert, slot):
        return (pltpu.make_async_copy(wgu_hbm.at[layer, expert], wgu_f32.at[slot], sem.at[0, slot]),
                pltpu.make_async_copy(wd_hbm.at[layer, expert], wd_f32.at[slot], sem.at[1, slot]))

    @pl.when(step == 0)
    def _():
        for copy in weight_copies(be_ref[0], 0):
            copy.start()

    for sub in range(xs_ref.shape[0] // tb):
        i = step * (xs_ref.shape[0] // tb) + sub
        rows = slice(sub * tb, (sub + 1) * tb)

        @pl.when(i >= nv_ref[0])
        def _():
            ys_ref[rows, :] = jnp.zeros((tb, ys_ref.shape[1]), U32)

        @pl.when(first_ref[i] == 1)
        def _():
            slot = slot_ref[i]

            @pl.when(next_ref[i] >= 0)
            def _():
                for copy in weight_copies(next_ref[i], 1 - slot):
                    copy.start()

            for copy in weight_copies(be_ref[i], slot):
                copy.wait()
            wgu_bf[...] = wgu_f32[slot].astype(BF16)
            wd_bf[...] = wd_f32[slot].astype(BF16)

        @pl.when(i < nv_ref[0])
        def _():
            live = lax.broadcasted_iota(I32, (tb, 1), 0) < live_ref[i]
            xb = jnp.where(live, _unpack_bf16_pairs(xs_ref[rows, :]), 0.0).astype(BF16)
            gu = _dot(xb, wgu_bf[...]) + bgu_ref[layer, be_ref[i]]
            ff = gu.shape[1] // 2
            gate = jnp.minimum(gu[:, :ff], SWIGLU_LIMIT)
            up = jnp.clip(gu[:, ff:], -SWIGLU_LIMIT, SWIGLU_LIMIT)
            glu = gate * jax.nn.sigmoid(SWIGLU_ALPHA * gate)
            y = _dot(((up + 1.0) * glu).astype(BF16), wd_bf[...]) + bd_ref[layer, be_ref[i]]
            ys_ref[rows, :] = _pack_bf16_pairs(y)


def _experts(layer, plan, xs, w_gu, b_gu, w_down, b_down):
    n_rows, half = xs.shape
    depth, _, d, two_f = w_gu.shape
    step_rows = EXPERT_ROWS * EXPERT_BLOCKS_PER_STEP

    def rows(i, be, nv, *_):
        return (jnp.minimum(i, (nv[0] - 1) // EXPERT_BLOCKS_PER_STEP), 0)

    return pl.pallas_call(
        functools.partial(_expert_kernel, layer=layer),
        grid_spec=pltpu.PrefetchScalarGridSpec(
            num_scalar_prefetch=len(plan),
            grid=(n_rows // step_rows,),
            in_specs=[
                pl.BlockSpec((step_rows, half), rows),
                pl.BlockSpec(memory_space=pl.ANY),
                _resident((depth, N_EXPERTS, 1, two_f)),
                pl.BlockSpec(memory_space=pl.ANY),
                _resident((depth, N_EXPERTS, 1, d)),
            ],
            out_specs=pl.BlockSpec((step_rows, half), lambda i, *_: (i, 0)),
            scratch_shapes=[
                pltpu.VMEM((2, d, two_f), F32), pltpu.VMEM((2, two_f // 2, d), F32),
                pltpu.VMEM((d, two_f), BF16), pltpu.VMEM((two_f // 2, d), BF16),
                pltpu.SemaphoreType.DMA((2, 2)),
            ],
        ),
        out_shape=jax.ShapeDtypeStruct((n_rows, half), U32),
        compiler_params=_params(1),
        name="experts",
    )(*plan, xs, w_gu, b_gu.reshape(depth, N_EXPERTS, 1, two_f), w_down, b_down.reshape(depth, N_EXPERTS, 1, d))


def _gather_rows(dest, ys):
    n_tok = dest.shape[1]
    width = ys.shape[1]
    win = SC_WINDOW
    assert n_tok % (win * SC_WORKERS) == 0
    per_worker = n_tok // win // SC_WORKERS
    mesh = plsc.VectorSubcoreMesh(core_axis_name="c", subcore_axis_name="s")

    part = win // SC_GATHER_PARTS
    n_items = TOP_K * SC_GATHER_PARTS
    n_buf = SC_GATHER_BUFFERS

    @pl.kernel(out_type=jax.ShapeDtypeStruct((TOP_K, n_tok, width), ys.dtype), mesh=mesh,
               scratch_types=[pltpu.VMEM((n_buf, part, width), ys.dtype), pltpu.VMEM((TOP_K, win), I32),
                              pltpu.SemaphoreType.DMA((2, n_buf))])
    def gather(ys_hbm, idx_hbm, g_hbm, buf, ibuf, sem):
        worker = lax.axis_index("c") * SC_SUBCORES + lax.axis_index("s")

        @pl.loop(0, per_worker)
        def _(j):
            t0 = pl.multiple_of((worker * per_worker + j) * win, win)
            pltpu.sync_copy(idx_hbm.at[:, pl.ds(t0, win)], ibuf)

            def fetch(n):
                k, h = divmod(n, SC_GATHER_PARTS)
                return pltpu.async_copy(ys_hbm.at[ibuf.at[k, pl.ds(h * part, part)]], buf.at[n % n_buf],
                                        sem.at[0, n % n_buf])

            def store(n):
                k, h = divmod(n, SC_GATHER_PARTS)
                return pltpu.async_copy(buf.at[n % n_buf], g_hbm.at[k, pl.ds(t0 + h * part, part)],
                                        sem.at[1, n % n_buf])

            fetches = {n: fetch(n) for n in range(n_buf - 1)}
            stores = {}
            for n in range(n_items):
                fetches[n].wait()
                stores[n] = store(n)
                nxt = n + n_buf - 1
                if nxt < n_items:
                    if n >= 1:
                        stores.pop(n - 1).wait()
                    fetches[nxt] = fetch(nxt)
            for pending_store in stores.values():
                pending_store.wait()

    return gather(ys, dest)


def _gather_combine(dest, gw, ys):
    n_tok = dest.shape[1]
    half = ys.shape[1]
    win = SC_WINDOW
    part = SC_COMBINE_PART
    assert n_tok % (win * SC_WORKERS) == 0
    per_worker = n_tok // win // SC_WORKERS
    mesh = plsc.VectorSubcoreMesh(core_axis_name="c", subcore_axis_name="s")

    @pl.kernel(out_type=jax.ShapeDtypeStruct((n_tok, 2 * half), F32), mesh=mesh,
               compiler_params=pltpu.CompilerParams(needs_layout_passes=False),
               scratch_types=[pltpu.VMEM((TOP_K, part, half), U32), pltpu.VMEM((part, 2 * half), F32),
                              pltpu.VMEM((TOP_K, win), I32), pltpu.VMEM((TOP_K, win), F32),
                              pltpu.SemaphoreType.DMA((TOP_K,))])
    def gather_sum(ys_hbm, idx_hbm, gw_hbm, o_hbm, rows, obuf, ibuf, wbuf, sem):
        worker = lax.axis_index("c") * SC_SUBCORES + lax.axis_index("s")

        @pl.loop(0, per_worker)
        def _(j):
            t0 = pl.multiple_of((worker * per_worker + j) * win, win)
            pltpu.sync_copy(idx_hbm.at[:, pl.ds(t0, win)], ibuf)
            pltpu.sync_copy(gw_hbm.at[:, pl.ds(t0, win)], wbuf)
            for p in range(win // part):
                fetches = [pltpu.async_copy(ys_hbm.at[ibuf.at[k, pl.ds(p * part, part)]], rows.at[k], sem.at[k])
                           for k in range(TOP_K)]
                for fetch in fetches:
                    fetch.wait()

                @pl.loop(0, part)
                def _(t):
                    col = jnp.full((SC_LANES,), p * part, I32) + t
                    wts = [plsc.load_gather(wbuf, [jnp.full((SC_LANES,), k, I32), col]) for k in range(TOP_K)]

                    @plsc.parallel_loop(0, half // SC_LANES, unroll=4)
                    def _(v):
                        words = pl.ds(v * SC_LANES, SC_LANES)
                        lo = jnp.zeros((SC_LANES,), F32)
                        hi = jnp.zeros((SC_LANES,), F32)
                        for k in range(TOP_K):
                            w = rows[k, t, words]
                            lo = lo + wts[k] * plsc.bitcast(w << 16, F32)
                            hi = hi + wts[k] * plsc.bitcast(w & jnp.uint32(0xFFFF0000), F32)
                        obuf[t, words] = lo
                        obuf[t, pl.ds(half + v * SC_LANES, SC_LANES)] = hi

                pltpu.sync_copy(obuf, o_hbm.at[pl.ds(t0 + p * part, part)])

    return gather_sum(ys, dest, gw)


def _final_kernel(g_ref, x_ref, gw_ref, gt_ref, fg_ref, *rest):
    o_ref = rest[-1]
    y = x_ref[...] + gt_ref[0] * _weighted_rows(g_ref, gw_ref[...].T)
    o_ref[...] = _rms(y, fg_ref[...])


def _final(out, g, x, gw, gt, final_g, b0, batch):
    bsz, seq, d = x.shape
    tm = min(FINAL_ROWS, seq)
    per_seq = seq // tm
    base = b0 * per_seq
    in_specs = [
        pl.BlockSpec((TOP_K, tm, d // 2), lambda i: (0, i, 0)),
        pl.BlockSpec((tm, d), lambda i: (i, 0)),
        pl.BlockSpec((TOP_K, tm), lambda i: (0, i)),
        pl.BlockSpec((1, 1, d), lambda i: (b0 + i // per_seq, 0, 0)),
        pl.BlockSpec((1, d), lambda i: (0, 0)),
    ]
    args = [g, x.reshape(bsz * seq, d), gw, gt, final_g.reshape(1, d)]
    aliases = {}
    if out is not None:
        in_specs.append(pl.BlockSpec(memory_space=pl.ANY))
        args.append(out)
        aliases = {len(args) - 1: 0}
    return pl.pallas_call(
        _final_kernel,
        grid=(bsz * per_seq,),
        in_specs=in_specs,
        out_specs=pl.BlockSpec((tm, d), lambda i: (base + i, 0)),
        out_shape=jax.ShapeDtypeStruct((batch * seq, d), F32),
        input_output_aliases=aliases,
        compiler_params=_params(1),
        name="final",
    )(*args)


def _plan_kernel(cnt_ref, start_ref, be_ref, nv_ref, live_ref, first_ref, slot_ref, next_ref):
    tb = EXPERT_ROWS
    n_blocks = be_ref.shape[0]

    def per_expert(e, carry):
        blk0, group = carry
        count = cnt_ref[e]
        n_blk = (count + (tb - 1)) // tb
        start_ref[e] = blk0 * tb

        def per_block(b, c):
            i = blk0 + b
            be_ref[i] = e
            live_ref[i] = jnp.minimum(count - b * tb, tb)
            first_ref[i] = jnp.where(b == 0, 1, 0)
            slot_ref[i] = group % 2
            return c

        lax.fori_loop(0, n_blk, per_block, 0)
        return blk0 + n_blk, group + jnp.where(n_blk > 0, 1, 0)

    used, _ = lax.fori_loop(0, N_EXPERTS, per_expert, (jnp.int32(0), jnp.int32(0)))
    nv_ref[0] = used

    def backwards(t, nxt):
        i = used - 1 - t
        after = be_ref[jnp.minimum(i + 1, used - 1)]
        nxt = jnp.where(jnp.logical_and(i < used - 1, after != be_ref[i]), after, nxt)
        next_ref[i] = nxt
        return nxt

    lax.fori_loop(0, used, backwards, jnp.int32(-1))
    last = be_ref[used - 1]

    def tail(i, c):
        be_ref[i] = last
        live_ref[i] = 0
        first_ref[i] = 0
        slot_ref[i] = 0
        next_ref[i] = -1
        return c

    lax.fori_loop(used, n_blocks, tail, 0)


def _plan(counts, n_blocks):
    smem = pl.BlockSpec(memory_space=pltpu.SMEM)
    per_block = jax.ShapeDtypeStruct((n_blocks,), I32)
    return pl.pallas_call(
        _plan_kernel,
        in_specs=[smem],
        out_specs=[smem] * 7,
        out_shape=[jax.ShapeDtypeStruct((N_EXPERTS,), I32), per_block, jax.ShapeDtypeStruct((1,), I32),
                   per_block, per_block, per_block, per_block],
        name="plan",
    )(counts)


def _moe(layer, routed, w_gu, b_gu, w_down, b_down, combine):
    n_tok = routed[0].shape[0]
    tb = EXPERT_ROWS
    step_rows = tb * EXPERT_BLOCKS_PER_STEP
    n_rows = -(-(n_tok * TOP_K + N_EXPERTS * (tb - 1)) // step_rows) * step_rows
    hpk, idx, gw, rank, cnt = routed
    pad_start, *plan = _plan(cnt[:, 0], n_rows // tb)
    dest = _dest(pad_start, idx, rank)
    xs = _dispatch(hpk, dest, n_rows)
    ys = _experts(layer, plan, xs, w_gu, b_gu, w_down, b_down)
    return _gather_combine(dest, gw, ys) if combine else _gather_rows(dest, ys)


def kernel(x, c, norm1_g, ada_w, ada_b, w_in, gmlp_norm_g, gmlp_ws, gmlp_bs, w_proj_a, pool_w, pool_scale,
           conv_w, w_proj_c, w_out, norm2_g, router_w, router_b, exp_w_gu, exp_b_gu, exp_w_down,
           exp_b_down, final_g):
    depth = ada_w.shape[0]
    bsz, seq, d = x.shape
    mods = _ada(c, ada_w, ada_b)
    group = bsz // BATCH_GROUPS
    streams = [x] * BATCH_GROUPS
    pending = [None] * BATCH_GROUPS
    g2 = None
    for l in range(depth):
        g2_prev = g2
        sh1, sc1, g1, sh2, sc2, g2 = [mods[l, :, i * d:(i + 1) * d].reshape(bsz, 1, d) for i in range(6)]
        routed = [None] * BATCH_GROUPS
        for h in range(BATCH_GROUPS):
            held = None if pending[h] is None else (pending[h], g2_prev)
            streams[h], *routed[h] = _mixer(
                streams[h], h * group if l == 0 else 0, h * group, group, held, sh1, sc1, g1, sh2, sc2,
                norm1_g[l], w_in[l], gmlp_norm_g[l], gmlp_ws[l], gmlp_bs[l], w_proj_a[l], pool_w[l],
                pool_scale[l], conv_w[l], w_proj_c[l], w_out[l], norm2_g[l], router_w[l], router_b[l])
        for h in range(BATCH_GROUPS):
            pending[h] = _moe(l, routed[h], exp_w_gu, exp_b_gu, exp_w_down, exp_b_down, l < depth - 1)
    out = None
    for h in range(BATCH_GROUPS):
        out = _final(out, pending[h], streams[h], routed[h][2], g2, final_g, h * group, bsz)
    return out.reshape(bsz, seq, d)
```
